```python
import math
import jax, jax.numpy as jnp
from jax import lax
import numpy as np

D_MODEL = 4096
BATCH = 4
SEQ = 2048
DEPTH = 2
DEC_BATCH = 128
DEC_SEQ = 1
PAST_LEN = 16384
PAGE_SIZE = 128

S5_WIDTH = D_MODEL // 2
S5_GROUP = 16
S5_GROUPS = S5_WIDTH // S5_GROUP
S5_STATE = 64
SGU_WIDTH = D_MODEL // 2
SGU_GROUPS = 8
SGU_CHUNK = 128
EVEN_PROJ = S5_WIDTH + 2 * SGU_WIDTH
GDN_HEADS = 32
GDN_DK = 128
GDN_DV = 128
GDN_QK = GDN_HEADS * GDN_DK
GDN_V = GDN_HEADS * GDN_DV
GDN_CONV = 4
GDN_CONV_CH = 2 * GDN_QK + GDN_V
GDN_PROJ = GDN_CONV_CH + GDN_V + 2 * GDN_HEADS
GDN_CHUNK = 64
PEER_HEADS = 8
PEER_NKEYS = 128
PEER_EXPERTS = PEER_NKEYS * PEER_NKEYS
PEER_TOPK = 16
PEER_DQ = 256
PEER_BLOCK = 64
PLE_DIM = 256
EPS = 1e-6

kernel_name = 'hybrid_s5_sgu_gdn_peer_step'


def rmsnorm(x, g):
    xf = x.astype(jnp.float32)
    y = xf * lax.rsqrt(jnp.mean(xf * xf, axis=-1, keepdims=True) + EPS)
    return (y * g.astype(jnp.float32)).astype(x.dtype)


def layernorm(x, g, b):
    xf = x.astype(jnp.float32)
    mu = jnp.mean(xf, axis=-1, keepdims=True)
    xc = xf - mu
    y = xc * lax.rsqrt(jnp.mean(xc * xc, axis=-1, keepdims=True) + EPS)
    return (y * g.astype(jnp.float32) + b.astype(jnp.float32)).astype(x.dtype)


def l2norm(x):
    xf = x.astype(jnp.float32)
    return xf * lax.rsqrt(jnp.sum(xf * xf, axis=-1, keepdims=True) + EPS)


def _linear_combine(e1, e2):
    a1, b1 = e1
    a2, b2 = e2
    return a1 * a2, a2 * b1 + b2


def s5_mixer(xa, h0_re, h0_im, a_re, a_im, log_dt, b_re, b_im, c_re, c_im, d_skip, w_glu):
    f32 = jnp.float32
    bsz, L, _ = xa.shape
    u = xa.astype(f32).reshape(bsz, L, S5_GROUPS, S5_GROUP)
    lam = lax.complex(a_re.astype(f32), a_im.astype(f32))
    dt = jnp.exp(log_dt.astype(f32))[:, None]
    lam_bar = jnp.exp(lam * dt)
    b_bar = ((lam_bar - 1.0) / lam)[..., None] * lax.complex(b_re.astype(f32), b_im.astype(f32))
    c_mat = lax.complex(c_re.astype(f32), c_im.astype(f32))
    bu = jnp.einsum('gpc,blgc->blgp', b_bar, u.astype(jnp.complex64))
    h0 = lax.complex(h0_re.astype(f32), h0_im.astype(f32))
    bu = bu.at[:, 0].add(lam_bar * h0)
    a = jnp.broadcast_to(lam_bar, bu.shape)
    _, h = lax.associative_scan(_linear_combine, (a, bu), axis=1)
    y = jnp.einsum('gcp,blgp->blgc', c_mat, h).real + d_skip.astype(f32).reshape(S5_GROUPS, S5_GROUP) * u
    z = jax.nn.gelu(y.reshape(bsz, L, S5_WIDTH)).astype(xa.dtype)
    out = z * jax.nn.sigmoid(z @ w_glu)
    h_last = h[:, -1]
    return out, jnp.real(h_last), jnp.imag(h_last)


def sgu_mixer(u, v, ln_g, ln_b, w_s, b_s):
    bsz, L, _ = u.shape
    u = jax.nn.gelu(u)
    v = layernorm(jax.nn.gelu(v), ln_g, ln_b)
    c = min(L, SGU_CHUNK)
    n = -(-L // c)
    pad = n * c - L
    dg = SGU_WIDTH // SGU_GROUPS
    vc = jnp.pad(v, ((0, 0), (0, pad), (0, 0))).reshape(bsz, n, c, SGU_GROUPS, dg)
    causal = jnp.tril(jnp.ones((c, c), dtype=bool))
    w = jnp.where(causal, w_s[:, :c, :c], jnp.zeros((), w_s.dtype))
    mixed = jnp.einsum('gts,bnsgd->bntgd', w, vc) + jnp.swapaxes(b_s[:, :c], 0, 1)[:, :, None]
    mixed = mixed.reshape(bsz, n * c, SGU_WIDTH)[:, :L]
    start = ((L - 1) // SGU_CHUNK) * SGU_CHUNK
    return u * mixed, v[:, start:]


def causal_conv(x, buf, w):
    L = x.shape[1]
    xc = jnp.concatenate([buf.astype(x.dtype), x], axis=1)
    out = xc[:, 0:L] * w[0]
    for tap in range(1, GDN_CONV):
        out = out + xc[:, tap:tap + L] * w[tap]
    return out, xc[:, L:]


def gated_delta_rule(q, k, v, g, beta, s0):
    f32 = jnp.float32
    bsz, L = q.shape[:2]
    c = min(L, GDN_CHUNK)
    n = -(-L // c)
    pad = n * c - L

    def chunks(t):
        t = jnp.pad(t, [(0, 0), (0, pad)] + [(0, 0)] * (t.ndim - 2))
        t = t.reshape((bsz, n, c) + t.shape[2:])
        return jnp.moveaxis(t, (1, 2), (0, 3))

    q, k, v, g, beta = chunks(q), chunks(k), chunks(v), chunks(g), chunks(beta)
    gc = jnp.cumsum(g, axis=-1)
    kb = k * beta[..., None]
    vb = v * beta[..., None]
    incl = jnp.tril(jnp.ones((c, c), dtype=bool))
    decay = jnp.exp(jnp.where(incl, gc[..., :, None] - gc[..., None, :], -jnp.inf))
    strict = jnp.tril(jnp.ones((c, c), f32), -1)
    lmat = jnp.eye(c, dtype=f32) + jnp.einsum('...id,...jd->...ij', kb, k) * decay * strict
    u = lax.linalg.triangular_solve(lmat, vb, left_side=True, lower=True, unit_diagonal=True)
    w = lax.linalg.triangular_solve(lmat, kb * jnp.exp(gc)[..., None], left_side=True, lower=True, unit_diagonal=True)
    attn = jnp.einsum('...id,...jd->...ij', q, k) * decay

    def step(s, xs):
        q_i, k_i, u_i, w_i, gc_i, a_i = xs
        v_new = u_i - jnp.einsum('bhck,bhkv->bhcv', w_i, s)
        o = jnp.einsum('bhck,bhkv->bhcv', q_i * jnp.exp(gc_i)[..., None], s) + jnp.einsum('bhcs,bhsv->bhcv', a_i, v_new)
        g_last = gc_i[..., -1]
        s = s * jnp.exp(g_last)[..., None, None] + jnp.einsum('bhck,bhcv->bhkv', k_i * jnp.exp(g_last[..., None] - gc_i)[..., None], v_new)
        return s, o

    s_final, o = lax.scan(step, s0, (q, k, u, w, gc, attn))
    o = jnp.moveaxis(o, (0, 3), (1, 2)).reshape(bsz, n * c, GDN_HEADS, GDN_DV)[:, :L]
    return o, s_final


def gdn_mixer(h, s0, conv_buf, w_in, conv_w, a_log, dt_bias, norm_g, w_out):
    f32 = jnp.float32
    bsz, L, _ = h.shape
    proj = h @ w_in
    qkv, z, a, b = jnp.split(proj, [GDN_CONV_CH, GDN_CONV_CH + GDN_V, GDN_CONV_CH + GDN_V + GDN_HEADS], axis=-1)
    qkv, conv_new = causal_conv(qkv, conv_buf, conv_w)
    qkv = jax.nn.silu(qkv)
    q, k, v = jnp.split(qkv, [GDN_QK, 2 * GDN_QK], axis=-1)
    q = l2norm(q.reshape(bsz, L, GDN_HEADS, GDN_DK)) * (GDN_DK ** -0.5)
    k = l2norm(k.reshape(bsz, L, GDN_HEADS, GDN_DK))
    v = v.reshape(bsz, L, GDN_HEADS, GDN_DV).astype(f32)
    beta = jax.nn.sigmoid(b.astype(f32))
    g = -jnp.exp(a_log.astype(f32)) * jax.nn.softplus(a.astype(f32) + dt_bias.astype(f32))
    o, s_new = gated_delta_rule(q, k, v, g, beta, s0.astype(f32))
    o = rmsnorm(o, norm_g) * jax.nn.silu(z.reshape(bsz, L, GDN_HEADS, GDN_DV).astype(f32))
    y = o.reshape(bsz, L, GDN_V).astype(h.dtype) @ w_out
    return y, s_new, conv_new


def peer_ffn(h, w_q, sub_keys, emb_u, emb_v):
    t = h.shape[0]
    q = (h @ w_q).reshape(t, PEER_HEADS, 2, PEER_DQ // 2)
    s = jnp.einsum('thpd,hpnd->thpn', q, sub_keys).astype(jnp.float32)
    s1, i1 = lax.top_k(s[:, :, 0], PEER_TOPK)
    s2, i2 = lax.top_k(s[:, :, 1], PEER_TOPK)
    cand_s = (s1[..., :, None] + s2[..., None, :]).reshape(t, PEER_HEADS, PEER_TOPK * PEER_TOPK)
    cand_i = (i1[..., :, None] * PEER_NKEYS + i2[..., None, :]).reshape(t, PEER_HEADS, PEER_TOPK * PEER_TOPK)
    top_s, pos = lax.top_k(cand_s, PEER_TOPK)
    idx = jnp.take_along_axis(cand_i, pos, axis=-1)
    gate = jax.nn.softmax(top_s, axis=-1).astype(h.dtype)
    pad = (-t) % PEER_BLOCK
    nb = (t + pad) // PEER_BLOCK
    hp = jnp.pad(h, ((0, pad), (0, 0))).reshape(nb, PEER_BLOCK, D_MODEL)
    ip = jnp.pad(idx, ((0, pad), (0, 0), (0, 0))).reshape(nb, PEER_BLOCK, PEER_HEADS, PEER_TOPK)
    gp = jnp.pad(gate, ((0, pad), (0, 0), (0, 0))).reshape(nb, PEER_BLOCK, PEER_HEADS, PEER_TOPK)

    def block(args):
        hb, ib, gb = args
        act = jax.nn.gelu(jnp.einsum('bd,bhkd->bhk', hb, emb_u[ib])) * gb
        return jnp.einsum('bhk,bhkd->bd', act, emb_v[ib])

    return lax.map(block, (hp, ip, gp)).reshape(nb * PEER_BLOCK, D_MODEL)[:t]


def trunk(x, p, s5_re, s5_im, gdn_state, gdn_conv, w):
    bsz, seq, _ = x.shape
    out_re, out_im, out_v, out_gdn, out_conv = [], [], [], [], []
    for i in range(DEPTH):
        j = i // 2
        h = rmsnorm(x, w['norm_mix'][i])
        if i % 2 == 0:
            proj = h @ w['ev_w_in'][j]
            xa, u, v = jnp.split(proj, [S5_WIDTH, S5_WIDTH + SGU_WIDTH], axis=-1)
            ya, hr, hi = s5_mixer(xa, s5_re[j], s5_im[j], w['s5_a_re'][j], w['s5_a_im'][j], w['s5_log_dt'][j],
                                  w['s5_b_re'][j], w['s5_b_im'][j], w['s5_c_re'][j], w['s5_c_im'][j],
                                  w['s5_d'][j], w['s5_w_glu'][j])
            yb, v_rows = sgu_mixer(u, v, w['sgu_ln_g'][j], w['sgu_ln_b'][j], w['sgu_w'][j], w['sgu_b'][j])
            mix = jnp.concatenate([ya, yb], axis=-1) @ w['ev_w_out'][j]
            out_re.append(hr)
            out_im.append(hi)
            out_v.append(v_rows)
        else:
            mix, s_new, buf_new = gdn_mixer(h, gdn_state[j], gdn_conv[j], w['gdn_w_in'][j], w['gdn_conv_w'][j],
                                            w['gdn_a_log'][j], w['gdn_dt_bias'][j], w['gdn_norm_g'][j],
                                            w['gdn_w_out'][j])
            out_gdn.append(s_new)
            out_conv.append(buf_new)
        x = x + mix
        h = rmsnorm(x, w['norm_ffn'][i])
        x = x + peer_ffn(h.reshape(bsz * seq, D_MODEL), w['peer_w_q'][i], w['peer_keys'][i],
                         w['peer_u'][i], w['peer_v'][i]).reshape(bsz, seq, D_MODEL)
        gate = jax.nn.sigmoid(rmsnorm(x, w['norm_ple'][i]) @ w['ple_gate'][i])
        x = x + gate * (p[i].astype(x.dtype) @ w['ple_proj'][i])
    y = rmsnorm(x, w['norm_final'])
    return y, jnp.stack(out_re), jnp.stack(out_im), jnp.stack(out_v), jnp.stack(out_gdn), jnp.stack(out_conv)


def setup_inputs(seed: int = 0) -> dict:
    f32 = jnp.float32
    n_even = (DEPTH + 1) // 2
    n_odd = DEPTH // 2
    keys = list(jax.random.split(jax.random.key(seed), 48))

    def normal(shape, scale):
        return jax.random.normal(keys.pop(), shape, f32) * scale

    def uniform(shape, lo, hi):
        return jax.random.uniform(keys.pop(), shape, f32, lo, hi)

    def gain(shape):
        return 1.0 + normal(shape, 0.02)

    dt_gdn = jnp.exp(uniform((n_odd, GDN_HEADS), math.log(1e-3), math.log(1e-1)))
    return {
        'x_prompt': normal((BATCH, SEQ, D_MODEL), 1.0),
        'x_sample': normal((DEC_BATCH, DEC_SEQ, D_MODEL), 1.0),
        'p_prompt': normal((DEPTH, BATCH, SEQ, PLE_DIM), 1.0),
        'p_sample': normal((DEPTH, DEC_BATCH, DEC_SEQ, PLE_DIM), 1.0),
        'state_s5_re': normal((n_even, DEC_BATCH, S5_GROUPS, S5_STATE), 0.1),
        'state_s5_im': normal((n_even, DEC_BATCH, S5_GROUPS, S5_STATE), 0.1),
        'state_gdn': normal((n_odd, DEC_BATCH, GDN_HEADS, GDN_DK, GDN_DV), 0.05),
        'state_gdn_conv': normal((n_odd, DEC_BATCH, GDN_CONV - 1, GDN_CONV_CH), 1.0),
        'norm_mix': gain((DEPTH, D_MODEL)),
        'norm_ffn': gain((DEPTH, D_MODEL)),
        'norm_ple': gain((DEPTH, D_MODEL)),
        'norm_final': gain((D_MODEL,)),
        'ev_w_in': normal((n_even, D_MODEL, EVEN_PROJ), D_MODEL ** -0.5),
        's5_a_re': -0.5 + normal((n_even, S5_GROUPS, S5_STATE), 0.01),
        's5_a_im': jnp.pi * jnp.arange(S5_STATE, dtype=f32) + normal((n_even, S5_GROUPS, S5_STATE), 0.01),
        's5_log_dt': uniform((n_even, S5_GROUPS), math.log(1e-3), math.log(1e-1)),
        's5_b_re': normal((n_even, S5_GROUPS, S5_STATE, S5_GROUP), (2 * S5_GROUP) ** -0.5),
        's5_b_im': normal((n_even, S5_GROUPS, S5_STATE, S5_GROUP), (2 * S5_GROUP) ** -0.5),
        's5_c_re': normal((n_even, S5_GROUPS, S5_GROUP, S5_STATE), (2 * S5_STATE) ** -0.5),
        's5_c_im': normal((n_even, S5_GROUPS, S5_GROUP, S5_STATE), (2 * S5_STATE) ** -0.5),
        's5_d': normal((n_even, S5_WIDTH), 1.0),
        's5_w_glu': normal((n_even, S5_WIDTH, S5_WIDTH), S5_WIDTH ** -0.5),
        'sgu_ln_g': gain((n_even, SGU_WIDTH)),
        'sgu_ln_b': normal((n_even, SGU_WIDTH), 0.02),
        'sgu_w': normal((n_even, SGU_GROUPS, SGU_CHUNK, SGU_CHUNK), SGU_CHUNK ** -0.5),
        'sgu_b': 1.0 + normal((n_even, SGU_GROUPS, SGU_CHUNK), 0.1),
        'ev_w_out': normal((n_even, S5_WIDTH + SGU_WIDTH, D_MODEL), (S5_WIDTH + SGU_WIDTH) ** -0.5),
        'gdn_w_in': normal((n_odd, D_MODEL, GDN_PROJ), D_MODEL ** -0.5),
        'gdn_conv_w': normal((n_odd, GDN_CONV, GDN_CONV_CH), GDN_CONV ** -0.5),
        'gdn_a_log': jnp.log(uniform((n_odd, GDN_HEADS), 1.0, 16.0)),
        'gdn_dt_bias': dt_gdn + jnp.log(-jnp.expm1(-dt_gdn)),
        'gdn_norm_g': gain((n_odd, GDN_DV)),
        'gdn_w_out': normal((n_odd, GDN_V, D_MODEL), GDN_V ** -0.5),
        'peer_w_q': normal((DEPTH, D_MODEL, PEER_HEADS * PEER_DQ), D_MODEL ** -0.5),
        'peer_keys': normal((DEPTH, PEER_HEADS, 2, PEER_NKEYS, PEER_DQ // 2), (PEER_DQ // 2) ** -0.5),
        'peer_u': normal((DEPTH, PEER_EXPERTS, D_MODEL), D_MODEL ** -0.5),
        'peer_v': normal((DEPTH, PEER_EXPERTS, D_MODEL), PEER_HEADS ** -0.5),
        'ple_proj': normal((DEPTH, PLE_DIM, D_MODEL), PLE_DIM ** -0.5),
        'ple_gate': normal((DEPTH, D_MODEL, D_MODEL), D_MODEL ** -0.5),
    }


def reference(x_prompt, x_sample, p_prompt, p_sample, state_s5_re, state_s5_im, state_gdn, state_gdn_conv,
              norm_mix, norm_ffn, norm_ple, norm_final, ev_w_in, s5_a_re, s5_a_im, s5_log_dt, s5_b_re, s5_b_im,
              s5_c_re, s5_c_im, s5_d, s5_w_glu, sgu_ln_g, sgu_ln_b, sgu_w, sgu_b, ev_w_out, gdn_w_in, gdn_conv_w,
              gdn_a_log, gdn_dt_bias, gdn_norm_g, gdn_w_out, peer_w_q, peer_keys, peer_u, peer_v, ple_proj, ple_gate):
    w = {
        'norm_mix': norm_mix, 'norm_ffn': norm_ffn, 'norm_ple': norm_ple, 'norm_final': norm_final,
        'ev_w_in': ev_w_in, 's5_a_re': s5_a_re, 's5_a_im': s5_a_im, 's5_log_dt': s5_log_dt,
        's5_b_re': s5_b_re, 's5_b_im': s5_b_im, 's5_c_re': s5_c_re, 's5_c_im': s5_c_im, 's5_d': s5_d,
        's5_w_glu': s5_w_glu, 'sgu_ln_g': sgu_ln_g, 'sgu_ln_b': sgu_ln_b, 'sgu_w': sgu_w, 'sgu_b': sgu_b,
        'ev_w_out': ev_w_out, 'gdn_w_in': gdn_w_in, 'gdn_conv_w': gdn_conv_w, 'gdn_a_log': gdn_a_log,
        'gdn_dt_bias': gdn_dt_bias, 'gdn_norm_g': gdn_norm_g, 'gdn_w_out': gdn_w_out,
        'peer_w_q': peer_w_q, 'peer_keys': peer_keys, 'peer_u': peer_u, 'peer_v': peer_v,
        'ple_proj': ple_proj, 'ple_gate': ple_gate,
    }
    f32 = jnp.float32
    n_even = (DEPTH + 1) // 2
    n_odd = DEPTH // 2
    bp = x_prompt.shape[0]
    zero_s5 = jnp.zeros((n_even, bp, S5_GROUPS, S5_STATE), f32)
    zero_gdn = jnp.zeros((n_odd, bp, GDN_HEADS, GDN_DK, GDN_DV), f32)
    zero_conv = jnp.zeros((n_odd, bp, GDN_CONV - 1, GDN_CONV_CH), x_prompt.dtype)
    y_prompt, s5r_p, s5i_p, sgu_p, gdn_p, conv_p = trunk(x_prompt, p_prompt, zero_s5, zero_s5, zero_gdn, zero_conv, w)
    y_sample, s5r_s, s5i_s, sgu_s, gdn_s, conv_s = trunk(x_sample, p_sample, state_s5_re, state_s5_im,
                                                         state_gdn, state_gdn_conv, w)
    return (y_prompt, y_sample, s5r_p, s5i_p, s5r_s, s5i_s, sgu_p, sgu_s, gdn_p, gdn_s, conv_p, conv_s)
```

```python
import functools

import jax
import jax.numpy as jnp
from jax import lax
from jax.experimental import pallas as pl
from jax.experimental.pallas import tpu as pltpu

F32 = jnp.float32
BF16 = jnp.bfloat16
EPS = 1e-6
LANE = 128
V7X_VMEM_BYTES = 64 * 1024 * 1024
VMEM_LIMIT = V7X_VMEM_BYTES - 8 * 1024 * 1024

S5_GROUP = 16
S5_STATE = 64
SGU_GROUPS = 8
SGU_CHUNK = 128
GDN_HEADS = 32
GDN_DK = 128
GDN_CONV = 4
GDN_CHUNK = 64
PEER_HEADS = 8
PEER_NKEYS = 128
PEER_TOPK = 16

TOK_TILE = 640
ROW_TILE = 128


def _params(sem, vmem=VMEM_LIMIT):
    return pltpu.CompilerParams(dimension_semantics=sem, vmem_limit_bytes=vmem)


def _dot(a, b):
    return jnp.dot(a, b, preferred_element_type=F32)


def _dot_nt(a, b):
    return lax.dot_general(a, b, (((1,), (1,)), ((), ())), preferred_element_type=F32)


def _dot_tn(a, b):
    return lax.dot_general(a, b, (((0,), (0,)), ((), ())), preferred_element_type=F32)


def _rms(x, g):
    return x * lax.rsqrt(jnp.mean(x * x, axis=-1, keepdims=True) + EPS) * g


def _silu(x):
    return x * jax.nn.sigmoid(x)


def _rmsnorm_kernel(x_ref, g_ref, o_ref, *, transpose):
    y = _rms(x_ref[...], g_ref[...])
    if transpose:
        y = y.T
    o_ref[...] = y.astype(o_ref.dtype)


def rmsnorm(x, g, out_dtype, transpose=False):
    t, d = x.shape
    tm = ROW_TILE
    if transpose:
        out_shape, out_spec = (d, t), pl.BlockSpec((d, tm), lambda i: (0, i))
    else:
        out_shape, out_spec = (t, d), pl.BlockSpec((tm, d), lambda i: (i, 0))
    return pl.pallas_call(
        functools.partial(_rmsnorm_kernel, transpose=transpose),
        grid=(t // tm,),
        in_specs=[pl.BlockSpec((tm, d), lambda i: (i, 0)), pl.BlockSpec((1, d), lambda i: (0, 0))],
        out_specs=out_spec,
        out_shape=jax.ShapeDtypeStruct(out_shape, out_dtype),
        compiler_params=_params(("parallel",)),
        name="rmsnorm_t" if transpose else "rmsnorm",
    )(x, g.reshape(1, d))


def _add_t_norm_kernel(x_ref, ot_ref, g_ref, x2_ref, h_ref):
    x2 = x_ref[...] + ot_ref[...].T
    x2_ref[...] = x2
    h_ref[...] = _rms(x2, g_ref[...]).astype(h_ref.dtype)


def add_t_norm(x, o_t, g):
    t, d = x.shape
    tm = ROW_TILE
    return pl.pallas_call(
        _add_t_norm_kernel,
        grid=(t // tm,),
        in_specs=[pl.BlockSpec((tm, d), lambda i: (i, 0)), pl.BlockSpec((d, tm), lambda i: (0, i)),
                  pl.BlockSpec((1, d), lambda i: (0, 0))],
        out_specs=[pl.BlockSpec((tm, d), lambda i: (i, 0)), pl.BlockSpec((tm, d), lambda i: (i, 0))],
        out_shape=[jax.ShapeDtypeStruct((t, d), F32), jax.ShapeDtypeStruct((t, d), BF16)],
        compiler_params=_params(("parallel",)),
        name="add_t_norm",
    )(x, o_t, g.reshape(1, d))


def _mm_kernel(*refs, mode):
    x_ref, w_ref = refs[0], refs[1]
    o_ref = refs[-1]
    acc = _dot(x_ref[...], w_ref[...])
    if mode == "resid":
        acc = refs[2][...] + acc
    elif mode == "glu":
        z = refs[2][...]
        acc = z * jax.nn.sigmoid(acc)
    elif mode == "ple":
        r_ref, p_ref, pw_ref = refs[2], refs[3], refs[4]
        acc = r_ref[...] + jax.nn.sigmoid(acc) * _dot(p_ref[...], pw_ref[...])
    o_ref[...] = acc.astype(o_ref.dtype)


def matmul(x, w, *, mode="plain", extras=(), out_dtype=F32, tn=1024):
    m, k = x.shape
    n = w.shape[1]
    tm = TOK_TILE
    tn = min(tn, n)
    in_specs = [pl.BlockSpec((tm, k), lambda j, i: (i, 0)), pl.BlockSpec((k, tn), lambda j, i: (0, j))]
    if mode in ("resid", "glu", "ple"):
        in_specs.append(pl.BlockSpec((tm, tn), lambda j, i: (i, j)))
    if mode == "ple":
        kp = extras[1].shape[1]
        in_specs += [pl.BlockSpec((tm, kp), lambda j, i: (i, 0)), pl.BlockSpec((kp, tn), lambda j, i: (0, j))]
    return pl.pallas_call(
        functools.partial(_mm_kernel, mode=mode),
        grid=(n // tn, m // tm),
        in_specs=in_specs,
        out_specs=pl.BlockSpec((tm, tn), lambda j, i: (i, j)),
        out_shape=jax.ShapeDtypeStruct((m, n), out_dtype),
        compiler_params=_params(("parallel", "parallel")),
        name="mm_" + mode,
    )(x, w, *extras)


def _s5_prompt_kernel(x0, x1, x2, x3, bblk_ref, cblk_ref, lre_ref, lim_ref, d_ref,
                      z_ref, hre_out, him_out, bure, buim, hre, him, st_re, st_im, *, c):
    tc = pl.program_id(1)

    @pl.when(tc == 0)
    def _():
        st_re[...] = jnp.zeros_like(st_re)
        st_im[...] = jnp.zeros_like(st_im)

    xs = (x0, x1, x2, x3)
    nq = bure.shape[0]
    half_w = nq * LANE
    for b in range(4):
        xb = xs[b][...]
        for half in range(2):
            s = 2 * b + half
            bu = _dot(xb[:, half * LANE:(half + 1) * LANE], bblk_ref[half])
            for q in range(nq):
                bure[q, s * c:(s + 1) * c, :] = bu[:, q * LANE:(q + 1) * LANE]
                buim[q, s * c:(s + 1) * c, :] = bu[:, half_w + q * LANE:half_w + (q + 1) * LANE]
    for q in range(nq):
        ls = slice(q * LANE, (q + 1) * LANE)
        lr = lre_ref[0, :, ls]
        li = lim_ref[0, :, ls]
        hr = st_re[:, ls]
        hi = st_im[:, ls]
        for t in range(c):
            br = bure[q, pl.ds(t, 8, stride=c), :]
            bi = buim[q, pl.ds(t, 8, stride=c), :]
            nr = lr * hr - li * hi + br
            ni = lr * hi + li * hr + bi
            hre[q, pl.ds(t, 8, stride=c), :] = nr
            him[q, pl.ds(t, 8, stride=c), :] = ni
            hr, hi = nr, ni
        st_re[:, ls] = hr
        st_im[:, ls] = hi
    hre_out[0] = st_re[...]
    him_out[0] = st_im[...]
    for b in range(4):
        xb = xs[b][...]
        for half in range(2):
            s = 2 * b + half
            ls = slice(half * LANE, (half + 1) * LANE)
            h_re = jnp.concatenate([hre[q, s * c:(s + 1) * c, :] for q in range(nq)], axis=1)
            h_im = jnp.concatenate([him[q, s * c:(s + 1) * c, :] for q in range(nq)], axis=1)
            y = (_dot(h_re, cblk_ref[half, :half_w, :]) + _dot(h_im, cblk_ref[half, half_w:, :])
                 + d_ref[:, ls] * xb[:, ls])
            z_ref[b, :, ls] = jax.nn.gelu(y)


def _s5_sample_kernel(x_ref, h0re_ref, h0im_ref, bblk_ref, cblk_ref, lre_ref, lim_ref, d_ref,
                      z_ref, hre_out, him_out):
    x = x_ref[...]
    half_w = h0re_ref.shape[1]
    bu = _dot(x, bblk_ref[0])
    lr = lre_ref[0]
    li = lim_ref[0]
    h0r = h0re_ref[...]
    h0i = h0im_ref[...]
    nr = lr * h0r - li * h0i + bu[:, :half_w]
    ni = lr * h0i + li * h0r + bu[:, half_w:]
    hre_out[...] = nr
    him_out[...] = ni
    y = _dot(nr, cblk_ref[0, :half_w, :]) + _dot(ni, cblk_ref[0, half_w:, :]) + d_ref[...] * x
    z_ref[...] = jax.nn.gelu(y)


def _s5_params(a_re, a_im, log_dt, b_re, b_im, c_re, c_im):
    g, p = a_re.shape
    nb = g // 8
    lam = lax.complex(a_re, a_im)
    dt = jnp.exp(log_dt)[:, None]
    lam_bar = jnp.exp(lam * dt)
    b_bar = ((lam_bar - 1.0) / lam)[..., None] * lax.complex(b_re, b_im)
    eye = jnp.eye(8, dtype=F32)

    def bmat(v):
        v = jnp.swapaxes(v, 1, 2).reshape(nb, 8, S5_GROUP, p)
        return jnp.einsum("kjcp,jl->kjclp", v, eye).reshape(nb, 8 * S5_GROUP, 8 * p)

    def cmat(v):
        v = jnp.swapaxes(v, 1, 2).reshape(nb, 8, p, S5_GROUP)
        return jnp.einsum("kjpc,jl->kjplc", v, eye).reshape(nb, 8 * p, 8 * S5_GROUP)

    bblk = jnp.concatenate([bmat(jnp.real(b_bar)), bmat(jnp.imag(b_bar))], axis=2)
    cblk = jnp.concatenate([cmat(c_re), cmat(-c_im)], axis=1)
    lre = jnp.real(lam_bar).reshape(nb, 8 * p)
    lim = jnp.imag(lam_bar).reshape(nb, 8 * p)
    return bblk, cblk, lre, lim


def s5_prompt(proj, n_prompt, seq, params, d_skip, c=128):
    bblk, cblk, lre, lim = params
    nb, _, two_w = bblk.shape
    half_w = two_w // 2
    bsz = n_prompt // seq
    assert bsz == 4
    w = nb * 8 * S5_GROUP
    ngb = nb // 2
    nt = seq // c
    lre_t = jnp.tile(lre.reshape(ngb, 2, half_w), (1, bsz, 1))
    lim_t = jnp.tile(lim.reshape(ngb, 2, half_w), (1, bsz, 1))
    x_specs = [pl.BlockSpec((c, 2 * LANE), functools.partial(lambda gb, tc, b: (b * nt + tc, gb), b=b))
               for b in range(bsz)]
    z, hre, him = pl.pallas_call(
        functools.partial(_s5_prompt_kernel, c=c),
        grid=(ngb, nt),
        in_specs=x_specs + [
            pl.BlockSpec((2, LANE, two_w), lambda gb, tc: (gb, 0, 0)),
            pl.BlockSpec((2, two_w, LANE), lambda gb, tc: (gb, 0, 0)),
            pl.BlockSpec((1, 8, half_w), lambda gb, tc: (gb, 0, 0)),
            pl.BlockSpec((1, 8, half_w), lambda gb, tc: (gb, 0, 0)),
            pl.BlockSpec((1, 2 * LANE), lambda gb, tc: (0, gb)),
        ],
        out_specs=[
            pl.BlockSpec((bsz, c, 2 * LANE), lambda gb, tc: (0, tc, gb)),
            pl.BlockSpec((1, 8, half_w), lambda gb, tc: (gb, 0, 0)),
            pl.BlockSpec((1, 8, half_w), lambda gb, tc: (gb, 0, 0)),
        ],
        out_shape=[
            jax.ShapeDtypeStruct((bsz, seq, w), F32),
            jax.ShapeDtypeStruct((ngb, 8, half_w), F32),
            jax.ShapeDtypeStruct((ngb, 8, half_w), F32),
        ],
        scratch_shapes=[pltpu.VMEM((half_w // LANE, 8 * c, LANE), F32) for _ in range(4)]
        + [pltpu.VMEM((8, half_w), F32), pltpu.VMEM((8, half_w), F32)],
        compiler_params=_params(("parallel", "arbitrary")),
        name="s5_prompt",
    )(proj, proj, proj, proj, bblk, cblk, lre_t, lim_t, d_skip.reshape(1, w))

    def states(h):
        h = h.reshape(ngb, bsz, 2, 8, S5_STATE)
        return jnp.transpose(h, (1, 0, 2, 3, 4)).reshape(bsz, nb * 8, S5_STATE)

    return z.reshape(n_prompt, w), states(hre), states(him)


def s5_sample(proj, n_prompt, h0_re, h0_im, params, d_skip):
    bblk, cblk, lre, lim = params
    nb, _, two_w = bblk.shape
    half_w = two_w // 2
    ns = h0_re.shape[0]
    w = nb * 8 * S5_GROUP
    row_blk = n_prompt // ns
    z, hre, him = pl.pallas_call(
        _s5_sample_kernel,
        grid=(nb,),
        in_specs=[
            pl.BlockSpec((ns, LANE), lambda k: (row_blk, k)),
            pl.BlockSpec((ns, half_w), lambda k: (0, k)),
            pl.BlockSpec((ns, half_w), lambda k: (0, k)),
            pl.BlockSpec((1, LANE, two_w), lambda k: (k, 0, 0)),
            pl.BlockSpec((1, two_w, LANE), lambda k: (k, 0, 0)),
            pl.BlockSpec((1, 1, half_w), lambda k: (k, 0, 0)),
            pl.BlockSpec((1, 1, half_w), lambda k: (k, 0, 0)),
            pl.BlockSpec((1, LANE), lambda k: (0, k)),
        ],
        out_specs=[
            pl.BlockSpec((ns, LANE), lambda k: (0, k)),
            pl.BlockSpec((ns, half_w), lambda k: (0, k)),
            pl.BlockSpec((ns, half_w), lambda k: (0, k)),
        ],
        out_shape=[
            jax.ShapeDtypeStruct((ns, w), F32),
            jax.ShapeDtypeStruct((ns, nb * half_w), F32),
            jax.ShapeDtypeStruct((ns, nb * half_w), F32),
        ],
        compiler_params=_params(("parallel",)),
        name="s5_sample",
    )(proj, h0_re.reshape(ns, -1), h0_im.reshape(ns, -1), bblk, cblk,
      lre.reshape(nb, 1, half_w), lim.reshape(nb, 1, half_w), d_skip.reshape(1, w))
    return z, hre.reshape(ns, nb * 8, S5_STATE), him.reshape(ns, nb * 8, S5_STATE)


def _sgu_kernel(u_ref, v_ref, g_ref, b_ref, w_ref, bias_ref, y_ref, vl_ref):
    u = jax.nn.gelu(u_ref[...])
    v = jax.nn.gelu(v_ref[...])
    mu = jnp.mean(v, axis=-1, keepdims=True)
    vc = v - mu
    vn = vc * lax.rsqrt(jnp.mean(vc * vc, axis=-1, keepdims=True) + EPS) * g_ref[...] + b_ref[...]
    vl_ref[...] = vn
    c = u.shape[0]
    dg = u.shape[1] // SGU_GROUPS
    causal = lax.broadcasted_iota(jnp.int32, (c, c), 0) >= lax.broadcasted_iota(jnp.int32, (c, c), 1)
    for g in range(SGU_GROUPS):
        ls = slice(g * dg, (g + 1) * dg)
        w = jnp.where(causal, w_ref[0, g], 0.0).astype(BF16)
        mixed = _dot(w, vn[:, ls].astype(BF16)) + bias_ref[0, g]
        y_ref[:, ls] = (u[:, ls] * mixed).astype(y_ref.dtype)


def sgu(proj, n_prompt, seq, ln_g, ln_b, w_s, b_s):
    t = proj.shape[0]
    wdt = proj.shape[1] // 3
    c = SGU_CHUNK
    dg = wdt // SGU_GROUPS
    n_chunks = t // c
    per_seq = seq // c
    n_prompt_chunks = n_prompt // c
    eye = jnp.eye(c, dtype=F32)
    w_sets = jnp.stack([w_s, w_s[:, :1, :1] * eye])
    bias_sets = jnp.stack([jnp.broadcast_to(b_s[:, :, None], (SGU_GROUPS, c, dg)),
                           jnp.broadcast_to(b_s[:, :1, None], (SGU_GROUPS, c, dg))])
    n_last = n_prompt // seq + (n_chunks - n_prompt_chunks)
    y, vl = pl.pallas_call(
        _sgu_kernel,
        grid=(n_chunks,),
        in_specs=[
            pl.BlockSpec((c, wdt), lambda i: (i, 1)),
            pl.BlockSpec((c, wdt), lambda i: (i, 2)),
            pl.BlockSpec((1, wdt), lambda i: (0, 0)),
            pl.BlockSpec((1, wdt), lambda i: (0, 0)),
            pl.BlockSpec((1, SGU_GROUPS, c, c), lambda i: (i // n_prompt_chunks, 0, 0, 0)),
            pl.BlockSpec((1, SGU_GROUPS, c, dg), lambda i: (i // n_prompt_chunks, 0, 0, 0)),
        ],
        out_specs=[
            pl.BlockSpec((c, wdt), lambda i: (i, 0)),
            pl.BlockSpec((c, wdt), lambda i: (i // per_seq, 0)),
        ],
        out_shape=[jax.ShapeDtypeStruct((t, wdt), BF16), jax.ShapeDtypeStruct((n_last * c, wdt), F32)],
        compiler_params=_params(("arbitrary",)),
        name="sgu",
    )(proj, proj, ln_g.reshape(1, wdt), ln_b.reshape(1, wdt), w_sets, bias_sets)
    return y, vl


def _top16(s):
    n, l = s.shape
    iota = lax.broadcasted_iota(jnp.int32, (n, l), 0)
    viota = lax.broadcasted_iota(jnp.int32, (PEER_TOPK, l), 0)

    def body(r, carry):
        s, rank, vals = carry
        m = jnp.max(s, axis=0, keepdims=True)
        idx = jnp.min(jnp.where(s == m, iota, n), axis=0, keepdims=True)
        sel = iota == idx
        rank = jnp.where(sel, r, rank)
        vals = jnp.where(viota == r, m, vals)
        s = jnp.where(sel, -jnp.inf, s)
        return s, rank, vals

    init = (s, jnp.full((n, l), PEER_TOPK, jnp.int32), jnp.zeros((PEER_TOPK, l), F32))
    _, rank, vals = lax.fori_loop(0, PEER_TOPK, body, init)
    return rank, vals


def _peer_topk_kernel(ht_ref, wq_ref, keys_ref, r2_ref, e2_ref, n_ref, e1_ref):
    qt = _dot(wq_ref[...], ht_ref[...]).astype(BF16)
    nk = keys_ref.shape[2]
    dq = keys_ref.shape[3]
    s1_all = _dot(keys_ref[0, 0].astype(BF16), qt[:dq])
    s2_all = _dot(keys_ref[0, 1].astype(BF16), qt[dq:])
    for c in range(ht_ref.shape[1] // LANE):
        cs = slice(c * LANE, (c + 1) * LANE)
        s1 = s1_all[:, cs]
        s2 = s2_all[:, cs]
        rank1, t1 = _top16(s1)
        rank2, t2 = _top16(s2)
        cand = jnp.concatenate([t1[i:i + 1] + t2 for i in range(PEER_TOPK)], axis=0)
        crank, _ = _top16(cand)
        selected = crank < PEER_TOPK
        cmax = t1[0:1] + t2[0:1]
        zsum = jnp.sum(jnp.where(selected, jnp.exp(cand - cmax), 0.0), axis=0, keepdims=True)
        n = jnp.zeros((nk, LANE), F32)
        for i in range(PEER_TOPK):
            cnt = jnp.sum(jnp.where(selected[i * PEER_TOPK:(i + 1) * PEER_TOPK], 1.0, 0.0), axis=0, keepdims=True)
            n = jnp.where(rank1 == i, cnt, n)
        r2_ref[0, :, cs] = rank2.astype(F32)
        n_ref[0, :, cs] = n
        e1_ref[0, :, cs] = jnp.exp(s1 - t1[0:1])
        e2_ref[0, :, cs] = jnp.exp(s2 - t2[0:1]) / zsum


def peer_topk(h_t, wq_t, keys):
    d, t = h_t.shape
    nh, _, nk, dq = keys.shape
    tt = TOK_TILE
    spec = pl.BlockSpec((1, nk, tt), lambda j, h: (h, 0, j))
    shp = jax.ShapeDtypeStruct((nh, nk, t), F32)
    return pl.pallas_call(
        _peer_topk_kernel,
        grid=(t // tt, nh),
        in_specs=[
            pl.BlockSpec((d, tt), lambda j, h: (0, j)),
            pl.BlockSpec((2 * dq, d), lambda j, h: (h, 0)),
            pl.BlockSpec((1, 2, nk, dq), lambda j, h: (h, 0, 0, 0)),
        ],
        out_specs=[spec, spec, spec, spec],
        out_shape=[shp, shp, shp, shp],
        compiler_params=_params(("parallel", "arbitrary")),
        name="peer_topk",
    )(h_t, wq_t, keys)


def _peer_dense_kernel(ht_ref, u_ref, vt_ref, r2_ref, e2_ref, n_ref, e1_ref, o_ref, s_scr, act_scr, *, na, nh):
    i = pl.program_id(1)
    tt = ht_ref.shape[1]
    s_scr[...] = _dot(u_ref[...], ht_ref[...])
    nk = r2_ref.shape[1]
    for a in range(na):
        rs = slice(a * nk, (a + 1) * nk)
        for c in range(tt // LANE):
            cs = slice(c * LANE, (c + 1) * LANE)
            w = jnp.zeros((nk, LANE), F32)
            for h in range(nh):
                nrow = n_ref[h, 0, a:a + 1, cs]
                e1row = e1_ref[h, 0, a:a + 1, cs]
                w = w + jnp.where(r2_ref[h, :, cs] < nrow, e2_ref[h, :, cs] * e1row, 0.0)
            act_scr[rs, cs] = (jax.nn.gelu(s_scr[rs, cs]) * w).astype(BF16)
    d = o_ref.shape[0]
    rb = 512

    @pl.when(i == 0)
    def _():
        for r in range(0, d, rb):
            o_ref[r:r + rb, :] = _dot(vt_ref[r:r + rb, :], act_scr[...])

    @pl.when(i > 0)
    def _():
        for r in range(0, d, rb):
            o_ref[r:r + rb, :] += _dot(vt_ref[r:r + rb, :], act_scr[...])


def peer_dense(h_t, u, v_t, r2, e2, n, e1, na=4):
    d, t = h_t.shape
    e = u.shape[0]
    nh, nk, _ = r2.shape
    tt = TOK_TILE
    ne = na * nk
    n4 = n.reshape(nh, nk // na, na, t)
    e14 = e1.reshape(nh, nk // na, na, t)
    once = pl.Buffered(1)
    return pl.pallas_call(
        functools.partial(_peer_dense_kernel, na=na, nh=nh),
        grid=(t // tt, e // ne),
        in_specs=[
            pl.BlockSpec((d, tt), lambda j, i: (0, j), pipeline_mode=once),
            pl.BlockSpec((ne, d), lambda j, i: (i, 0)),
            pl.BlockSpec((d, ne), lambda j, i: (0, i)),
            pl.BlockSpec((nh, nk, tt), lambda j, i: (0, 0, j), pipeline_mode=once),
            pl.BlockSpec((nh, nk, tt), lambda j, i: (0, 0, j), pipeline_mode=once),
            pl.BlockSpec((nh, 1, na, tt), lambda j, i: (0, i, 0, j)),
            pl.BlockSpec((nh, 1, na, tt), lambda j, i: (0, i, 0, j)),
        ],
        out_specs=pl.BlockSpec((d, tt), lambda j, i: (0, j)),
        out_shape=jax.ShapeDtypeStruct((d, t), F32),
        scratch_shapes=[pltpu.VMEM((ne, tt), F32), pltpu.VMEM((ne, tt), BF16)],
        compiler_params=_params(("parallel", "arbitrary")),
        name="peer_dense",
    )(h_t, u, v_t, r2, e2, n4, e14)


def _softplus(x):
    return jnp.maximum(x, 0.0) + jnp.log1p(jnp.exp(-jnp.abs(x)))


def _gdn_gates_kernel(ab_ref, alog_ref, dtb_ref, tril_ref, eg_ref, eb_ref, brep_ref, gcrep_ref):
    ab = ab_ref[...]
    g = -jnp.exp(alog_ref[...]) * _softplus(ab + dtb_ref[...])
    beta = jax.nn.sigmoid(ab)
    hi = lax.Precision.HIGHEST
    gc = jnp.dot(tril_ref[...], g, preferred_element_type=F32, precision=hi)
    gcrep_ref[...] = jnp.dot(gc, eg_ref[...], preferred_element_type=F32, precision=hi)
    brep_ref[...] = jnp.dot(beta, eb_ref[...], preferred_element_type=F32, precision=hi)


def gdn_gates(ab, a_log, dt_bias, tril):
    rows = ab.shape[0]
    tm = tril.shape[0]
    nh = a_log.shape[0]
    wide = nh * LANE
    pad = LANE - nh
    alog_p = jnp.pad(a_log, (0, pad)).reshape(1, LANE)
    dtb_p = jnp.pad(dt_bias, (0, pad)).reshape(1, LANE)
    head_of_col = jnp.arange(wide) // LANE
    lane = jnp.arange(LANE)[:, None]
    e_g = (lane == head_of_col[None, :]).astype(F32)
    e_b = (lane == head_of_col[None, :] + nh).astype(F32)
    tn = 1024
    return pl.pallas_call(
        _gdn_gates_kernel,
        grid=(rows // tm, wide // tn),
        in_specs=[
            pl.BlockSpec((tm, LANE), lambda i, j: (i, 0)),
            pl.BlockSpec((1, LANE), lambda i, j: (0, 0)),
            pl.BlockSpec((1, LANE), lambda i, j: (0, 0)),
            pl.BlockSpec((tm, tm), lambda i, j: (0, 0)),
            pl.BlockSpec((LANE, tn), lambda i, j: (0, j)),
            pl.BlockSpec((LANE, tn), lambda i, j: (0, j)),
        ],
        out_specs=[pl.BlockSpec((tm, tn), lambda i, j: (i, j)), pl.BlockSpec((tm, tn), lambda i, j: (i, j))],
        out_shape=[jax.ShapeDtypeStruct((rows, wide), F32), jax.ShapeDtypeStruct((rows, wide), F32)],
        compiler_params=_params(("parallel", "parallel")),
        name="gdn_gates",
    )(ab, alog_p, dtb_p, tril, e_g, e_b)


def _gdn_post_conv(y, o_ref, cb, n_qk_blocks, n_q_blocks):
    y = _silu(y)
    is_qk = cb < n_qk_blocks
    qscale = jnp.where(cb < n_q_blocks, GDN_DK ** -0.5, 1.0)
    for hh in range(y.shape[1] // GDN_DK):
        ls = slice(hh * GDN_DK, (hh + 1) * GDN_DK)
        seg = y[:, ls]
        rs = lax.rsqrt(jnp.sum(seg * seg, axis=-1, keepdims=True) + EPS)
        o_ref[:, ls] = seg * jnp.where(is_qk, rs * qscale, 1.0)


def _gdn_conv_kernel(x_ref, w_ref, o_ref, ext, *, tm, n_qk_blocks, n_q_blocks):
    cb = pl.program_id(0)
    tt = pl.program_id(2)

    @pl.when(tt == 0)
    def _():
        ext[0:8, :] = jnp.zeros((8, ext.shape[1]), F32)

    ext[8:8 + tm, :] = x_ref[...]
    w = w_ref[...]
    y = w[0:1] * ext[5:5 + tm, :]
    for tap in range(1, GDN_CONV):
        y = y + w[tap:tap + 1] * ext[5 + tap:5 + tap + tm, :]
    ext[0:8, :] = ext[tm:tm + 8, :]
    _gdn_post_conv(y, o_ref, cb, n_qk_blocks, n_q_blocks)


def gdn_conv_prompt(proj, n_prompt, seq, conv_w, n_ch, tm=512, tc=1024):
    bsz = n_prompt // seq
    nt = seq // tm
    qk = (2 * n_ch) // 3
    return pl.pallas_call(
        functools.partial(_gdn_conv_kernel, tm=tm, n_qk_blocks=qk // tc, n_q_blocks=qk // 2 // tc),
        grid=(n_ch // tc, bsz, nt),
        in_specs=[pl.BlockSpec((tm, tc), lambda cb, b, tt: (b * nt + tt, cb)),
                  pl.BlockSpec((GDN_CONV, tc), lambda cb, b, tt: (0, cb))],
        out_specs=pl.BlockSpec((tm, tc), lambda cb, b, tt: (b * nt + tt, cb)),
        out_shape=jax.ShapeDtypeStruct((n_prompt, n_ch), F32),
        scratch_shapes=[pltpu.VMEM((tm + 8, tc), F32)],
        compiler_params=_params(("parallel", "parallel", "arbitrary")),
        name="gdn_conv_prompt",
    )(proj, conv_w)


def _gdn_conv_sample_kernel(x_ref, buf_ref, w_ref, o_ref, *, n_qk_blocks, n_q_blocks):
    cb = pl.program_id(0)
    w = w_ref[...]
    y = w[0:1] * buf_ref[0]
    for tap in range(1, GDN_CONV - 1):
        y = y + w[tap:tap + 1] * buf_ref[tap]
    y = y + w[GDN_CONV - 1:GDN_CONV] * x_ref[...]
    _gdn_post_conv(y, o_ref, cb, n_qk_blocks, n_q_blocks)


def gdn_conv_sample(proj, n_prompt, buf_t, conv_w, n_ch, tc=1024):
    ns = buf_t.shape[1]
    row_blk = n_prompt // ns
    qk = (2 * n_ch) // 3
    return pl.pallas_call(
        functools.partial(_gdn_conv_sample_kernel, n_qk_blocks=qk // tc, n_q_blocks=qk // 2 // tc),
        grid=(n_ch // tc,),
        in_specs=[pl.BlockSpec((ns, tc), lambda cb: (row_blk, cb)),
                  pl.BlockSpec((GDN_CONV - 1, ns, tc), lambda cb: (0, 0, cb)),
                  pl.BlockSpec((GDN_CONV, tc), lambda cb: (0, cb))],
        out_specs=pl.BlockSpec((ns, tc), lambda cb: (0, cb)),
        out_shape=jax.ShapeDtypeStruct((ns, n_ch), F32),
        compiler_params=_params(("parallel",)),
        name="gdn_conv_sample",
    )(proj, buf_t, conv_w)


def _unit_lower_inverse(a):
    c = a.shape[0]
    row = lax.broadcasted_iota(jnp.int32, (c, c), 0)
    col = lax.broadcasted_iota(jnp.int32, (c, c), 1)
    eye = jnp.where(row == col, 1.0, 0.0)
    blk = 16
    d = jnp.where(row // blk == col // blk, a, 0.0)
    d2 = _dot(d, d)
    d4 = _dot(d2, d2)
    d8 = _dot(d4, d4)
    t = _dot(_dot(_dot(eye - d, eye + d2), eye + d4), eye + d8)
    while blk < c:
        off = jnp.where((row // (2 * blk) == col // (2 * blk)) & (row // blk != col // blk), a, 0.0)
        t = t - _dot(_dot(t, off), t)
        blk *= 2
    return t


def _gdn_chunk(q, k, v, br, gcr, z, ng, s):
    c = q.shape[0]
    row = lax.broadcasted_iota(jnp.int32, (c, c), 0)
    col = lax.broadcasted_iota(jnp.int32, (c, c), 1)
    kb = k * br
    vb = v * br
    eg = jnp.exp(gcr)
    glast = gcr[c - 1:c, :]
    kbg = kb * eg
    qg = q * eg
    kdec = k * jnp.exp(glast - gcr)
    diff = gcr[:, :c] - gcr.T[:c, :]
    decay = jnp.exp(jnp.where(row >= col, diff, -jnp.inf))
    k16 = k.astype(BF16)
    a = _dot_nt(kb.astype(BF16), k16) * jnp.where(row > col, decay, 0.0)
    attn = _dot_nt(q.astype(BF16), k16) * decay
    t = _unit_lower_inverse(a)
    uw = _dot(t, jnp.concatenate([vb, kbg], axis=1))
    dv = v.shape[1]
    s16 = s.astype(BF16)
    v_new = uw[:, :dv] - _dot(uw[:, dv:].astype(BF16), s16)
    vn16 = v_new.astype(BF16)
    o = _dot(qg.astype(BF16), s16) + _dot(attn.astype(BF16), vn16)
    s_new = s * jnp.exp(glast) + _dot_tn(kdec.astype(BF16), vn16)
    og = _rms(o, ng) * _silu(z)
    return og, s_new


def _gdn_core_kernel(q_ref, k_ref, v_ref, b_ref, gc_ref, z_ref, ng_ref, og_ref, sout_ref, s_scr, *, hb, nck, c):
    n = pl.program_id(2)

    @pl.when(n == 0)
    def _():
        s_scr[...] = jnp.zeros_like(s_scr)

    ng = ng_ref[...]
    for hh in range(hb):
        ls = slice(hh * LANE, (hh + 1) * LANE)
        s = s_scr[hh]
        for ck in range(nck):
            rs = slice(ck * c, (ck + 1) * c)
            og, s = _gdn_chunk(q_ref[rs, ls], k_ref[rs, ls], v_ref[rs, ls], b_ref[rs, ls], gc_ref[rs, ls],
                               z_ref[rs, ls], ng, s)
            og_ref[rs, ls] = og.astype(og_ref.dtype)
        s_scr[hh] = s
        sout_ref[0, hh] = s


def gdn_core_prompt(qkv, brep, gcrep, proj, n_prompt, seq, norm_g, hb=2, nck=2):
    c = GDN_CHUNK
    nh = GDN_HEADS
    bsz = n_prompt // seq
    tm = c * nck
    nt = seq // tm
    bw = hb * LANE
    nhb = nh // hb

    def rows(b, h, n):
        return b * nt + n

    og, s_out = pl.pallas_call(
        functools.partial(_gdn_core_kernel, hb=hb, nck=nck, c=c),
        grid=(bsz, nhb, nt),
        in_specs=[
            pl.BlockSpec((tm, bw), lambda b, h, n: (rows(b, h, n), h)),
            pl.BlockSpec((tm, bw), lambda b, h, n: (rows(b, h, n), nhb + h)),
            pl.BlockSpec((tm, bw), lambda b, h, n: (rows(b, h, n), 2 * nhb + h)),
            pl.BlockSpec((tm, bw), lambda b, h, n: (rows(b, h, n), h)),
            pl.BlockSpec((tm, bw), lambda b, h, n: (rows(b, h, n), h)),
            pl.BlockSpec((tm, bw), lambda b, h, n: (rows(b, h, n), 3 * nhb + h)),
            pl.BlockSpec((1, LANE), lambda b, h, n: (0, 0)),
        ],
        out_specs=[
            pl.BlockSpec((tm, bw), lambda b, h, n: (rows(b, h, n), h)),
            pl.BlockSpec((1, hb, GDN_DK, LANE), lambda b, h, n: (b, h, 0, 0)),
        ],
        out_shape=[jax.ShapeDtypeStruct((n_prompt, nh * LANE), BF16),
                   jax.ShapeDtypeStruct((bsz, nh, GDN_DK, LANE), F32)],
        scratch_shapes=[pltpu.VMEM((hb, GDN_DK, LANE), F32)],
        compiler_params=_params(("parallel", "parallel", "arbitrary")),
        name="gdn_core_prompt",
    )(qkv, qkv, qkv, brep, gcrep, proj, norm_g.reshape(1, LANE))
    return og, s_out


def _gdn_sample_kernel(q_ref, k_ref, v_ref, b_ref, g_ref, z_ref, ng_ref, s_ref, og_ref, sout_ref, o_scr):
    ns = q_ref.shape[0]
    qt = q_ref[...].T
    kt = k_ref[...].T
    lane = lax.broadcasted_iota(jnp.int32, qt.shape, 1)

    def body(i, carry):
        pick = lane == i
        qcol = jnp.sum(jnp.where(pick, qt, 0.0), axis=1, keepdims=True)
        kcol = jnp.sum(jnp.where(pick, kt, 0.0), axis=1, keepdims=True)
        v = v_ref[pl.ds(i, 1), :]
        beta = b_ref[pl.ds(i, 1), :]
        eg = jnp.exp(g_ref[pl.ds(i, 1), :])
        sd = s_ref[i, 0] * eg
        ks = jnp.sum(sd * kcol, axis=0, keepdims=True)
        v_new = beta * (v - ks)
        s_new = sd + kcol * v_new
        sout_ref[i, 0] = s_new
        o_scr[pl.ds(i, 1), :] = jnp.sum(s_new * qcol, axis=0, keepdims=True)
        return carry

    lax.fori_loop(0, ns, body, 0)
    og_ref[...] = (_rms(o_scr[...], ng_ref[...]) * _silu(z_ref[...])).astype(og_ref.dtype)


def gdn_core_sample(qkv_s, brep_s, grep_s, proj, n_prompt, state, norm_g):
    ns, nh = state.shape[0], state.shape[1]
    row_blk = n_prompt // ns
    blk = pl.BlockSpec((ns, LANE), lambda h: (0, h))
    st = pl.BlockSpec((ns, 1, GDN_DK, LANE), lambda h: (0, h, 0, 0))
    return pl.pallas_call(
        _gdn_sample_kernel,
        grid=(nh,),
        in_specs=[
            blk,
            pl.BlockSpec((ns, LANE), lambda h: (0, nh + h)),
            pl.BlockSpec((ns, LANE), lambda h: (0, 2 * nh + h)),
            blk,
            blk,
            pl.BlockSpec((ns, LANE), lambda h: (row_blk, 3 * nh + h)),
            pl.BlockSpec((1, LANE), lambda h: (0, 0)),
            st,
        ],
        out_specs=[blk, st],
        out_shape=[jax.ShapeDtypeStruct((ns, nh * LANE), BF16), jax.ShapeDtypeStruct(state.shape, F32)],
        scratch_shapes=[pltpu.VMEM((ns, LANE), F32)],
        compiler_params=_params(("parallel",)),
        name="gdn_core_sample",
    )(qkv_s, qkv_s, qkv_s, brep_s, grep_s, proj, norm_g.reshape(1, LANE), state)


def _peer_ple(x, p16, norm_ffn, norm_ple, w_q, keys, emb_u, emb_v, ple_proj, ple_gate):
    h_t = rmsnorm(x, norm_ffn, BF16, transpose=True)
    r2, e2, n, e1 = peer_topk(h_t, w_q.T.astype(BF16), keys)
    o_t = peer_dense(h_t, emb_u.astype(BF16), emb_v.T.astype(BF16), r2, e2, n, e1)
    x2, hp = add_t_norm(x, o_t, norm_ple)
    return matmul(hp, ple_gate.astype(BF16), mode="ple", extras=(x2, p16, ple_proj.astype(BF16)))


def _chunk_tril(tm, c):
    r = jnp.arange(tm)
    return ((r[:, None] >= r[None, :]) & (r[:, None] // c == r[None, :] // c)).astype(F32)


def kernel(x_prompt, x_sample, p_prompt, p_sample, state_s5_re, state_s5_im, state_gdn, state_gdn_conv, norm_mix, norm_ffn, norm_ple, norm_final, ev_w_in, s5_a_re, s5_a_im, s5_log_dt, s5_b_re, s5_b_im, s5_c_re, s5_c_im, s5_d, s5_w_glu, sgu_ln_g, sgu_ln_b, sgu_w, sgu_b, ev_w_out, gdn_w_in, gdn_conv_w, gdn_a_log, gdn_dt_bias, gdn_norm_g, gdn_w_out, peer_w_q, peer_keys, peer_u, peer_v, ple_proj, ple_gate):
    bsz, seq, d = x_prompt.shape
    ns = x_sample.shape[0]
    n_prompt = bsz * seq
    x = jnp.concatenate([x_prompt.reshape(n_prompt, d), x_sample.reshape(ns, d)], axis=0)
    p16 = jnp.concatenate([p_prompt.reshape(2, n_prompt, -1), p_sample.reshape(2, ns, -1)], axis=1).astype(BF16)

    h = rmsnorm(x, norm_mix[0], BF16)
    proj = matmul(h, ev_w_in[0].astype(BF16))
    s5p = _s5_params(s5_a_re[0], s5_a_im[0], s5_log_dt[0], s5_b_re[0], s5_b_im[0], s5_c_re[0], s5_c_im[0])
    z_p, s5re_p, s5im_p = s5_prompt(proj, n_prompt, seq, s5p, s5_d[0])
    z_s, s5re_s, s5im_s = s5_sample(proj, n_prompt, state_s5_re[0], state_s5_im[0], s5p, s5_d[0])
    z = jnp.concatenate([z_p, z_s], axis=0)
    ya = matmul(z.astype(BF16), s5_w_glu[0].astype(BF16), mode="glu", extras=(z,), out_dtype=BF16)
    yb, v_last = sgu(proj, n_prompt, seq, sgu_ln_g[0], sgu_ln_b[0], sgu_w[0], sgu_b[0])
    x = matmul(jnp.concatenate([ya, yb], axis=1), ev_w_out[0].astype(BF16), mode="resid", extras=(x,))
    x = _peer_ple(x, p16[0], norm_ffn[0], norm_ple[0], peer_w_q[0], peer_keys[0], peer_u[0], peer_v[0],
                  ple_proj[0], ple_gate[0])

    nh = GDN_HEADS
    n_ch = gdn_conv_w.shape[-1]
    n_qkvz = n_ch + nh * LANE
    h = rmsnorm(x, norm_mix[1], BF16)
    w_in = gdn_w_in[0]
    proj = matmul(h, w_in[:, :n_qkvz].astype(BF16))
    w_ab = jnp.pad(w_in[:, n_qkvz:], ((0, 0), (0, LANE - 2 * nh))).astype(BF16)
    ab = matmul(h, w_ab)
    tm_g = 512
    brep_p, gcrep_p = gdn_gates(ab[:n_prompt], gdn_a_log[0], gdn_dt_bias[0], _chunk_tril(tm_g, GDN_CHUNK))
    brep_s, grep_s = gdn_gates(ab[n_prompt:], gdn_a_log[0], gdn_dt_bias[0], jnp.eye(ns, dtype=F32))
    qkv_p = gdn_conv_prompt(proj, n_prompt, seq, gdn_conv_w[0], n_ch)
    buf = state_gdn_conv[0]
    qkv_s = gdn_conv_sample(proj, n_prompt, jnp.swapaxes(buf, 0, 1), gdn_conv_w[0], n_ch)
    og_p, gdn_p = gdn_core_prompt(qkv_p, brep_p, gcrep_p, proj, n_prompt, seq, gdn_norm_g[0])
    og_s, gdn_s = gdn_core_sample(qkv_s, brep_s, grep_s, proj, n_prompt, state_gdn[0], gdn_norm_g[0])
    x = matmul(jnp.concatenate([og_p, og_s], axis=0), gdn_w_out[0].astype(BF16), mode="resid", extras=(x,))
    x = _peer_ple(x, p16[1], norm_ffn[1], norm_ple[1], peer_w_q[1], peer_keys[1], peer_u[1], peer_v[1],
                  ple_proj[1], ple_gate[1])

    y = rmsnorm(x, norm_final, F32)

    conv_p = jnp.stack([proj[(b + 1) * seq - (GDN_CONV - 1):(b + 1) * seq, :n_ch] for b in range(bsz)])
    conv_s = jnp.concatenate([buf[:, 1:], proj[n_prompt:, None, :n_ch]], axis=1)
    n_v = bsz * SGU_CHUNK
    return (
        y[:n_prompt].reshape(bsz, seq, d),
        y[n_prompt:].reshape(ns, 1, d),
        s5re_p[None], s5im_p[None], s5re_s[None], s5im_s[None],
        v_last[:n_v].reshape(1, bsz, SGU_CHUNK, -1),
        v_last[n_v:].reshape(1, ns, 1, -1),
        gdn_p[None], gdn_s[None],
        conv_p[None], conv_s[None],
    )
```

```python
import functools

import jax
import jax.numpy as jnp
from jax import lax
from jax.experimental import pallas as pl
from jax.experimental.pallas import tpu as pltpu

F32 = jnp.float32
BF16 = jnp.bfloat16
EPS = 1e-6
LANE = 128
V7X_VMEM_BYTES = 64 * 1024 * 1024
VMEM_LIMIT = V7X_VMEM_BYTES - 8 * 1024 * 1024

S5_GROUP = 16
S5_STATE = 64
SGU_GROUPS = 8
SGU_CHUNK = 128
GDN_HEADS = 32
GDN_DK = 128
GDN_CONV = 4
GDN_CHUNK = 64
PEER_HEADS = 8
PEER_NKEYS = 128
PEER_TOPK = 16

TOK_TILE = 640
ROW_TILE = 128


def _params(sem, vmem=VMEM_LIMIT):
    return pltpu.CompilerParams(dimension_semantics=sem, vmem_limit_bytes=vmem)


def _dot(a, b):
    return jnp.dot(a, b, preferred_element_type=F32)


def _dot_nt(a, b):
    return lax.dot_general(a, b, (((1,), (1,)), ((), ())), preferred_element_type=F32)


def _dot_tn(a, b):
    return lax.dot_general(a, b, (((0,), (0,)), ((), ())), preferred_element_type=F32)


def _rms(x, g):
    return x * lax.rsqrt(jnp.mean(x * x, axis=-1, keepdims=True) + EPS) * g


def _silu(x):
    return x * jax.nn.sigmoid(x)


def _rmsnorm_kernel(x_ref, g_ref, o_ref, *, transpose):
    y = _rms(x_ref[...], g_ref[...])
    if transpose:
        y = y.T
    o_ref[...] = y.astype(o_ref.dtype)


def rmsnorm(x, g, out_dtype, transpose=False):
    t, d = x.shape
    tm = ROW_TILE
    if transpose:
        out_shape, out_spec = (d, t), pl.BlockSpec((d, tm), lambda i: (0, i))
    else:
        out_shape, out_spec = (t, d), pl.BlockSpec((tm, d), lambda i: (i, 0))
    return pl.pallas_call(
        functools.partial(_rmsnorm_kernel, transpose=transpose),
        grid=(t // tm,),
        in_specs=[pl.BlockSpec((tm, d), lambda i: (i, 0)), pl.BlockSpec((1, d), lambda i: (0, 0))],
        out_specs=out_spec,
        out_shape=jax.ShapeDtypeStruct(out_shape, out_dtype),
        compiler_params=_params(("parallel",)),
        name="rmsnorm_t" if transpose else "rmsnorm",
    )(x, g.reshape(1, d))


def _add_t_norm_kernel(x_ref, ot_ref, g_ref, x2_ref, h_ref):
    x2 = x_ref[...] + ot_ref[...].T
    x2_ref[...] = x2
    h_ref[...] = _rms(x2, g_ref[...]).astype(h_ref.dtype)


def add_t_norm(x, o_t, g):
    t, d = x.shape
    tm = ROW_TILE
    return pl.pallas_call(
        _add_t_norm_kernel,
        grid=(t // tm,),
        in_specs=[pl.BlockSpec((tm, d), lambda i: (i, 0)), pl.BlockSpec((d, tm), lambda i: (0, i)),
                  pl.BlockSpec((1, d), lambda i: (0, 0))],
        out_specs=[pl.BlockSpec((tm, d), lambda i: (i, 0)), pl.BlockSpec((tm, d), lambda i: (i, 0))],
        out_shape=[jax.ShapeDtypeStruct((t, d), F32), jax.ShapeDtypeStruct((t, d), BF16)],
        compiler_params=_params(("parallel",)),
        name="add_t_norm",
    )(x, o_t, g.reshape(1, d))


def _mm_kernel(*refs, mode):
    x_ref, w_ref = refs[0], refs[1]
    o_ref = refs[-1]
    acc = _dot(x_ref[...], w_ref[...])
    if mode == "resid":
        acc = refs[2][...] + acc
    elif mode == "glu":
        z = refs[2][...]
        acc = z * jax.nn.sigmoid(acc)
    elif mode == "ple":
        r_ref, p_ref, pw_ref = refs[2], refs[3], refs[4]
        acc = r_ref[...] + jax.nn.sigmoid(acc) * _dot(p_ref[...], pw_ref[...])
    o_ref[...] = acc.astype(o_ref.dtype)


def matmul(x, w, *, mode="plain", extras=(), out_dtype=F32, tn=1024):
    m, k = x.shape
    n = w.shape[1]
    tm = TOK_TILE
    tn = min(tn, n)
    in_specs = [pl.BlockSpec((tm, k), lambda j, i: (i, 0)), pl.BlockSpec((k, tn), lambda j, i: (0, j))]
    if mode in ("resid", "glu", "ple"):
        in_specs.append(pl.BlockSpec((tm, tn), lambda j, i: (i, j)))
    if mode == "ple":
        kp = extras[1].shape[1]
        in_specs += [pl.BlockSpec((tm, kp), lambda j, i: (i, 0)), pl.BlockSpec((kp, tn), lambda j, i: (0, j))]
    return pl.pallas_call(
        functools.partial(_mm_kernel, mode=mode),
        grid=(n // tn, m // tm),
        in_specs=in_specs,
        out_specs=pl.BlockSpec((tm, tn), lambda j, i: (i, j)),
        out_shape=jax.ShapeDtypeStruct((m, n), out_dtype),
        compiler_params=_params(("parallel", "parallel")),
        name="mm_" + mode,
    )(x, w, *extras)


def _s5_prompt_kernel(x0, x1, x2, x3, bblk_ref, cblk_ref, lre_ref, lim_ref, d_ref,
                      z_ref, hre_out, him_out, bure, buim, hre, him, st_re, st_im, *, c):
    tc = pl.program_id(1)

    @pl.when(tc == 0)
    def _():
        st_re[...] = jnp.zeros_like(st_re)
        st_im[...] = jnp.zeros_like(st_im)

    xs = (x0, x1, x2, x3)
    nq = bure.shape[0]
    half_w = nq * LANE
    for b in range(4):
        xb = xs[b][...]
        for half in range(2):
            s = 2 * b + half
            bu = _dot(xb[:, half * LANE:(half + 1) * LANE], bblk_ref[half])
            for q in range(nq):
                bure[q, s * c:(s + 1) * c, :] = bu[:, q * LANE:(q + 1) * LANE]
                buim[q, s * c:(s + 1) * c, :] = bu[:, half_w + q * LANE:half_w + (q + 1) * LANE]
    for q in range(nq):
        ls = slice(q * LANE, (q + 1) * LANE)
        lr = lre_ref[0, :, ls]
        li = lim_ref[0, :, ls]
        hr = st_re[:, ls]
        hi = st_im[:, ls]
        for t in range(c):
            br = bure[q, pl.ds(t, 8, stride=c), :]
            bi = buim[q, pl.ds(t, 8, stride=c), :]
            nr = lr * hr - li * hi + br
            ni = lr * hi + li * hr + bi
            hre[q, pl.ds(t, 8, stride=c), :] = nr
            him[q, pl.ds(t, 8, stride=c), :] = ni
            hr, hi = nr, ni
        st_re[:, ls] = hr
        st_im[:, ls] = hi
    hre_out[0] = st_re[...]
    him_out[0] = st_im[...]
    for b in range(4):
        xb = xs[b][...]
        for half in range(2):
            s = 2 * b + half
            ls = slice(half * LANE, (half + 1) * LANE)
            h_re = jnp.concatenate([hre[q, s * c:(s + 1) * c, :] for q in range(nq)], axis=1)
            h_im = jnp.concatenate([him[q, s * c:(s + 1) * c, :] for q in range(nq)], axis=1)
            y = (_dot(h_re, cblk_ref[half, :half_w, :]) + _dot(h_im, cblk_ref[half, half_w:, :])
                 + d_ref[:, ls] * xb[:, ls])
            z_ref[b, :, ls] = jax.nn.gelu(y)


def _s5_sample_kernel(x_ref, h0re_ref, h0im_ref, bblk_ref, cblk_ref, lre_ref, lim_ref, d_ref,
                      z_ref, hre_out, him_out):
    x = x_ref[...]
    half_w = h0re_ref.shape[1]
    bu = _dot(x, bblk_ref[0])
    lr = lre_ref[0]
    li = lim_ref[0]
    h0r = h0re_ref[...]
    h0i = h0im_ref[...]
    nr = lr * h0r - li * h0i + bu[:, :half_w]
    ni = lr * h0i + li * h0r + bu[:, half_w:]
    hre_out[...] = nr
    him_out[...] = ni
    y = _dot(nr, cblk_ref[0, :half_w, :]) + _dot(ni, cblk_ref[0, half_w:, :]) + d_ref[...] * x
    z_ref[...] = jax.nn.gelu(y)


def _s5_params(a_re, a_im, log_dt, b_re, b_im, c_re, c_im):
    g, p = a_re.shape
    nb = g // 8
    lam = lax.complex(a_re, a_im)
    dt = jnp.exp(log_dt)[:, None]
    lam_bar = jnp.exp(lam * dt)
    b_bar = ((lam_bar - 1.0) / lam)[..., None] * lax.complex(b_re, b_im)
    eye = jnp.eye(8, dtype=F32)

    def bmat(v):
        v = jnp.swapaxes(v, 1, 2).reshape(nb, 8, S5_GROUP, p)
        return jnp.einsum("kjcp,jl->kjclp", v, eye).reshape(nb, 8 * S5_GROUP, 8 * p)

    def cmat(v):
        v = jnp.swapaxes(v, 1, 2).reshape(nb, 8, p, S5_GROUP)
        return jnp.einsum("kjpc,jl->kjplc", v, eye).reshape(nb, 8 * p, 8 * S5_GROUP)

    bblk = jnp.concatenate([bmat(jnp.real(b_bar)), bmat(jnp.imag(b_bar))], axis=2)
    cblk = jnp.concatenate([cmat(c_re), cmat(-c_im)], axis=1)
    lre = jnp.real(lam_bar).reshape(nb, 8 * p)
    lim = jnp.imag(lam_bar).reshape(nb, 8 * p)
    return bblk, cblk, lre, lim


def s5_prompt(proj, n_prompt, seq, params, d_skip, c=128):
    bblk, cblk, lre, lim = params
    nb, _, two_w = bblk.shape
    half_w = two_w // 2
    bsz = n_prompt // seq
    assert bsz == 4
    w = nb * 8 * S5_GROUP
    ngb = nb // 2
    nt = seq // c
    lre_t = jnp.tile(lre.reshape(ngb, 2, half_w), (1, bsz, 1))
    lim_t = jnp.tile(lim.reshape(ngb, 2, half_w), (1, bsz, 1))
    x_specs = [pl.BlockSpec((c, 2 * LANE), functools.partial(lambda gb, tc, b: (b * nt + tc, gb), b=b))
               for b in range(bsz)]
    z, hre, him = pl.pallas_call(
        functools.partial(_s5_prompt_kernel, c=c),
        grid=(ngb, nt),
        in_specs=x_specs + [
            pl.BlockSpec((2, LANE, two_w), lambda gb, tc: (gb, 0, 0)),
            pl.BlockSpec((2, two_w, LANE), lambda gb, tc: (gb, 0, 0)),
            pl.BlockSpec((1, 8, half_w), lambda gb, tc: (gb, 0, 0)),
            pl.BlockSpec((1, 8, half_w), lambda gb, tc: (gb, 0, 0)),
            pl.BlockSpec((1, 2 * LANE), lambda gb, tc: (0, gb)),
        ],
        out_specs=[
            pl.BlockSpec((bsz, c, 2 * LANE), lambda gb, tc: (0, tc, gb)),
            pl.BlockSpec((1, 8, half_w), lambda gb, tc: (gb, 0, 0)),
            pl.BlockSpec((1, 8, half_w), lambda gb, tc: (gb, 0, 0)),
        ],
        out_shape=[
            jax.ShapeDtypeStruct((bsz, seq, w), F32),
            jax.ShapeDtypeStruct((ngb, 8, half_w), F32),
            jax.ShapeDtypeStruct((ngb, 8, half_w), F32),
        ],
        scratch_shapes=[pltpu.VMEM((half_w // LANE, 8 * c, LANE), F32) for _ in range(4)]
        + [pltpu.VMEM((8, half_w), F32), pltpu.VMEM((8, half_w), F32)],
        compiler_params=_params(("parallel", "arbitrary")),
        name="s5_prompt",
    )(proj, proj, proj, proj, bblk, cblk, lre_t, lim_t, d_skip.reshape(1, w))

    def states(h):
        h = h.reshape(ngb, bsz, 2, 8, S5_STATE)
        return jnp.transpose(h, (1, 0, 2, 3, 4)).reshape(bsz, nb * 8, S5_STATE)

    return z.reshape(n_prompt, w), states(hre), states(him)


def s5_sample(proj, n_prompt, h0_re, h0_im, params, d_skip):
    bblk, cblk, lre, lim = params
    nb, _, two_w = bblk.shape
    half_w = two_w // 2
    ns = h0_re.shape[0]
    w = nb * 8 * S5_GROUP
    row_blk = n_prompt // ns
    z, hre, him = pl.pallas_call(
        _s5_sample_kernel,
        grid=(nb,),
        in_specs=[
            pl.BlockSpec((ns, LANE), lambda k: (row_blk, k)),
            pl.BlockSpec((ns, half_w), lambda k: (0, k)),
            pl.BlockSpec((ns, half_w), lambda k: (0, k)),
            pl.BlockSpec((1, LANE, two_w), lambda k: (k, 0, 0)),
            pl.BlockSpec((1, two_w, LANE), lambda k: (k, 0, 0)),
            pl.BlockSpec((1, 1, half_w), lambda k: (k, 0, 0)),
            pl.BlockSpec((1, 1, half_w), lambda k: (k, 0, 0)),
            pl.BlockSpec((1, LANE), lambda k: (0, k)),
        ],
        out_specs=[
            pl.BlockSpec((ns, LANE), lambda k: (0, k)),
            pl.BlockSpec((ns, half_w), lambda k: (0, k)),
            pl.BlockSpec((ns, half_w), lambda k: (0, k)),
        ],
        out_shape=[
            jax.ShapeDtypeStruct((ns, w), F32),
            jax.ShapeDtypeStruct((ns, nb * half_w), F32),
            jax.ShapeDtypeStruct((ns, nb * half_w), F32),
        ],
        compiler_params=_params(("parallel",)),
        name="s5_sample",
    )(proj, h0_re.reshape(ns, -1), h0_im.reshape(ns, -1), bblk, cblk,
      lre.reshape(nb, 1, half_w), lim.reshape(nb, 1, half_w), d_skip.reshape(1, w))
    return z, hre.reshape(ns, nb * 8, S5_STATE), him.reshape(ns, nb * 8, S5_STATE)


def _sgu_kernel(u_ref, v_ref, g_ref, b_ref, w_ref, bias_ref, y_ref, vl_ref):
    u = jax.nn.gelu(u_ref[...])
    v = jax.nn.gelu(v_ref[...])
    mu = jnp.mean(v, axis=-1, keepdims=True)
    vc = v - mu
    vn = vc * lax.rsqrt(jnp.mean(vc * vc, axis=-1, keepdims=True) + EPS) * g_ref[...] + b_ref[...]
    vl_ref[...] = vn
    c = u.shape[0]
    dg = u.shape[1] // SGU_GROUPS
    causal = lax.broadcasted_iota(jnp.int32, (c, c), 0) >= lax.broadcasted_iota(jnp.int32, (c, c), 1)
    for g in range(SGU_GROUPS):
        ls = slice(g * dg, (g + 1) * dg)
        w = jnp.where(causal, w_ref[0, g], 0.0).astype(BF16)
        mixed = _dot(w, vn[:, ls].astype(BF16)) + bias_ref[0, g]
        y_ref[:, ls] = (u[:, ls] * mixed).astype(y_ref.dtype)


def sgu(proj, n_prompt, seq, ln_g, ln_b, w_s, b_s):
    t = proj.shape[0]
    wdt = proj.shape[1] // 3
    c = SGU_CHUNK
    dg = wdt // SGU_GROUPS
    n_chunks = t // c
    per_seq = seq // c
    n_prompt_chunks = n_prompt // c
    eye = jnp.eye(c, dtype=F32)
    w_sets = jnp.stack([w_s, w_s[:, :1, :1] * eye])
    bias_sets = jnp.stack([jnp.broadcast_to(b_s[:, :, None], (SGU_GROUPS, c, dg)),
                           jnp.broadcast_to(b_s[:, :1, None], (SGU_GROUPS, c, dg))])
    n_last = n_prompt // seq + (n_chunks - n_prompt_chunks)
    y, vl = pl.pallas_call(
        _sgu_kernel,
        grid=(n_chunks,),
        in_specs=[
            pl.BlockSpec((c, wdt), lambda i: (i, 1)),
            pl.BlockSpec((c, wdt), lambda i: (i, 2)),
            pl.BlockSpec((1, wdt), lambda i: (0, 0)),
            pl.BlockSpec((1, wdt), lambda i: (0, 0)),
            pl.BlockSpec((1, SGU_GROUPS, c, c), lambda i: (i // n_prompt_chunks, 0, 0, 0)),
            pl.BlockSpec((1, SGU_GROUPS, c, dg), lambda i: (i // n_prompt_chunks, 0, 0, 0)),
        ],
        out_specs=[
            pl.BlockSpec((c, wdt), lambda i: (i, 0)),
            pl.BlockSpec((c, wdt), lambda i: (i // per_seq, 0)),
        ],
        out_shape=[jax.ShapeDtypeStruct((t, wdt), BF16), jax.ShapeDtypeStruct((n_last * c, wdt), F32)],
        compiler_params=_params(("arbitrary",)),
        name="sgu",
    )(proj, proj, ln_g.reshape(1, wdt), ln_b.reshape(1, wdt), w_sets, bias_sets)
    return y, vl


def _top16(arrays):
    l = arrays[0].shape[1]
    viota = lax.broadcasted_iota(jnp.int32, (PEER_TOPK, l), 0)
    iotas = [lax.broadcasted_iota(jnp.int32, s.shape, 0) for s in arrays]

    def body(r, carry):
        out = []
        for (s, rank, vals), iota in zip(carry, iotas):
            m = jnp.max(s, axis=0, keepdims=True)
            idx = jnp.min(jnp.where(s == m, iota, s.shape[0]), axis=0, keepdims=True)
            sel = iota == idx
            out.append((jnp.where(sel, -jnp.inf, s), jnp.where(sel, r, rank), jnp.where(viota == r, m, vals)))
        return tuple(out)

    init = tuple((s, jnp.full(s.shape, PEER_TOPK, jnp.int32), jnp.zeros((PEER_TOPK, l), F32)) for s in arrays)
    res = lax.fori_loop(0, PEER_TOPK, body, init)
    return [(rank, vals) for _, rank, vals in res]


def _peer_candidates(t1, t2):
    k = PEER_TOPK
    sub = lax.broadcasted_iota(jnp.int32, (8, t1.shape[1]), 0)
    groups = [t1[0:1] + t2[0:8], t1[0:1] + t2[8:16], t1[1:2] + t2[0:8]]
    slices = [slice(0, 16), slice(16, 24)]
    for i in range(2, 8):
        groups.append(jnp.where(sub < k // (i + 1), t1[i:i + 1] + t2[0:8], -jnp.inf))
        slices.append(slice(8 * (i + 1), 8 * (i + 2)))
    groups.append(t1[8:16] + t2[0:1])
    slices += [slice(64 + i, 65 + i) for i in range(8, k)]
    return jnp.concatenate(groups, axis=0), slices


def _peer_topk_kernel(ht_ref, wq_ref, keys_ref, r2_ref, e2_ref, n_ref, e1_ref):
    qt = _dot(wq_ref[...], ht_ref[...]).astype(BF16)
    nk = keys_ref.shape[2]
    dq = keys_ref.shape[3]
    s1_all = _dot(keys_ref[0, 0].astype(BF16), qt[:dq])
    s2_all = _dot(keys_ref[0, 1].astype(BF16), qt[dq:])
    for c in range(ht_ref.shape[1] // LANE):
        cs = slice(c * LANE, (c + 1) * LANE)
        s1 = s1_all[:, cs]
        s2 = s2_all[:, cs]
        (rank1, t1), (rank2, t2) = _top16([s1, s2])
        cand, cand_rows = _peer_candidates(t1, t2)
        (crank, _), = _top16([cand])
        selected = crank < PEER_TOPK
        cmax = t1[0:1] + t2[0:1]
        zsum = jnp.sum(jnp.where(selected, jnp.exp(cand - cmax), 0.0), axis=0, keepdims=True)
        n = jnp.zeros((nk, LANE), F32)
        for i in range(PEER_TOPK):
            cnt = jnp.sum(jnp.where(selected[cand_rows[i]], 1.0, 0.0), axis=0, keepdims=True)
            n = jnp.where(rank1 == i, cnt, n)
        r2_ref[0, :, cs] = rank2.astype(F32)
        n_ref[0, :, cs] = n
        e1_ref[0, :, cs] = jnp.exp(s1 - t1[0:1])
        e2_ref[0, :, cs] = jnp.exp(s2 - t2[0:1]) / zsum


def peer_topk(h_t, wq_t, keys):
    d, t = h_t.shape
    nh, _, nk, dq = keys.shape
    tt = TOK_TILE
    spec = pl.BlockSpec((1, nk, tt), lambda j, h: (h, 0, j))
    shp = jax.ShapeDtypeStruct((nh, nk, t), F32)
    return pl.pallas_call(
        _peer_topk_kernel,
        grid=(t // tt, nh),
        in_specs=[
            pl.BlockSpec((d, tt), lambda j, h: (0, j)),
            pl.BlockSpec((2 * dq, d), lambda j, h: (h, 0)),
            pl.BlockSpec((1, 2, nk, dq), lambda j, h: (h, 0, 0, 0)),
        ],
        out_specs=[spec, spec, spec, spec],
        out_shape=[shp, shp, shp, shp],
        compiler_params=_params(("parallel", "arbitrary")),
        name="peer_topk",
    )(h_t, wq_t, keys)


def _peer_dense_kernel(ht_ref, u_ref, vt_ref, r2_ref, e2_ref, n_ref, e1_ref, o_ref, s_scr, act_a, act_b,
                       *, na, nh, nb):
    i = pl.program_id(1)
    tt = ht_ref.shape[1]
    nk = r2_ref.shape[1]
    d = o_ref.shape[0]
    rb = 512

    @pl.when(i == 0)
    def _():
        o_ref[...] = jnp.zeros_like(o_ref)
        act_b[...] = jnp.zeros_like(act_b)

    @pl.when(i < nb)
    def _():
        s_scr[...] = _dot(u_ref[...], ht_ref[...])

    def step(act_prev, act_cur):
        for r in range(0, d, rb):
            o_ref[r:r + rb, :] += _dot(vt_ref[r:r + rb, :], act_prev[...])
        for a in range(na):
            rs = slice(a * nk, (a + 1) * nk)
            for c in range(tt // LANE):
                cs = slice(c * LANE, (c + 1) * LANE)
                w = jnp.zeros((nk, LANE), F32)
                for h in range(nh):
                    nrow = n_ref[h, 0, a:a + 1, cs]
                    e1row = e1_ref[h, 0, a:a + 1, cs]
                    w = w + jnp.where(r2_ref[h, :, cs] < nrow, e2_ref[h, :, cs] * e1row, 0.0)
                act_cur[rs, cs] = (jax.nn.gelu(s_scr[rs, cs]) * w).astype(BF16)

    @pl.when(i % 2 == 0)
    def _():
        step(act_b, act_a)

    @pl.when(i % 2 == 1)
    def _():
        step(act_a, act_b)


def peer_dense(h_t, u, v_t, r2, e2, n, e1, na=4):
    d, t = h_t.shape
    e = u.shape[0]
    nh, nk, _ = r2.shape
    tt = TOK_TILE
    ne = na * nk
    n4 = n.reshape(nh, nk // na, na, t)
    e14 = e1.reshape(nh, nk // na, na, t)
    once = pl.Buffered(1)
    nb = e // ne

    def score_blk(i):
        return jnp.minimum(i, nb - 1)

    return pl.pallas_call(
        functools.partial(_peer_dense_kernel, na=na, nh=nh, nb=nb),
        grid=(t // tt, nb + 1),
        in_specs=[
            pl.BlockSpec((d, tt), lambda j, i: (0, j), pipeline_mode=once),
            pl.BlockSpec((ne, d), lambda j, i: (score_blk(i), 0)),
            pl.BlockSpec((d, ne), lambda j, i: (0, jnp.maximum(i - 1, 0))),
            pl.BlockSpec((nh, nk, tt), lambda j, i: (0, 0, j), pipeline_mode=once),
            pl.BlockSpec((nh, nk, tt), lambda j, i: (0, 0, j), pipeline_mode=once),
            pl.BlockSpec((nh, 1, na, tt), lambda j, i: (0, score_blk(i), 0, j)),
            pl.BlockSpec((nh, 1, na, tt), lambda j, i: (0, score_blk(i), 0, j)),
        ],
        out_specs=pl.BlockSpec((d, tt), lambda j, i: (0, j)),
        out_shape=jax.ShapeDtypeStruct((d, t), F32),
        scratch_shapes=[pltpu.VMEM((ne, tt), F32), pltpu.VMEM((ne, tt), BF16), pltpu.VMEM((ne, tt), BF16)],
        compiler_params=_params(("parallel", "arbitrary")),
        name="peer_dense",
    )(h_t, u, v_t, r2, e2, n4, e14)


def _softplus(x):
    return jnp.maximum(x, 0.0) + jnp.log1p(jnp.exp(-jnp.abs(x)))


def _gdn_gates_kernel(ab_ref, alog_ref, dtb_ref, tril_ref, eg_ref, eb_ref, brep_ref, gcrep_ref):
    ab = ab_ref[...]
    g = -jnp.exp(alog_ref[...]) * _softplus(ab + dtb_ref[...])
    beta = jax.nn.sigmoid(ab)
    hi = lax.Precision.HIGHEST
    gc = jnp.dot(tril_ref[...], g, preferred_element_type=F32, precision=hi)
    gcrep_ref[...] = jnp.dot(gc, eg_ref[...], preferred_element_type=F32, precision=hi)
    brep_ref[...] = jnp.dot(beta, eb_ref[...], preferred_element_type=F32, precision=hi)


def gdn_gates(ab, a_log, dt_bias, tril):
    rows = ab.shape[0]
    tm = tril.shape[0]
    nh = a_log.shape[0]
    wide = nh * LANE
    pad = LANE - nh
    alog_p = jnp.pad(a_log, (0, pad)).reshape(1, LANE)
    dtb_p = jnp.pad(dt_bias, (0, pad)).reshape(1, LANE)
    head_of_col = jnp.arange(wide) // LANE
    lane = jnp.arange(LANE)[:, None]
    e_g = (lane == head_of_col[None, :]).astype(F32)
    e_b = (lane == head_of_col[None, :] + nh).astype(F32)
    tn = 1024
    return pl.pallas_call(
        _gdn_gates_kernel,
        grid=(rows // tm, wide // tn),
        in_specs=[
            pl.BlockSpec((tm, LANE), lambda i, j: (i, 0)),
            pl.BlockSpec((1, LANE), lambda i, j: (0, 0)),
            pl.BlockSpec((1, LANE), lambda i, j: (0, 0)),
            pl.BlockSpec((tm, tm), lambda i, j: (0, 0)),
            pl.BlockSpec((LANE, tn), lambda i, j: (0, j)),
            pl.BlockSpec((LANE, tn), lambda i, j: (0, j)),
        ],
        out_specs=[pl.BlockSpec((tm, tn), lambda i, j: (i, j)), pl.BlockSpec((tm, tn), lambda i, j: (i, j))],
        out_shape=[jax.ShapeDtypeStruct((rows, wide), F32), jax.ShapeDtypeStruct((rows, wide), F32)],
        compiler_params=_params(("parallel", "parallel")),
        name="gdn_gates",
    )(ab, alog_p, dtb_p, tril, e_g, e_b)


def _gdn_post_conv(y, o_ref, cb, n_qk_blocks, n_q_blocks):
    y = _silu(y)
    is_qk = cb < n_qk_blocks
    qscale = jnp.where(cb < n_q_blocks, GDN_DK ** -0.5, 1.0)
    for hh in range(y.shape[1] // GDN_DK):
        ls = slice(hh * GDN_DK, (hh + 1) * GDN_DK)
        seg = y[:, ls]
        rs = lax.rsqrt(jnp.sum(seg * seg, axis=-1, keepdims=True) + EPS)
        o_ref[:, ls] = seg * jnp.where(is_qk, rs * qscale, 1.0)


def _gdn_conv_kernel(x_ref, w_ref, o_ref, ext, *, tm, n_qk_blocks, n_q_blocks):
    cb = pl.program_id(0)
    tt = pl.program_id(2)

    @pl.when(tt == 0)
    def _():
        ext[0:8, :] = jnp.zeros((8, ext.shape[1]), F32)

    ext[8:8 + tm, :] = x_ref[...]
    w = w_ref[...]
    y = w[0:1] * ext[5:5 + tm, :]
    for tap in range(1, GDN_CONV):
        y = y + w[tap:tap + 1] * ext[5 + tap:5 + tap + tm, :]
    ext[0:8, :] = ext[tm:tm + 8, :]
    _gdn_post_conv(y, o_ref, cb, n_qk_blocks, n_q_blocks)


def gdn_conv_prompt(proj, n_prompt, seq, conv_w, n_ch, tm=512, tc=1024):
    bsz = n_prompt // seq
    nt = seq // tm
    qk = (2 * n_ch) // 3
    return pl.pallas_call(
        functools.partial(_gdn_conv_kernel, tm=tm, n_qk_blocks=qk // tc, n_q_blocks=qk // 2 // tc),
        grid=(n_ch // tc, bsz, nt),
        in_specs=[pl.BlockSpec((tm, tc), lambda cb, b, tt: (b * nt + tt, cb)),
                  pl.BlockSpec((GDN_CONV, tc), lambda cb, b, tt: (0, cb))],
        out_specs=pl.BlockSpec((tm, tc), lambda cb, b, tt: (b * nt + tt, cb)),
        out_shape=jax.ShapeDtypeStruct((n_prompt, n_ch), F32),
        scratch_shapes=[pltpu.VMEM((tm + 8, tc), F32)],
        compiler_params=_params(("parallel", "parallel", "arbitrary")),
        name="gdn_conv_prompt",
    )(proj, conv_w)


def _gdn_conv_sample_kernel(x_ref, buf_ref, w_ref, o_ref, *, n_qk_blocks, n_q_blocks):
    cb = pl.program_id(0)
    w = w_ref[...]
    y = w[0:1] * buf_ref[0]
    for tap in range(1, GDN_CONV - 1):
        y = y + w[tap:tap + 1] * buf_ref[tap]
    y = y + w[GDN_CONV - 1:GDN_CONV] * x_ref[...]
    _gdn_post_conv(y, o_ref, cb, n_qk_blocks, n_q_blocks)


def gdn_conv_sample(proj, n_prompt, buf_t, conv_w, n_ch, tc=1024):
    ns = buf_t.shape[1]
    row_blk = n_prompt // ns
    qk = (2 * n_ch) // 3
    return pl.pallas_call(
        functools.partial(_gdn_conv_sample_kernel, n_qk_blocks=qk // tc, n_q_blocks=qk // 2 // tc),
        grid=(n_ch // tc,),
        in_specs=[pl.BlockSpec((ns, tc), lambda cb: (row_blk, cb)),
                  pl.BlockSpec((GDN_CONV - 1, ns, tc), lambda cb: (0, 0, cb)),
                  pl.BlockSpec((GDN_CONV, tc), lambda cb: (0, cb))],
        out_specs=pl.BlockSpec((ns, tc), lambda cb: (0, cb)),
        out_shape=jax.ShapeDtypeStruct((ns, n_ch), F32),
        compiler_params=_params(("parallel",)),
        name="gdn_conv_sample",
    )(proj, buf_t, conv_w)


def _unit_lower_inverses(mats):
    c = mats[0].shape[0]
    row = lax.broadcasted_iota(jnp.int32, (c, c), 0)
    col = lax.broadcasted_iota(jnp.int32, (c, c), 1)
    eye = jnp.where(row == col, 1.0, 0.0)
    blk = 16
    ds = [jnp.where(row // blk == col // blk, a, 0.0) for a in mats]
    d2 = [_dot(d, d) for d in ds]
    d4 = [_dot(d, d) for d in d2]
    d8 = [_dot(d, d) for d in d4]
    ts = [_dot(eye - d, eye + x) for d, x in zip(ds, d2)]
    ts = [_dot(t, eye + x) for t, x in zip(ts, d4)]
    ts = [_dot(t, eye + x) for t, x in zip(ts, d8)]
    while blk < c:
        off_mask = (row // (2 * blk) == col // (2 * blk)) & (row // blk != col // blk)
        tmp = [_dot(t, jnp.where(off_mask, a, 0.0)) for t, a in zip(ts, mats)]
        tmp = [_dot(x, t) for x, t in zip(tmp, ts)]
        ts = [t - x for t, x in zip(ts, tmp)]
        blk *= 2
    return ts


def _gdn_chunks(qs, ks, vs, brs, gcs, zs, ng, ss):
    c = qs[0].shape[0]
    dv = vs[0].shape[1]
    row = lax.broadcasted_iota(jnp.int32, (c, c), 0)
    col = lax.broadcasted_iota(jnp.int32, (c, c), 1)
    kb = [k * b for k, b in zip(ks, brs)]
    vb = [v * b for v, b in zip(vs, brs)]
    eg = [jnp.exp(g) for g in gcs]
    glast = [g[c - 1:c, :] for g in gcs]
    kbg = [x * e for x, e in zip(kb, eg)]
    qg = [q * e for q, e in zip(qs, eg)]
    kdec = [k * jnp.exp(gl - g) for k, gl, g in zip(ks, glast, gcs)]
    decay = [jnp.exp(jnp.where(row >= col, g[:, :c] - g.T[:c, :], -jnp.inf)) for g in gcs]
    k16 = [k.astype(BF16) for k in ks]
    a = [_dot_nt(x.astype(BF16), k) * jnp.where(row > col, d, 0.0) for x, k, d in zip(kb, k16, decay)]
    attn = [_dot_nt(q.astype(BF16), k) * d for q, k, d in zip(qs, k16, decay)]
    ts = _unit_lower_inverses(a)
    uw = [_dot(t, jnp.concatenate([x, y], axis=1)) for t, x, y in zip(ts, vb, kbg)]
    s16 = [s.astype(BF16) for s in ss]
    v_new = [x[:, :dv] - _dot(x[:, dv:].astype(BF16), s) for x, s in zip(uw, s16)]
    vn16 = [x.astype(BF16) for x in v_new]
    o_state = [_dot(x.astype(BF16), s) for x, s in zip(qg, s16)]
    o_local = [_dot(x.astype(BF16), v) for x, v in zip(attn, vn16)]
    s_new = [s * jnp.exp(gl) + _dot_tn(x.astype(BF16), v) for s, gl, x, v in zip(ss, glast, kdec, vn16)]
    og = [_rms(x + y, ng) * _silu(z) for x, y, z in zip(o_state, o_local, zs)]
    return og, s_new


def _gdn_core_kernel(q_ref, k_ref, v_ref, b_ref, gc_ref, z_ref, ng_ref, og_ref, sout_ref, s_scr, *, hb, nck, c):
    n = pl.program_id(2)

    @pl.when(n == 0)
    def _():
        s_scr[...] = jnp.zeros_like(s_scr)

    ng = ng_ref[...]
    ss = [s_scr[hh] for hh in range(hb)]
    for ck in range(nck):
        rs = slice(ck * c, (ck + 1) * c)

        def heads(ref):
            return [ref[rs, hh * LANE:(hh + 1) * LANE] for hh in range(hb)]

        og, ss = _gdn_chunks(heads(q_ref), heads(k_ref), heads(v_ref), heads(b_ref), heads(gc_ref), heads(z_ref),
                             ng, ss)
        for hh in range(hb):
            og_ref[rs, hh * LANE:(hh + 1) * LANE] = og[hh].astype(og_ref.dtype)
    for hh in range(hb):
        s_scr[hh] = ss[hh]
        sout_ref[0, hh] = ss[hh]


def gdn_core_prompt(qkv, brep, gcrep, proj, n_prompt, seq, norm_g, hb=GDN_HEADS, nck=1):
    c = GDN_CHUNK
    nh = GDN_HEADS
    bsz = n_prompt // seq
    tm = c * nck
    nt = seq // tm
    bw = hb * LANE
    nhb = nh // hb

    def rows(b, h, n):
        return b * nt + n

    og, s_out = pl.pallas_call(
        functools.partial(_gdn_core_kernel, hb=hb, nck=nck, c=c),
        grid=(bsz, nhb, nt),
        in_specs=[
            pl.BlockSpec((tm, bw), lambda b, h, n: (rows(b, h, n), h)),
            pl.BlockSpec((tm, bw), lambda b, h, n: (rows(b, h, n), nhb + h)),
            pl.BlockSpec((tm, bw), lambda b, h, n: (rows(b, h, n), 2 * nhb + h)),
            pl.BlockSpec((tm, bw), lambda b, h, n: (rows(b, h, n), h)),
            pl.BlockSpec((tm, bw), lambda b, h, n: (rows(b, h, n), h)),
            pl.BlockSpec((tm, bw), lambda b, h, n: (rows(b, h, n), 3 * nhb + h)),
            pl.BlockSpec((1, LANE), lambda b, h, n: (0, 0)),
        ],
        out_specs=[
            pl.BlockSpec((tm, bw), lambda b, h, n: (rows(b, h, n), h)),
            pl.BlockSpec((1, hb, GDN_DK, LANE), lambda b, h, n: (b, h, 0, 0)),
        ],
        out_shape=[jax.ShapeDtypeStruct((n_prompt, nh * LANE), BF16),
                   jax.ShapeDtypeStruct((bsz, nh, GDN_DK, LANE), F32)],
        scratch_shapes=[pltpu.VMEM((hb, GDN_DK, LANE), F32)],
        compiler_params=_params(("parallel", "parallel", "arbitrary")),
        name="gdn_core_prompt",
    )(qkv, qkv, qkv, brep, gcrep, proj, norm_g.reshape(1, LANE))
    return og, s_out


def _gdn_sample_kernel(q_ref, k_ref, v_ref, b_ref, g_ref, z_ref, ng_ref, s_ref, og_ref, sout_ref, o_scr):
    ns = q_ref.shape[0]
    qt = q_ref[...].T
    kt = k_ref[...].T
    lane = lax.broadcasted_iota(jnp.int32, qt.shape, 1)

    def body(i, carry):
        pick = lane == i
        qcol = jnp.sum(jnp.where(pick, qt, 0.0), axis=1, keepdims=True)
        kcol = jnp.sum(jnp.where(pick, kt, 0.0), axis=1, keepdims=True)
        v = v_ref[pl.ds(i, 1), :]
        beta = b_ref[pl.ds(i, 1), :]
        eg = jnp.exp(g_ref[pl.ds(i, 1), :])
        sd = s_ref[i, 0] * eg
        ks = jnp.sum(sd * kcol, axis=0, keepdims=True)
        v_new = beta * (v - ks)
        s_new = sd + kcol * v_new
        sout_ref[i, 0] = s_new
        o_scr[pl.ds(i, 1), :] = jnp.sum(s_new * qcol, axis=0, keepdims=True)
        return carry

    lax.fori_loop(0, ns, body, 0)
    og_ref[...] = (_rms(o_scr[...], ng_ref[...]) * _silu(z_ref[...])).astype(og_ref.dtype)


def gdn_core_sample(qkv_s, brep_s, grep_s, proj, n_prompt, state, norm_g):
    ns, nh = state.shape[0], state.shape[1]
    row_blk = n_prompt // ns
    blk = pl.BlockSpec((ns, LANE), lambda h: (0, h))
    st = pl.BlockSpec((ns, 1, GDN_DK, LANE), lambda h: (0, h, 0, 0))
    return pl.pallas_call(
        _gdn_sample_kernel,
        grid=(nh,),
        in_specs=[
            blk,
            pl.BlockSpec((ns, LANE), lambda h: (0, nh + h)),
            pl.BlockSpec((ns, LANE), lambda h: (0, 2 * nh + h)),
            blk,
            blk,
            pl.BlockSpec((ns, LANE), lambda h: (row_blk, 3 * nh + h)),
            pl.BlockSpec((1, LANE), lambda h: (0, 0)),
            st,
        ],
        out_specs=[blk, st],
        out_shape=[jax.ShapeDtypeStruct((ns, nh * LANE), BF16), jax.ShapeDtypeStruct(state.shape, F32)],
        scratch_shapes=[pltpu.VMEM((ns, LANE), F32)],
        compiler_params=_params(("parallel",)),
        name="gdn_core_sample",
    )(qkv_s, qkv_s, qkv_s, brep_s, grep_s, proj, norm_g.reshape(1, LANE), state)


def _peer_ple(x, p16, norm_ffn, norm_ple, w_q, keys, emb_u, emb_v, ple_proj, ple_gate):
    h_t = rmsnorm(x, norm_ffn, BF16, transpose=True)
    r2, e2, n, e1 = peer_topk(h_t, w_q.T.astype(BF16), keys)
    o_t = peer_dense(h_t, emb_u.astype(BF16), emb_v.T.astype(BF16), r2, e2, n, e1)
    x2, hp = add_t_norm(x, o_t, norm_ple)
    return matmul(hp, ple_gate.astype(BF16), mode="ple", extras=(x2, p16, ple_proj.astype(BF16)))


def _chunk_tril(tm, c):
    r = jnp.arange(tm)
    return ((r[:, None] >= r[None, :]) & (r[:, None] // c == r[None, :] // c)).astype(F32)


def kernel(x_prompt, x_sample, p_prompt, p_sample, state_s5_re, state_s5_im, state_gdn, state_gdn_conv, norm_mix, norm_ffn, norm_ple, norm_final, ev_w_in, s5_a_re, s5_a_im, s5_log_dt, s5_b_re, s5_b_im, s5_c_re, s5_c_im, s5_d, s5_w_glu, sgu_ln_g, sgu_ln_b, sgu_w, sgu_b, ev_w_out, gdn_w_in, gdn_conv_w, gdn_a_log, gdn_dt_bias, gdn_norm_g, gdn_w_out, peer_w_q, peer_keys, peer_u, peer_v, ple_proj, ple_gate):
    bsz, seq, d = x_prompt.shape
    ns = x_sample.shape[0]
    n_prompt = bsz * seq
    x = jnp.concatenate([x_prompt.reshape(n_prompt, d), x_sample.reshape(ns, d)], axis=0)
    p16 = jnp.concatenate([p_prompt.reshape(2, n_prompt, -1), p_sample.reshape(2, ns, -1)], axis=1).astype(BF16)

    h = rmsnorm(x, norm_mix[0], BF16)
    proj = matmul(h, ev_w_in[0].astype(BF16))
    s5p = _s5_params(s5_a_re[0], s5_a_im[0], s5_log_dt[0], s5_b_re[0], s5_b_im[0], s5_c_re[0], s5_c_im[0])
    z_p, s5re_p, s5im_p = s5_prompt(proj, n_prompt, seq, s5p, s5_d[0])
    z_s, s5re_s, s5im_s = s5_sample(proj, n_prompt, state_s5_re[0], state_s5_im[0], s5p, s5_d[0])
    z = jnp.concatenate([z_p, z_s], axis=0)
    ya = matmul(z.astype(BF16), s5_w_glu[0].astype(BF16), mode="glu", extras=(z,), out_dtype=BF16)
    yb, v_last = sgu(proj, n_prompt, seq, sgu_ln_g[0], sgu_ln_b[0], sgu_w[0], sgu_b[0])
    x = matmul(jnp.concatenate([ya, yb], axis=1), ev_w_out[0].astype(BF16), mode="resid", extras=(x,))
    x = _peer_ple(x, p16[0], norm_ffn[0], norm_ple[0], peer_w_q[0], peer_keys[0], peer_u[0], peer_v[0],
                  ple_proj[0], ple_gate[0])

    nh = GDN_HEADS
    n_ch = gdn_conv_w.shape[-1]
    n_qkvz = n_ch + nh * LANE
    h = rmsnorm(x, norm_mix[1], BF16)
    w_in = gdn_w_in[0]
    proj = matmul(h, w_in[:, :n_qkvz].astype(BF16))
    w_ab = jnp.pad(w_in[:, n_qkvz:], ((0, 0), (0, LANE - 2 * nh))).astype(BF16)
    ab = matmul(h, w_ab)
    tm_g = 512
    brep_p, gcrep_p = gdn_gates(ab[:n_prompt], gdn_a_log[0], gdn_dt_bias[0], _chunk_tril(tm_g, GDN_CHUNK))
    brep_s, grep_s = gdn_gates(ab[n_prompt:], gdn_a_log[0], gdn_dt_bias[0], jnp.eye(ns, dtype=F32))
    qkv_p = gdn_conv_prompt(proj, n_prompt, seq, gdn_conv_w[0], n_ch)
    buf = state_gdn_conv[0]
    qkv_s = gdn_conv_sample(proj, n_prompt, jnp.swapaxes(buf, 0, 1), gdn_conv_w[0], n_ch)
    og_p, gdn_p = gdn_core_prompt(qkv_p, brep_p, gcrep_p, proj, n_prompt, seq, gdn_norm_g[0])
    og_s, gdn_s = gdn_core_sample(qkv_s, brep_s, grep_s, proj, n_prompt, state_gdn[0], gdn_norm_g[0])
    x = matmul(jnp.concatenate([og_p, og_s], axis=0), gdn_w_out[0].astype(BF16), mode="resid", extras=(x,))
    x = _peer_ple(x, p16[1], norm_ffn[1], norm_ple[1], peer_w_q[1], peer_keys[1], peer_u[1], peer_v[1],
                  ple_proj[1], ple_gate[1])

    y = rmsnorm(x, norm_final, F32)

    conv_p = jnp.stack([proj[(b + 1) * seq - (GDN_CONV - 1):(b + 1) * seq, :n_ch] for b in range(bsz)])
    conv_s = jnp.concatenate([buf[:, 1:], proj[n_prompt:, None, :n_ch]], axis=1)
    n_v = bsz * SGU_CHUNK
    return (
        y[:n_prompt].reshape(bsz, seq, d),
        y[n_prompt:].reshape(ns, 1, d),
        s5re_p[None], s5im_p[None], s5re_s[None], s5im_s[None],
        v_last[:n_v].reshape(1, bsz, SGU_CHUNK, -1),
        v_last[n_v:].reshape(1, ns, 1, -1),
        gdn_p[None], gdn_s[None],
        conv_p[None], conv_s[None],
    )
```

```python
import functools

import jax
import jax.numpy as jnp
from jax import lax
from jax.experimental import pallas as pl
from jax.experimental.pallas import tpu as pltpu

F32 = jnp.float32
BF16 = jnp.bfloat16
EPS = 1e-6
LANE = 128
V7X_VMEM_BYTES = 64 * 1024 * 1024
VMEM_LIMIT = V7X_VMEM_BYTES - 8 * 1024 * 1024

S5_GROUP = 16
S5_STATE = 64
SGU_GROUPS = 8
SGU_CHUNK = 128
GDN_HEADS = 32
GDN_DK = 128
GDN_CONV = 4
GDN_CHUNK = 64
PEER_HEADS = 8
PEER_NKEYS = 128
PEER_TOPK = 16
PEER_BLOCK_KEYS = 4

TOK_TILE = 640
ROW_TILE = 128


def _params(sem, vmem=VMEM_LIMIT):
    return pltpu.CompilerParams(dimension_semantics=sem, vmem_limit_bytes=vmem)


def _dot(a, b):
    return jnp.dot(a, b, preferred_element_type=F32)


def _dot_nt(a, b):
    return lax.dot_general(a, b, (((1,), (1,)), ((), ())), preferred_element_type=F32)


def _dot_tn(a, b):
    return lax.dot_general(a, b, (((0,), (0,)), ((), ())), preferred_element_type=F32)


def _rms(x, g):
    return x * lax.rsqrt(jnp.mean(x * x, axis=-1, keepdims=True) + EPS) * g


def _silu(x):
    return x * jax.nn.sigmoid(x)


def _rmsnorm_kernel(x_ref, g_ref, o_ref, *, transpose):
    y = _rms(x_ref[...], g_ref[...])
    if transpose:
        y = y.T
    o_ref[...] = y.astype(o_ref.dtype)


def rmsnorm(x, g, out_dtype, transpose=False):
    t, d = x.shape
    tm = ROW_TILE
    if transpose:
        out_shape, out_spec = (d, t), pl.BlockSpec((d, tm), lambda i: (0, i))
    else:
        out_shape, out_spec = (t, d), pl.BlockSpec((tm, d), lambda i: (i, 0))
    return pl.pallas_call(
        functools.partial(_rmsnorm_kernel, transpose=transpose),
        grid=(t // tm,),
        in_specs=[pl.BlockSpec((tm, d), lambda i: (i, 0)), pl.BlockSpec((1, d), lambda i: (0, 0))],
        out_specs=out_spec,
        out_shape=jax.ShapeDtypeStruct(out_shape, out_dtype),
        compiler_params=_params(("parallel",)),
        name="rmsnorm_t" if transpose else "rmsnorm",
    )(x, g.reshape(1, d))


def _add_norm_kernel(x_ref, o_ref, g_ref, x2_ref, h_ref):
    x2 = x_ref[...] + o_ref[...]
    x2_ref[...] = x2
    h_ref[...] = _rms(x2, g_ref[...]).astype(h_ref.dtype)


def add_norm(x, o, g):
    t, d = x.shape
    tm = ROW_TILE
    row = pl.BlockSpec((tm, d), lambda i: (i, 0))
    return pl.pallas_call(
        _add_norm_kernel,
        grid=(t // tm,),
        in_specs=[row, row, pl.BlockSpec((1, d), lambda i: (0, 0))],
        out_specs=[row, row],
        out_shape=[jax.ShapeDtypeStruct((t, d), F32), jax.ShapeDtypeStruct((t, d), BF16)],
        compiler_params=_params(("parallel",)),
        name="add_norm",
    )(x, o, g.reshape(1, d))


def _mm_kernel(*refs, mode):
    x_ref, w_ref = refs[0], refs[1]
    o_ref = refs[-1]
    acc = _dot(x_ref[...], w_ref[...])
    if mode == "resid":
        acc = refs[2][...] + acc
    elif mode == "glu":
        z = refs[2][...]
        acc = z * jax.nn.sigmoid(acc)
    elif mode == "ple":
        r_ref, p_ref, pw_ref = refs[2], refs[3], refs[4]
        acc = r_ref[...] + jax.nn.sigmoid(acc) * _dot(p_ref[...], pw_ref[...])
    o_ref[...] = acc.astype(o_ref.dtype)


def matmul(x, w, *, mode="plain", extras=(), out_dtype=F32, tn=1024):
    m, k = x.shape
    n = w.shape[1]
    tm = TOK_TILE
    tn = min(tn, n)
    in_specs = [pl.BlockSpec((tm, k), lambda j, i: (i, 0)), pl.BlockSpec((k, tn), lambda j, i: (0, j))]
    if mode in ("resid", "glu", "ple"):
        in_specs.append(pl.BlockSpec((tm, tn), lambda j, i: (i, j)))
    if mode == "ple":
        kp = extras[1].shape[1]
        in_specs += [pl.BlockSpec((tm, kp), lambda j, i: (i, 0)), pl.BlockSpec((kp, tn), lambda j, i: (0, j))]
    return pl.pallas_call(
        functools.partial(_mm_kernel, mode=mode),
        grid=(n // tn, m // tm),
        in_specs=in_specs,
        out_specs=pl.BlockSpec((tm, tn), lambda j, i: (i, j)),
        out_shape=jax.ShapeDtypeStruct((m, n), out_dtype),
        compiler_params=_params(("parallel", "parallel")),
        name="mm_" + mode,
    )(x, w, *extras)


def _s5_prompt_kernel(x0, x1, x2, x3, bcat_ref, ccat_ref, lre_ref, lim_ref, d_ref,
                      z_ref, hre_out, him_out, u_scr, bu_scr, h_scr, y_scr, st_re, st_im, *, c):
    tc = pl.program_id(1)

    @pl.when(tc == 0)
    def _():
        st_re[...] = jnp.zeros_like(st_re)
        st_im[...] = jnp.zeros_like(st_im)

    xs = (x0, x1, x2, x3)
    half_w = st_re.shape[1]
    zeros = jnp.zeros((c, LANE), F32)
    for b in range(4):
        for half in range(2):
            s = 2 * b + half
            u_scr[half, pl.ds(s, c, stride=8), :] = xs[b][:, half * LANE:(half + 1) * LANE]
            u_scr[1 - half, pl.ds(s, c, stride=8), :] = zeros
    bu_scr[...] = _dot(jnp.concatenate([u_scr[0], u_scr[1]], axis=1), bcat_ref[0])
    lr = lre_ref[0]
    li = lim_ref[0]
    hr = st_re[...]
    hi = st_im[...]
    for t in range(c):
        rows = slice(8 * t, 8 * t + 8)
        nr = lr * hr - li * hi + bu_scr[rows, :half_w]
        ni = lr * hi + li * hr + bu_scr[rows, half_w:]
        h_scr[rows, :half_w] = nr
        h_scr[rows, half_w:] = ni
        hr, hi = nr, ni
    st_re[...] = hr
    st_im[...] = hi
    hre_out[0] = hr
    him_out[0] = hi
    y = _dot(h_scr[...], ccat_ref[0])
    odd = lax.broadcasted_iota(jnp.int32, (8 * c, LANE), 0) % 2 == 1
    y_scr[...] = jnp.where(odd, y[:, LANE:], y[:, :LANE])
    for b in range(4):
        for half in range(2):
            s = 2 * b + half
            ls = slice(half * LANE, (half + 1) * LANE)
            ys = y_scr[pl.ds(s, c, stride=8), :]
            z_ref[b, :, ls] = jax.nn.gelu(ys + d_ref[:, ls] * xs[b][:, ls])


def _s5_sample_kernel(x_ref, h0re_ref, h0im_ref, bblk_ref, cblk_ref, lre_ref, lim_ref, d_ref,
                      z_ref, hre_out, him_out):
    x = x_ref[...]
    half_w = h0re_ref.shape[1]
    bu = _dot(x, bblk_ref[0])
    lr = lre_ref[0]
    li = lim_ref[0]
    h0r = h0re_ref[...]
    h0i = h0im_ref[...]
    nr = lr * h0r - li * h0i + bu[:, :half_w]
    ni = lr * h0i + li * h0r + bu[:, half_w:]
    hre_out[...] = nr
    him_out[...] = ni
    y = _dot(nr, cblk_ref[0, :half_w, :]) + _dot(ni, cblk_ref[0, half_w:, :]) + d_ref[...] * x
    z_ref[...] = jax.nn.gelu(y)


def _s5_params(a_re, a_im, log_dt, b_re, b_im, c_re, c_im):
    g, p = a_re.shape
    nb = g // 8
    lam = lax.complex(a_re, a_im)
    dt = jnp.exp(log_dt)[:, None]
    lam_bar = jnp.exp(lam * dt)
    b_bar = ((lam_bar - 1.0) / lam)[..., None] * lax.complex(b_re, b_im)
    eye = jnp.eye(8, dtype=F32)

    def bmat(v):
        v = jnp.swapaxes(v, 1, 2).reshape(nb, 8, S5_GROUP, p)
        return jnp.einsum("kjcp,jl->kjclp", v, eye).reshape(nb, 8 * S5_GROUP, 8 * p)

    def cmat(v):
        v = jnp.swapaxes(v, 1, 2).reshape(nb, 8, p, S5_GROUP)
        return jnp.einsum("kjpc,jl->kjplc", v, eye).reshape(nb, 8 * p, 8 * S5_GROUP)

    bblk = jnp.concatenate([bmat(jnp.real(b_bar)), bmat(jnp.imag(b_bar))], axis=2)
    cblk = jnp.concatenate([cmat(c_re), cmat(-c_im)], axis=1)
    lre = jnp.real(lam_bar).reshape(nb, 8 * p)
    lim = jnp.imag(lam_bar).reshape(nb, 8 * p)
    return bblk, cblk, lre, lim


def s5_prompt(proj, n_prompt, seq, params, d_skip, c=128):
    bblk, cblk, lre, lim = params
    nb, _, two_w = bblk.shape
    half_w = two_w // 2
    bsz = n_prompt // seq
    assert bsz == 4
    w = nb * 8 * S5_GROUP
    ngb = nb // 2
    nt = seq // c
    lre_t = jnp.tile(lre.reshape(ngb, 2, half_w), (1, bsz, 1))
    lim_t = jnp.tile(lim.reshape(ngb, 2, half_w), (1, bsz, 1))
    bcat = bblk.reshape(ngb, 2 * LANE, two_w)
    ccat = jnp.swapaxes(cblk.reshape(ngb, 2, two_w, LANE), 1, 2).reshape(ngb, two_w, 2 * LANE)
    x_specs = [pl.BlockSpec((c, 2 * LANE), functools.partial(lambda gb, tc, b: (b * nt + tc, gb), b=b))
               for b in range(bsz)]
    z, hre, him = pl.pallas_call(
        functools.partial(_s5_prompt_kernel, c=c),
        grid=(ngb, nt),
        in_specs=x_specs + [
            pl.BlockSpec((1, 2 * LANE, two_w), lambda gb, tc: (gb, 0, 0)),
            pl.BlockSpec((1, two_w, 2 * LANE), lambda gb, tc: (gb, 0, 0)),
            pl.BlockSpec((1, 8, half_w), lambda gb, tc: (gb, 0, 0)),
            pl.BlockSpec((1, 8, half_w), lambda gb, tc: (gb, 0, 0)),
            pl.BlockSpec((1, 2 * LANE), lambda gb, tc: (0, gb)),
        ],
        out_specs=[
            pl.BlockSpec((bsz, c, 2 * LANE), lambda gb, tc: (0, tc, gb)),
            pl.BlockSpec((1, 8, half_w), lambda gb, tc: (gb, 0, 0)),
            pl.BlockSpec((1, 8, half_w), lambda gb, tc: (gb, 0, 0)),
        ],
        out_shape=[
            jax.ShapeDtypeStruct((bsz, seq, w), F32),
            jax.ShapeDtypeStruct((ngb, 8, half_w), F32),
            jax.ShapeDtypeStruct((ngb, 8, half_w), F32),
        ],
        scratch_shapes=[pltpu.VMEM((2, 8 * c, LANE), F32), pltpu.VMEM((8 * c, two_w), F32),
                        pltpu.VMEM((8 * c, two_w), F32), pltpu.VMEM((8 * c, LANE), F32),
                        pltpu.VMEM((8, half_w), F32), pltpu.VMEM((8, half_w), F32)],
        compiler_params=_params(("parallel", "arbitrary")),
        name="s5_prompt",
    )(proj, proj, proj, proj, bcat, ccat, lre_t, lim_t, d_skip.reshape(1, w))

    def states(h):
        h = h.reshape(ngb, bsz, 2, 8, S5_STATE)
        return jnp.transpose(h, (1, 0, 2, 3, 4)).reshape(bsz, nb * 8, S5_STATE)

    return z.reshape(n_prompt, w), states(hre), states(him)


def s5_sample(proj, n_prompt, h0_re, h0_im, params, d_skip):
    bblk, cblk, lre, lim = params
    nb, _, two_w = bblk.shape
    half_w = two_w // 2
    ns = h0_re.shape[0]
    w = nb * 8 * S5_GROUP
    row_blk = n_prompt // ns
    z, hre, him = pl.pallas_call(
        _s5_sample_kernel,
        grid=(nb,),
        in_specs=[
            pl.BlockSpec((ns, LANE), lambda k: (row_blk, k)),
            pl.BlockSpec((ns, half_w), lambda k: (0, k)),
            pl.BlockSpec((ns, half_w), lambda k: (0, k)),
            pl.BlockSpec((1, LANE, two_w), lambda k: (k, 0, 0)),
            pl.BlockSpec((1, two_w, LANE), lambda k: (k, 0, 0)),
            pl.BlockSpec((1, 1, half_w), lambda k: (k, 0, 0)),
            pl.BlockSpec((1, 1, half_w), lambda k: (k, 0, 0)),
            pl.BlockSpec((1, LANE), lambda k: (0, k)),
        ],
        out_specs=[
            pl.BlockSpec((ns, LANE), lambda k: (0, k)),
            pl.BlockSpec((ns, half_w), lambda k: (0, k)),
            pl.BlockSpec((ns, half_w), lambda k: (0, k)),
        ],
        out_shape=[
            jax.ShapeDtypeStruct((ns, w), F32),
            jax.ShapeDtypeStruct((ns, nb * half_w), F32),
            jax.ShapeDtypeStruct((ns, nb * half_w), F32),
        ],
        compiler_params=_params(("parallel",)),
        name="s5_sample",
    )(proj, h0_re.reshape(ns, -1), h0_im.reshape(ns, -1), bblk, cblk,
      lre.reshape(nb, 1, half_w), lim.reshape(nb, 1, half_w), d_skip.reshape(1, w))
    return z, hre.reshape(ns, nb * 8, S5_STATE), him.reshape(ns, nb * 8, S5_STATE)


def _sgu_kernel(u_ref, v_ref, g_ref, b_ref, w_ref, bias_ref, y_ref, vl_ref):
    u = jax.nn.gelu(u_ref[...])
    v = jax.nn.gelu(v_ref[...])
    mu = jnp.mean(v, axis=-1, keepdims=True)
    vc = v - mu
    vn = vc * lax.rsqrt(jnp.mean(vc * vc, axis=-1, keepdims=True) + EPS) * g_ref[...] + b_ref[...]
    vl_ref[...] = vn
    c = u.shape[0]
    dg = u.shape[1] // SGU_GROUPS
    causal = lax.broadcasted_iota(jnp.int32, (c, c), 0) >= lax.broadcasted_iota(jnp.int32, (c, c), 1)
    for g in range(SGU_GROUPS):
        ls = slice(g * dg, (g + 1) * dg)
        w = jnp.where(causal, w_ref[0, g], 0.0).astype(BF16)
        mixed = _dot(w, vn[:, ls].astype(BF16)) + bias_ref[0, g]
        y_ref[:, ls] = (u[:, ls] * mixed).astype(y_ref.dtype)


def sgu(proj, n_prompt, seq, ln_g, ln_b, w_s, b_s):
    t = proj.shape[0]
    wdt = proj.shape[1] // 3
    c = SGU_CHUNK
    dg = wdt // SGU_GROUPS
    n_chunks = t // c
    per_seq = seq // c
    n_prompt_chunks = n_prompt // c
    eye = jnp.eye(c, dtype=F32)
    w_sets = jnp.stack([w_s, w_s[:, :1, :1] * eye])
    bias_sets = jnp.stack([jnp.broadcast_to(b_s[:, :, None], (SGU_GROUPS, c, dg)),
                           jnp.broadcast_to(b_s[:, :1, None], (SGU_GROUPS, c, dg))])
    n_last = n_prompt // seq + (n_chunks - n_prompt_chunks)
    y, vl = pl.pallas_call(
        _sgu_kernel,
        grid=(n_chunks,),
        in_specs=[
            pl.BlockSpec((c, wdt), lambda i: (i, 1)),
            pl.BlockSpec((c, wdt), lambda i: (i, 2)),
            pl.BlockSpec((1, wdt), lambda i: (0, 0)),
            pl.BlockSpec((1, wdt), lambda i: (0, 0)),
            pl.BlockSpec((1, SGU_GROUPS, c, c), lambda i: (i // n_prompt_chunks, 0, 0, 0)),
            pl.BlockSpec((1, SGU_GROUPS, c, dg), lambda i: (i // n_prompt_chunks, 0, 0, 0)),
        ],
        out_specs=[
            pl.BlockSpec((c, wdt), lambda i: (i, 0)),
            pl.BlockSpec((c, wdt), lambda i: (i // per_seq, 0)),
        ],
        out_shape=[jax.ShapeDtypeStruct((t, wdt), BF16), jax.ShapeDtypeStruct((n_last * c, wdt), F32)],
        compiler_params=_params(("arbitrary",)),
        name="sgu",
    )(proj, proj, ln_g.reshape(1, wdt), ln_b.reshape(1, wdt), w_sets, bias_sets)
    return y, vl


def _extract16(arrays):
    l = arrays[0].shape[1]
    viota = lax.broadcasted_iota(jnp.int32, (PEER_TOPK, l), 0)

    def body(r, carry):
        out = []
        for s, vals, idxs in carry:
            iota = lax.broadcasted_iota(jnp.int32, s.shape, 0)
            m = jnp.max(s, axis=0, keepdims=True)
            idx = jnp.min(jnp.where(s == m, iota, s.shape[0]), axis=0, keepdims=True)
            out.append((jnp.where(iota == idx, -jnp.inf, s), jnp.where(viota == r, m, vals),
                        jnp.where(viota == r, idx, idxs)))
        return tuple(out)

    init = tuple((s, jnp.zeros((PEER_TOPK, l), F32), jnp.zeros((PEER_TOPK, l), jnp.int32)) for s in arrays)
    return lax.fori_loop(0, PEER_TOPK, body, init)


def _ranks(idxs, n):
    iota = lax.broadcasted_iota(jnp.int32, (n, idxs.shape[1]), 0)
    rank = jnp.full(iota.shape, PEER_TOPK, jnp.int32)
    for r in range(PEER_TOPK):
        rank = jnp.where(iota == idxs[r:r + 1], r, rank)
    return rank


def _peer_candidates(t1, t2):
    k = PEER_TOPK
    sub = lax.broadcasted_iota(jnp.int32, (8, t1.shape[1]), 0)
    groups = [t1[0:1] + t2[0:8], t1[0:1] + t2[8:16], t1[1:2] + t2[0:8]]
    slices = [slice(0, 16), slice(16, 24)]
    for i in range(2, 8):
        groups.append(jnp.where(sub < k // (i + 1), t1[i:i + 1] + t2[0:8], -jnp.inf))
        slices.append(slice(8 * (i + 1), 8 * (i + 2)))
    groups.append(t1[8:16] + t2[0:1])
    slices += [slice(64 + i, 65 + i) for i in range(8, k)]
    return jnp.concatenate(groups, axis=0), slices


def _peer_topk_kernel(ht_ref, wq_ref, keys_ref, r2_ref, e2_ref, n_ref, e1_ref):
    qt = _dot(wq_ref[...], ht_ref[...]).astype(BF16)
    nk = keys_ref.shape[2]
    dq = keys_ref.shape[3]
    s1_all = _dot(keys_ref[0, 0].astype(BF16), qt[:dq])
    s2_all = _dot(keys_ref[0, 1].astype(BF16), qt[dq:])
    for c in range(ht_ref.shape[1] // LANE):
        cs = slice(c * LANE, (c + 1) * LANE)
        s1 = s1_all[:, cs]
        s2 = s2_all[:, cs]
        (_, t1, idx1), (_, t2, idx2) = _extract16([s1, s2])
        rank1 = _ranks(idx1, nk)
        rank2 = _ranks(idx2, nk)
        cand, cand_rows = _peer_candidates(t1, t2)
        (cand_left, _, _), = _extract16([cand])
        selected = (cand_left == -jnp.inf) & (cand > -jnp.inf)
        cmax = t1[0:1] + t2[0:1]
        zsum = jnp.sum(jnp.where(selected, jnp.exp(cand - cmax), 0.0), axis=0, keepdims=True)
        n = jnp.zeros((nk, LANE), F32)
        for i in range(PEER_TOPK):
            cnt = jnp.sum(jnp.where(selected[cand_rows[i]], 1.0, 0.0), axis=0, keepdims=True)
            n = jnp.where(rank1 == i, cnt, n)
        r2_ref[0, :, cs] = rank2.astype(F32)
        n_ref[0, :, cs] = n
        e1_ref[0, :, cs] = jnp.exp(s1 - t1[0:1])
        e2_ref[0, :, cs] = jnp.exp(s2 - t2[0:1]) / zsum


def peer_topk(h_t, wq_t, keys):
    d, t = h_t.shape
    nh, _, nk, dq = keys.shape
    tt = TOK_TILE
    spec = pl.BlockSpec((1, nk, tt), lambda j, h: (h, 0, j))
    shp = jax.ShapeDtypeStruct((nh, nk, t), F32)
    return pl.pallas_call(
        _peer_topk_kernel,
        grid=(t // tt, nh),
        in_specs=[
            pl.BlockSpec((d, tt), lambda j, h: (0, j)),
            pl.BlockSpec((2 * dq, d), lambda j, h: (h, 0)),
            pl.BlockSpec((1, 2, nk, dq), lambda j, h: (h, 0, 0, 0)),
        ],
        out_specs=[spec, spec, spec, spec],
        out_shape=[shp, shp, shp, shp],
        compiler_params=_params(("parallel", "arbitrary")),
        name="peer_topk",
    )(h_t, wq_t, keys)


def _peer_dense_kernel(ht_ref, u_ref, v_ref, r2_ref, e2_ref, n_ref, e1_ref, o_ref, s_scr, act_a, act_b,
                       *, na, nh, nb):
    i = pl.program_id(1)
    tt = ht_ref.shape[1]
    nk = r2_ref.shape[1]
    d = o_ref.shape[1]
    cb = 512

    @pl.when(i == 0)
    def _():
        o_ref[...] = jnp.zeros_like(o_ref)
        act_b[...] = jnp.zeros_like(act_b)

    @pl.when(i < nb)
    def _():
        s_scr[...] = _dot(u_ref[...], ht_ref[...])

    tiles = [(a, c) for a in range(na) for c in range(tt // LANE)]
    n_chunks = d // cb

    def step(act_prev, act_cur):
        for r in range(n_chunks):
            o_ref[:, r * cb:(r + 1) * cb] += _dot(act_prev[...], v_ref[:, r * cb:(r + 1) * cb])
            for a, c in tiles[r * len(tiles) // n_chunks:(r + 1) * len(tiles) // n_chunks]:
                rs = slice(a * nk, (a + 1) * nk)
                cs = slice(c * LANE, (c + 1) * LANE)
                w = jnp.zeros((nk, LANE), F32)
                for h in range(nh):
                    nrow = n_ref[h, 0, a:a + 1, cs]
                    e1row = e1_ref[h, 0, a:a + 1, cs]
                    w = w + jnp.where(r2_ref[h, :, cs] < nrow, e2_ref[h, :, cs] * e1row, 0.0)
                act_cur[cs, rs] = (jax.nn.gelu(s_scr[rs, cs]) * w).T.astype(BF16)

    @pl.when(i % 2 == 0)
    def _():
        step(act_b, act_a)

    @pl.when(i % 2 == 1)
    def _():
        step(act_a, act_b)


def peer_dense(h_t, u, v, r2, e2, n, e1):
    d, t = h_t.shape
    e = u.shape[0]
    nh, nk, _ = r2.shape
    tt = TOK_TILE
    na = PEER_BLOCK_KEYS
    ne = na * nk
    nb = e // ne
    n4 = n.reshape(nh, nk // na, na, t)
    e14 = e1.reshape(nh, nk // na, na, t)
    once = pl.Buffered(1)

    def score_blk(i):
        return jnp.minimum(i, nb - 1)

    return pl.pallas_call(
        functools.partial(_peer_dense_kernel, na=na, nh=nh, nb=nb),
        grid=(t // tt, nb + 1),
        in_specs=[
            pl.BlockSpec((d, tt), lambda j, i: (0, j), pipeline_mode=once),
            pl.BlockSpec((ne, d), lambda j, i: (score_blk(i), 0)),
            pl.BlockSpec((ne, d), lambda j, i: (jnp.maximum(i - 1, 0), 0)),
            pl.BlockSpec((nh, nk, tt), lambda j, i: (0, 0, j), pipeline_mode=once),
            pl.BlockSpec((nh, nk, tt), lambda j, i: (0, 0, j), pipeline_mode=once),
            pl.BlockSpec((nh, 1, na, tt), lambda j, i: (0, score_blk(i), 0, j)),
            pl.BlockSpec((nh, 1, na, tt), lambda j, i: (0, score_blk(i), 0, j)),
        ],
        out_specs=pl.BlockSpec((tt, d), lambda j, i: (j, 0)),
        out_shape=jax.ShapeDtypeStruct((t, d), F32),
        scratch_shapes=[pltpu.VMEM((ne, tt), F32), pltpu.VMEM((tt, ne), BF16), pltpu.VMEM((tt, ne), BF16)],
        compiler_params=_params(("parallel", "arbitrary")),
        name="peer_dense",
    )(h_t, u, v, r2, e2, n4, e14)


def _softplus(x):
    return jnp.maximum(x, 0.0) + jnp.log1p(jnp.exp(-jnp.abs(x)))


def _gdn_gates_kernel(ab_ref, alog_ref, dtb_ref, tril_ref, eg_ref, eb_ref, brep_ref, gcrep_ref):
    ab = ab_ref[...]
    g = -jnp.exp(alog_ref[...]) * _softplus(ab + dtb_ref[...])
    beta = jax.nn.sigmoid(ab)
    hi = lax.Precision.HIGHEST
    gc = jnp.dot(tril_ref[...], g, preferred_element_type=F32, precision=hi)
    gcrep_ref[...] = jnp.dot(gc, eg_ref[...], preferred_element_type=F32, precision=hi)
    brep_ref[...] = jnp.dot(beta, eb_ref[...], preferred_element_type=F32, precision=hi)


def gdn_gates(ab, a_log, dt_bias, tril):
    rows = ab.shape[0]
    tm = tril.shape[0]
    nh = a_log.shape[0]
    wide = nh * LANE
    pad = LANE - nh
    alog_p = jnp.pad(a_log, (0, pad)).reshape(1, LANE)
    dtb_p = jnp.pad(dt_bias, (0, pad)).reshape(1, LANE)
    head_of_col = jnp.arange(wide) // LANE
    lane = jnp.arange(LANE)[:, None]
    e_g = (lane == head_of_col[None, :]).astype(F32)
    e_b = (lane == head_of_col[None, :] + nh).astype(F32)
    tn = 1024
    return pl.pallas_call(
        _gdn_gates_kernel,
        grid=(rows // tm, wide // tn),
        in_specs=[
            pl.BlockSpec((tm, LANE), lambda i, j: (i, 0)),
            pl.BlockSpec((1, LANE), lambda i, j: (0, 0)),
            pl.BlockSpec((1, LANE), lambda i, j: (0, 0)),
            pl.BlockSpec((tm, tm), lambda i, j: (0, 0)),
            pl.BlockSpec((LANE, tn), lambda i, j: (0, j)),
            pl.BlockSpec((LANE, tn), lambda i, j: (0, j)),
        ],
        out_specs=[pl.BlockSpec((tm, tn), lambda i, j: (i, j)), pl.BlockSpec((tm, tn), lambda i, j: (i, j))],
        out_shape=[jax.ShapeDtypeStruct((rows, wide), F32), jax.ShapeDtypeStruct((rows, wide), F32)],
        compiler_params=_params(("parallel", "parallel")),
        name="gdn_gates",
    )(ab, alog_p, dtb_p, tril, e_g, e_b)


def _gdn_post_conv(y, o_ref, cb, n_qk_blocks, n_q_blocks):
    y = _silu(y)
    is_qk = cb < n_qk_blocks
    qscale = jnp.where(cb < n_q_blocks, GDN_DK ** -0.5, 1.0)
    for hh in range(y.shape[1] // GDN_DK):
        ls = slice(hh * GDN_DK, (hh + 1) * GDN_DK)
        seg = y[:, ls]
        rs = lax.rsqrt(jnp.sum(seg * seg, axis=-1, keepdims=True) + EPS)
        o_ref[:, ls] = seg * jnp.where(is_qk, rs * qscale, 1.0)


def _gdn_conv_kernel(x_ref, w_ref, o_ref, ext, *, tm, n_qk_blocks, n_q_blocks):
    cb = pl.program_id(0)
    tt = pl.program_id(2)

    @pl.when(tt == 0)
    def _():
        ext[0:8, :] = jnp.zeros((8, ext.shape[1]), F32)

    ext[8:8 + tm, :] = x_ref[...]
    w = w_ref[...]
    y = w[0:1] * ext[5:5 + tm, :]
    for tap in range(1, GDN_CONV):
        y = y + w[tap:tap + 1] * ext[5 + tap:5 + tap + tm, :]
    ext[0:8, :] = ext[tm:tm + 8, :]
    _gdn_post_conv(y, o_ref, cb, n_qk_blocks, n_q_blocks)


def gdn_conv_prompt(proj, n_prompt, seq, conv_w, n_ch, tm=512, tc=1024):
    bsz = n_prompt // seq
    nt = seq // tm
    qk = (2 * n_ch) // 3
    return pl.pallas_call(
        functools.partial(_gdn_conv_kernel, tm=tm, n_qk_blocks=qk // tc, n_q_blocks=qk // 2 // tc),
        grid=(n_ch // tc, bsz, nt),
        in_specs=[pl.BlockSpec((tm, tc), lambda cb, b, tt: (b * nt + tt, cb)),
                  pl.BlockSpec((GDN_CONV, tc), lambda cb, b, tt: (0, cb))],
        out_specs=pl.BlockSpec((tm, tc), lambda cb, b, tt: (b * nt + tt, cb)),
        out_shape=jax.ShapeDtypeStruct((n_prompt, n_ch), F32),
        scratch_shapes=[pltpu.VMEM((tm + 8, tc), F32)],
        compiler_params=_params(("parallel", "parallel", "arbitrary")),
        name="gdn_conv_prompt",
    )(proj, conv_w)


def _gdn_conv_sample_kernel(x_ref, buf_ref, w_ref, o_ref, *, n_qk_blocks, n_q_blocks):
    cb = pl.program_id(0)
    w = w_ref[...]
    y = w[0:1] * buf_ref[0]
    for tap in range(1, GDN_CONV - 1):
        y = y + w[tap:tap + 1] * buf_ref[tap]
    y = y + w[GDN_CONV - 1:GDN_CONV] * x_ref[...]
    _gdn_post_conv(y, o_ref, cb, n_qk_blocks, n_q_blocks)


def gdn_conv_sample(proj, n_prompt, buf_t, conv_w, n_ch, tc=1024):
    ns = buf_t.shape[1]
    row_blk = n_prompt // ns
    qk = (2 * n_ch) // 3
    return pl.pallas_call(
        functools.partial(_gdn_conv_sample_kernel, n_qk_blocks=qk // tc, n_q_blocks=qk // 2 // tc),
        grid=(n_ch // tc,),
        in_specs=[pl.BlockSpec((ns, tc), lambda cb: (row_blk, cb)),
                  pl.BlockSpec((GDN_CONV - 1, ns, tc), lambda cb: (0, 0, cb)),
                  pl.BlockSpec((GDN_CONV, tc), lambda cb: (0, cb))],
        out_specs=pl.BlockSpec((ns, tc), lambda cb: (0, cb)),
        out_shape=jax.ShapeDtypeStruct((ns, n_ch), F32),
        compiler_params=_params(("parallel",)),
        name="gdn_conv_sample",
    )(proj, buf_t, conv_w)


def _unit_lower_inverses(mats):
    c = mats[0].shape[0]
    row = lax.broadcasted_iota(jnp.int32, (c, c), 0)
    col = lax.broadcasted_iota(jnp.int32, (c, c), 1)
    eye = jnp.where(row == col, 1.0, 0.0)
    blk = 16
    ds = [jnp.where(row // blk == col // blk, a, 0.0) for a in mats]
    d2 = [_dot(d, d) for d in ds]
    d4 = [_dot(d, d) for d in d2]
    d8 = [_dot(d, d) for d in d4]
    ts = [_dot(eye - d, eye + x) for d, x in zip(ds, d2)]
    ts = [_dot(t, eye + x) for t, x in zip(ts, d4)]
    ts = [_dot(t, eye + x) for t, x in zip(ts, d8)]
    while blk < c:
        off_mask = (row // (2 * blk) == col // (2 * blk)) & (row // blk != col // blk)
        tmp = [_dot(t, jnp.where(off_mask, a, 0.0)) for t, a in zip(ts, mats)]
        tmp = [_dot(x, t) for x, t in zip(tmp, ts)]
        ts = [t - x for t, x in zip(ts, tmp)]
        blk *= 2
    return ts


def _gdn_chunks(qs, ks, vs, brs, gcs, zs, ng, ss):
    c = qs[0].shape[0]
    dv = vs[0].shape[1]
    row = lax.broadcasted_iota(jnp.int32, (c, c), 0)
    col = lax.broadcasted_iota(jnp.int32, (c, c), 1)
    kb = [k * b for k, b in zip(ks, brs)]
    vb = [v * b for v, b in zip(vs, brs)]
    eg = [jnp.exp(g) for g in gcs]
    glast = [g[c - 1:c, :] for g in gcs]
    kbg = [x * e for x, e in zip(kb, eg)]
    qg = [q * e for q, e in zip(qs, eg)]
    kdec = [k * jnp.exp(gl - g) for k, gl, g in zip(ks, glast, gcs)]
    decay = [jnp.exp(jnp.where(row >= col, g[:, :c] - g.T[:c, :], -jnp.inf)) for g in gcs]
    k16 = [k.astype(BF16) for k in ks]
    a = [_dot_nt(x.astype(BF16), k) * jnp.where(row > col, d, 0.0) for x, k, d in zip(kb, k16, decay)]
    attn = [_dot_nt(q.astype(BF16), k) * d for q, k, d in zip(qs, k16, decay)]
    ts = _unit_lower_inverses(a)
    uw = [_dot(t, jnp.concatenate([x, y], axis=1)) for t, x, y in zip(ts, vb, kbg)]
    s16 = [s.astype(BF16) for s in ss]
    v_new = [x[:, :dv] - _dot(x[:, dv:].astype(BF16), s) for x, s in zip(uw, s16)]
    vn16 = [x.astype(BF16) for x in v_new]
    o_state = [_dot(x.astype(BF16), s) for x, s in zip(qg, s16)]
    o_local = [_dot(x.astype(BF16), v) for x, v in zip(attn, vn16)]
    s_new = [s * jnp.exp(gl) + _dot_tn(x.astype(BF16), v) for s, gl, x, v in zip(ss, glast, kdec, vn16)]
    og = [_rms(x + y, ng) * _silu(z) for x, y, z in zip(o_state, o_local, zs)]
    return og, s_new


def _gdn_core_kernel(q_ref, k_ref, v_ref, b_ref, gc_ref, z_ref, ng_ref, og_ref, sout_ref, s_scr, *, hb, nck, c):
    n = pl.program_id(2)

    @pl.when(n == 0)
    def _():
        s_scr[...] = jnp.zeros_like(s_scr)

    ng = ng_ref[...]
    ss = [s_scr[hh] for hh in range(hb)]
    for ck in range(nck):
        rs = slice(ck * c, (ck + 1) * c)

        def heads(ref):
            return [ref[rs, hh * LANE:(hh + 1) * LANE] for hh in range(hb)]

        og, ss = _gdn_chunks(heads(q_ref), heads(k_ref), heads(v_ref), heads(b_ref), heads(gc_ref), heads(z_ref),
                             ng, ss)
        for hh in range(hb):
            og_ref[rs, hh * LANE:(hh + 1) * LANE] = og[hh].astype(og_ref.dtype)
    for hh in range(hb):
        s_scr[hh] = ss[hh]
        sout_ref[0, hh] = ss[hh]


def gdn_core_prompt(qkv, brep, gcrep, proj, n_prompt, seq, norm_g, hb=GDN_HEADS, nck=1):
    c = GDN_CHUNK
    nh = GDN_HEADS
    bsz = n_prompt // seq
    tm = c * nck
    nt = seq // tm
    bw = hb * LANE
    nhb = nh // hb

    def rows(b, h, n):
        return b * nt + n

    og, s_out = pl.pallas_call(
        functools.partial(_gdn_core_kernel, hb=hb, nck=nck, c=c),
        grid=(bsz, nhb, nt),
        in_specs=[
            pl.BlockSpec((tm, bw), lambda b, h, n: (rows(b, h, n), h)),
            pl.BlockSpec((tm, bw), lambda b, h, n: (rows(b, h, n), nhb + h)),
            pl.BlockSpec((tm, bw), lambda b, h, n: (rows(b, h, n), 2 * nhb + h)),
            pl.BlockSpec((tm, bw), lambda b, h, n: (rows(b, h, n), h)),
            pl.BlockSpec((tm, bw), lambda b, h, n: (rows(b, h, n), h)),
            pl.BlockSpec((tm, bw), lambda b, h, n: (rows(b, h, n), 3 * nhb + h)),
            pl.BlockSpec((1, LANE), lambda b, h, n: (0, 0)),
        ],
        out_specs=[
            pl.BlockSpec((tm, bw), lambda b, h, n: (rows(b, h, n), h)),
            pl.BlockSpec((1, hb, GDN_DK, LANE), lambda b, h, n: (b, h, 0, 0)),
        ],
        out_shape=[jax.ShapeDtypeStruct((n_prompt, nh * LANE), BF16),
                   jax.ShapeDtypeStruct((bsz, nh, GDN_DK, LANE), F32)],
        scratch_shapes=[pltpu.VMEM((hb, GDN_DK, LANE), F32)],
        compiler_params=_params(("parallel", "parallel", "arbitrary")),
        name="gdn_core_prompt",
    )(qkv, qkv, qkv, brep, gcrep, proj, norm_g.reshape(1, LANE))
    return og, s_out


def _gdn_sample_kernel(q_ref, k_ref, v_ref, b_ref, g_ref, z_ref, ng_ref, s_ref, og_ref, sout_ref, o_scr):
    ns = q_ref.shape[0]
    qt = q_ref[...].T
    kt = k_ref[...].T
    lane = lax.broadcasted_iota(jnp.int32, qt.shape, 1)

    def body(i, carry):
        pick = lane == i
        qcol = jnp.sum(jnp.where(pick, qt, 0.0), axis=1, keepdims=True)
        kcol = jnp.sum(jnp.where(pick, kt, 0.0), axis=1, keepdims=True)
        v = v_ref[pl.ds(i, 1), :]
        beta = b_ref[pl.ds(i, 1), :]
        eg = jnp.exp(g_ref[pl.ds(i, 1), :])
        sd = s_ref[i, 0] * eg
        ks = jnp.sum(sd * kcol, axis=0, keepdims=True)
        v_new = beta * (v - ks)
        s_new = sd + kcol * v_new
        sout_ref[i, 0] = s_new
        o_scr[pl.ds(i, 1), :] = jnp.sum(s_new * qcol, axis=0, keepdims=True)
        return carry

    lax.fori_loop(0, ns, body, 0, unroll=8)
    og_ref[...] = (_rms(o_scr[...], ng_ref[...]) * _silu(z_ref[...])).astype(og_ref.dtype)


def gdn_core_sample(qkv_s, brep_s, grep_s, proj, n_prompt, state, norm_g):
    ns, nh = state.shape[0], state.shape[1]
    row_blk = n_prompt // ns
    blk = pl.BlockSpec((ns, LANE), lambda h: (0, h))
    st = pl.BlockSpec((ns, 1, GDN_DK, LANE), lambda h: (0, h, 0, 0))
    return pl.pallas_call(
        _gdn_sample_kernel,
        grid=(nh,),
        in_specs=[
            blk,
            pl.BlockSpec((ns, LANE), lambda h: (0, nh + h)),
            pl.BlockSpec((ns, LANE), lambda h: (0, 2 * nh + h)),
            blk,
            blk,
            pl.BlockSpec((ns, LANE), lambda h: (row_blk, 3 * nh + h)),
            pl.BlockSpec((1, LANE), lambda h: (0, 0)),
            st,
        ],
        out_specs=[blk, st],
        out_shape=[jax.ShapeDtypeStruct((ns, nh * LANE), BF16), jax.ShapeDtypeStruct(state.shape, F32)],
        scratch_shapes=[pltpu.VMEM((ns, LANE), F32)],
        compiler_params=_params(("parallel",)),
        name="gdn_core_sample",
    )(qkv_s, qkv_s, qkv_s, brep_s, grep_s, proj, norm_g.reshape(1, LANE), state)


def _peer_ple(x, p16, norm_ffn, norm_ple, w_q, keys, emb_u, emb_v, ple_proj, ple_gate):
    h_t = rmsnorm(x, norm_ffn, BF16, transpose=True)
    r2, e2, n, e1 = peer_topk(h_t, w_q.T.astype(BF16), keys)
    o = peer_dense(h_t, emb_u.astype(BF16), emb_v.astype(BF16), r2, e2, n, e1)
    x2, hp = add_norm(x, o, norm_ple)
    return matmul(hp, ple_gate.astype(BF16), mode="ple", extras=(x2, p16, ple_proj.astype(BF16)))


def _chunk_tril(tm, c):
    r = jnp.arange(tm)
    return ((r[:, None] >= r[None, :]) & (r[:, None] // c == r[None, :] // c)).astype(F32)


def kernel(x_prompt, x_sample, p_prompt, p_sample, state_s5_re, state_s5_im, state_gdn, state_gdn_conv, norm_mix, norm_ffn, norm_ple, norm_final, ev_w_in, s5_a_re, s5_a_im, s5_log_dt, s5_b_re, s5_b_im, s5_c_re, s5_c_im, s5_d, s5_w_glu, sgu_ln_g, sgu_ln_b, sgu_w, sgu_b, ev_w_out, gdn_w_in, gdn_conv_w, gdn_a_log, gdn_dt_bias, gdn_norm_g, gdn_w_out, peer_w_q, peer_keys, peer_u, peer_v, ple_proj, ple_gate):
    bsz, seq, d = x_prompt.shape
    ns = x_sample.shape[0]
    n_prompt = bsz * seq
    x = jnp.concatenate([x_prompt.reshape(n_prompt, d), x_sample.reshape(ns, d)], axis=0)
    p16 = jnp.concatenate([p_prompt.reshape(2, n_prompt, -1), p_sample.reshape(2, ns, -1)], axis=1).astype(BF16)

    h = rmsnorm(x, norm_mix[0], BF16)
    proj = matmul(h, ev_w_in[0].astype(BF16))
    s5p = _s5_params(s5_a_re[0], s5_a_im[0], s5_log_dt[0], s5_b_re[0], s5_b_im[0], s5_c_re[0], s5_c_im[0])
    z_p, s5re_p, s5im_p = s5_prompt(proj, n_prompt, seq, s5p, s5_d[0])
    z_s, s5re_s, s5im_s = s5_sample(proj, n_prompt, state_s5_re[0], state_s5_im[0], s5p, s5_d[0])
    z = jnp.concatenate([z_p, z_s], axis=0)
    ya = matmul(z.astype(BF16), s5_w_glu[0].astype(BF16), mode="glu", extras=(z,), out_dtype=BF16)
    yb, v_last = sgu(proj, n_prompt, seq, sgu_ln_g[0], sgu_ln_b[0], sgu_w[0], sgu_b[0])
    x = matmul(jnp.concatenate([ya, yb], axis=1), ev_w_out[0].astype(BF16), mode="resid", extras=(x,))
    x = _peer_ple(x, p16[0], norm_ffn[0], norm_ple[0], peer_w_q[0], peer_keys[0], peer_u[0], peer_v[0],
                  ple_proj[0], ple_gate[0])

    nh = GDN_HEADS
    n_ch = gdn_conv_w.shape[-1]
    n_qkvz = n_ch + nh * LANE
    h = rmsnorm(x, norm_mix[1], BF16)
    w_in = gdn_w_in[0]
    proj = matmul(h, w_in[:, :n_qkvz].astype(BF16))
    w_ab = jnp.pad(w_in[:, n_qkvz:], ((0, 0), (0, LANE - 2 * nh))).astype(BF16)
    ab = matmul(h, w_ab)
    tm_g = 512
    brep_p, gcrep_p = gdn_gates(ab[:n_prompt], gdn_a_log[0], gdn_dt_bias[0], _chunk_tril(tm_g, GDN_CHUNK))
    brep_s, grep_s = gdn_gates(ab[n_prompt:], gdn_a_log[0], gdn_dt_bias[0], jnp.eye(ns, dtype=F32))
    qkv_p = gdn_conv_prompt(proj, n_prompt, seq, gdn_conv_w[0], n_ch)
    buf = state_gdn_conv[0]
    qkv_s = gdn_conv_sample(proj, n_prompt, jnp.swapaxes(buf, 0, 1), gdn_conv_w[0], n_ch)
    og_p, gdn_p = gdn_core_prompt(qkv_p, brep_p, gcrep_p, proj, n_prompt, seq, gdn_norm_g[0])
    og_s, gdn_s = gdn_core_sample(qkv_s, brep_s, grep_s, proj, n_prompt, state_gdn[0], gdn_norm_g[0])
    x = matmul(jnp.concatenate([og_p, og_s], axis=0), gdn_w_out[0].astype(BF16), mode="resid", extras=(x,))
    x = _peer_ple(x, p16[1], norm_ffn[1], norm_ple[1], peer_w_q[1], peer_keys[1], peer_u[1], peer_v[1],
                  ple_proj[1], ple_gate[1])

    y = rmsnorm(x, norm_final, F32)

    conv_p = jnp.stack([proj[(b + 1) * seq - (GDN_CONV - 1):(b + 1) * seq, :n_ch] for b in range(bsz)])
    conv_s = jnp.concatenate([buf[:, 1:], proj[n_prompt:, None, :n_ch]], axis=1)
    n_v = bsz * SGU_CHUNK
    return (
        y[:n_prompt].reshape(bsz, seq, d),
        y[n_prompt:].reshape(ns, 1, d),
        s5re_p[None], s5im_p[None], s5re_s[None], s5im_s[None],
        v_last[:n_v].reshape(1, bsz, SGU_CHUNK, -1),
        v_last[n_v:].reshape(1, ns, 1, -1),
        gdn_p[None], gdn_s[None],
        conv_p[None], conv_s[None],
    )
```

```python
import functools

import jax
import jax.numpy as jnp
from jax import lax
from jax.experimental import pallas as pl
from jax.experimental.pallas import tpu as pltpu

F32 = jnp.float32
BF16 = jnp.bfloat16
EPS = 1e-6
LANE = 128
V7X_VMEM_BYTES = 64 * 1024 * 1024
VMEM_LIMIT = V7X_VMEM_BYTES - 8 * 1024 * 1024

S5_GROUP = 16
S5_STATE = 64
SGU_GROUPS = 8
SGU_CHUNK = 128
GDN_HEADS = 32
GDN_DK = 128
GDN_CONV = 4
GDN_CHUNK = 64
PEER_HEADS = 8
PEER_NKEYS = 128
PEER_TOPK = 16
PEER_BLOCK_KEYS = 4
GATE_DTYPE = jnp.float32

TOK_TILE = 640
ROW_TILE = 128


def _params(sem, vmem=VMEM_LIMIT):
    return pltpu.CompilerParams(dimension_semantics=sem, vmem_limit_bytes=vmem)


def _dot(a, b):
    return jnp.dot(a, b, preferred_element_type=F32)


def _dot_nt(a, b):
    return lax.dot_general(a, b, (((1,), (1,)), ((), ())), preferred_element_type=F32)


def _dot_tn(a, b):
    return lax.dot_general(a, b, (((0,), (0,)), ((), ())), preferred_element_type=F32)


def _rms(x, g):
    return x * lax.rsqrt(jnp.mean(x * x, axis=-1, keepdims=True) + EPS) * g


def _silu(x):
    return x * jax.nn.sigmoid(x)


def _rmsnorm_kernel(x_ref, g_ref, o_ref, *, transpose):
    y = _rms(x_ref[...], g_ref[...])
    if transpose:
        y = y.T
    o_ref[...] = y.astype(o_ref.dtype)


def rmsnorm(x, g, out_dtype, transpose=False):
    t, d = x.shape
    tm = ROW_TILE
    if transpose:
        out_shape, out_spec = (d, t), pl.BlockSpec((d, tm), lambda i: (0, i))
    else:
        out_shape, out_spec = (t, d), pl.BlockSpec((tm, d), lambda i: (i, 0))
    return pl.pallas_call(
        functools.partial(_rmsnorm_kernel, transpose=transpose),
        grid=(t // tm,),
        in_specs=[pl.BlockSpec((tm, d), lambda i: (i, 0)), pl.BlockSpec((1, d), lambda i: (0, 0))],
        out_specs=out_spec,
        out_shape=jax.ShapeDtypeStruct(out_shape, out_dtype),
        compiler_params=_params(("parallel",)),
        name="rmsnorm_t" if transpose else "rmsnorm",
    )(x, g.reshape(1, d))


def _add_norm_kernel(x_ref, o_ref, g_ref, x2_ref, h_ref):
    x2 = x_ref[...] + o_ref[...]
    x2_ref[...] = x2
    h_ref[...] = _rms(x2, g_ref[...]).astype(h_ref.dtype)


def add_norm(x, o, g):
    t, d = x.shape
    tm = ROW_TILE
    row = pl.BlockSpec((tm, d), lambda i: (i, 0))
    return pl.pallas_call(
        _add_norm_kernel,
        grid=(t // tm,),
        in_specs=[row, row, pl.BlockSpec((1, d), lambda i: (0, 0))],
        out_specs=[row, row],
        out_shape=[jax.ShapeDtypeStruct((t, d), F32), jax.ShapeDtypeStruct((t, d), BF16)],
        compiler_params=_params(("parallel",)),
        name="add_norm",
    )(x, o, g.reshape(1, d))


def _mm_kernel(*refs, mode):
    x_ref, w_ref = refs[0], refs[1]
    o_ref = refs[-1]
    acc = _dot(x_ref[...], w_ref[...])
    if mode == "resid":
        acc = refs[2][...] + acc
    elif mode == "glu":
        z = refs[2][...]
        acc = z * jax.nn.sigmoid(acc)
    elif mode == "ple":
        r_ref, p_ref, pw_ref = refs[2], refs[3], refs[4]
        acc = r_ref[...] + jax.nn.sigmoid(acc) * _dot(p_ref[...], pw_ref[...])
    o_ref[...] = acc.astype(o_ref.dtype)


def matmul(x, w, *, mode="plain", extras=(), out_dtype=F32, tn=1024):
    m, k = x.shape
    n = w.shape[1]
    tm = TOK_TILE
    tn = min(tn, n)
    in_specs = [pl.BlockSpec((tm, k), lambda j, i: (i, 0)), pl.BlockSpec((k, tn), lambda j, i: (0, j))]
    if mode in ("resid", "glu", "ple"):
        in_specs.append(pl.BlockSpec((tm, tn), lambda j, i: (i, j)))
    if mode == "ple":
        kp = extras[1].shape[1]
        in_specs += [pl.BlockSpec((tm, kp), lambda j, i: (i, 0)), pl.BlockSpec((kp, tn), lambda j, i: (0, j))]
    return pl.pallas_call(
        functools.partial(_mm_kernel, mode=mode),
        grid=(n // tn, m // tm),
        in_specs=in_specs,
        out_specs=pl.BlockSpec((tm, tn), lambda j, i: (i, j)),
        out_shape=jax.ShapeDtypeStruct((m, n), out_dtype),
        compiler_params=_params(("parallel", "parallel")),
        name="mm_" + mode,
    )(x, w, *extras)


def _s5_prompt_kernel(x0, x1, x2, x3, bcat_ref, ccat_ref, lre_ref, lim_ref, d_ref,
                      z_ref, hre_out, him_out, u_scr, bu_scr, h_scr, y_scr, st_re, st_im, *, c):
    tc = pl.program_id(1)

    @pl.when(tc == 0)
    def _():
        st_re[...] = jnp.zeros_like(st_re)
        st_im[...] = jnp.zeros_like(st_im)

    xs = (x0, x1, x2, x3)
    half_w = st_re.shape[1]
    zeros = jnp.zeros((c, LANE), F32)
    for b in range(4):
        for half in range(2):
            s = 2 * b + half
            u_scr[half, pl.ds(s, c, stride=8), :] = xs[b][:, half * LANE:(half + 1) * LANE]
            u_scr[1 - half, pl.ds(s, c, stride=8), :] = zeros
    bu_scr[...] = _dot(jnp.concatenate([u_scr[0], u_scr[1]], axis=1), bcat_ref[0])
    lr = lre_ref[0]
    li = lim_ref[0]
    hr = st_re[...]
    hi = st_im[...]
    for t in range(c):
        rows = slice(8 * t, 8 * t + 8)
        nr = lr * hr - li * hi + bu_scr[rows, :half_w]
        ni = lr * hi + li * hr + bu_scr[rows, half_w:]
        h_scr[rows, :half_w] = nr
        h_scr[rows, half_w:] = ni
        hr, hi = nr, ni
    st_re[...] = hr
    st_im[...] = hi
    hre_out[0] = hr
    him_out[0] = hi
    y = _dot(h_scr[...], ccat_ref[0])
    odd = lax.broadcasted_iota(jnp.int32, (8 * c, LANE), 0) % 2 == 1
    y_scr[...] = jnp.where(odd, y[:, LANE:], y[:, :LANE])
    for b in range(4):
        for half in range(2):
            s = 2 * b + half
            ls = slice(half * LANE, (half + 1) * LANE)
            ys = y_scr[pl.ds(s, c, stride=8), :]
            z_ref[b, :, ls] = jax.nn.gelu(ys + d_ref[:, ls] * xs[b][:, ls])


def _s5_sample_kernel(x_ref, h0re_ref, h0im_ref, bblk_ref, cblk_ref, lre_ref, lim_ref, d_ref,
                      z_ref, hre_out, him_out):
    x = x_ref[...]
    half_w = h0re_ref.shape[1]
    bu = _dot(x, bblk_ref[0])
    lr = lre_ref[0]
    li = lim_ref[0]
    h0r = h0re_ref[...]
    h0i = h0im_ref[...]
    nr = lr * h0r - li * h0i + bu[:, :half_w]
    ni = lr * h0i + li * h0r + bu[:, half_w:]
    hre_out[...] = nr
    him_out[...] = ni
    y = _dot(nr, cblk_ref[0, :half_w, :]) + _dot(ni, cblk_ref[0, half_w:, :]) + d_ref[...] * x
    z_ref[...] = jax.nn.gelu(y)


def _s5_params(a_re, a_im, log_dt, b_re, b_im, c_re, c_im):
    g, p = a_re.shape
    nb = g // 8
    lam = lax.complex(a_re, a_im)
    dt = jnp.exp(log_dt)[:, None]
    lam_bar = jnp.exp(lam * dt)
    b_bar = ((lam_bar - 1.0) / lam)[..., None] * lax.complex(b_re, b_im)
    eye = jnp.eye(8, dtype=F32)

    def bmat(v):
        v = jnp.swapaxes(v, 1, 2).reshape(nb, 8, S5_GROUP, p)
        return jnp.einsum("kjcp,jl->kjclp", v, eye).reshape(nb, 8 * S5_GROUP, 8 * p)

    def cmat(v):
        v = jnp.swapaxes(v, 1, 2).reshape(nb, 8, p, S5_GROUP)
        return jnp.einsum("kjpc,jl->kjplc", v, eye).reshape(nb, 8 * p, 8 * S5_GROUP)

    bblk = jnp.concatenate([bmat(jnp.real(b_bar)), bmat(jnp.imag(b_bar))], axis=2)
    cblk = jnp.concatenate([cmat(c_re), cmat(-c_im)], axis=1)
    lre = jnp.real(lam_bar).reshape(nb, 8 * p)
    lim = jnp.imag(lam_bar).reshape(nb, 8 * p)
    return bblk, cblk, lre, lim


def s5_prompt(proj, n_prompt, seq, params, d_skip, c=128):
    bblk, cblk, lre, lim = params
    nb, _, two_w = bblk.shape
    half_w = two_w // 2
    bsz = n_prompt // seq
    assert bsz == 4
    w = nb * 8 * S5_GROUP
    ngb = nb // 2
    nt = seq // c
    lre_t = jnp.tile(lre.reshape(ngb, 2, half_w), (1, bsz, 1))
    lim_t = jnp.tile(lim.reshape(ngb, 2, half_w), (1, bsz, 1))
    bcat = bblk.reshape(ngb, 2 * LANE, two_w)
    ccat = jnp.swapaxes(cblk.reshape(ngb, 2, two_w, LANE), 1, 2).reshape(ngb, two_w, 2 * LANE)
    x_specs = [pl.BlockSpec((c, 2 * LANE), functools.partial(lambda gb, tc, b: (b * nt + tc, gb), b=b))
               for b in range(bsz)]
    z, hre, him = pl.pallas_call(
        functools.partial(_s5_prompt_kernel, c=c),
        grid=(ngb, nt),
        in_specs=x_specs + [
            pl.BlockSpec((1, 2 * LANE, two_w), lambda gb, tc: (gb, 0, 0)),
            pl.BlockSpec((1, two_w, 2 * LANE), lambda gb, tc: (gb, 0, 0)),
            pl.BlockSpec((1, 8, half_w), lambda gb, tc: (gb, 0, 0)),
            pl.BlockSpec((1, 8, half_w), lambda gb, tc: (gb, 0, 0)),
            pl.BlockSpec((1, 2 * LANE), lambda gb, tc: (0, gb)),
        ],
        out_specs=[
            pl.BlockSpec((bsz, c, 2 * LANE), lambda gb, tc: (0, tc, gb)),
            pl.BlockSpec((1, 8, half_w), lambda gb, tc: (gb, 0, 0)),
            pl.BlockSpec((1, 8, half_w), lambda gb, tc: (gb, 0, 0)),
        ],
        out_shape=[
            jax.ShapeDtypeStruct((bsz, seq, w), F32),
            jax.ShapeDtypeStruct((ngb, 8, half_w), F32),
            jax.ShapeDtypeStruct((ngb, 8, half_w), F32),
        ],
        scratch_shapes=[pltpu.VMEM((2, 8 * c, LANE), F32), pltpu.VMEM((8 * c, two_w), F32),
                        pltpu.VMEM((8 * c, two_w), F32), pltpu.VMEM((8 * c, LANE), F32),
                        pltpu.VMEM((8, half_w), F32), pltpu.VMEM((8, half_w), F32)],
        compiler_params=_params(("parallel", "arbitrary")),
        name="s5_prompt",
    )(proj, proj, proj, proj, bcat, ccat, lre_t, lim_t, d_skip.reshape(1, w))

    def states(h):
        h = h.reshape(ngb, bsz, 2, 8, S5_STATE)
        return jnp.transpose(h, (1, 0, 2, 3, 4)).reshape(bsz, nb * 8, S5_STATE)

    return z.reshape(n_prompt, w), states(hre), states(him)


def s5_sample(proj, n_prompt, h0_re, h0_im, params, d_skip):
    bblk, cblk, lre, lim = params
    nb, _, two_w = bblk.shape
    half_w = two_w // 2
    ns = h0_re.shape[0]
    w = nb * 8 * S5_GROUP
    row_blk = n_prompt // ns
    z, hre, him = pl.pallas_call(
        _s5_sample_kernel,
        grid=(nb,),
        in_specs=[
            pl.BlockSpec((ns, LANE), lambda k: (row_blk, k)),
            pl.BlockSpec((ns, half_w), lambda k: (0, k)),
            pl.BlockSpec((ns, half_w), lambda k: (0, k)),
            pl.BlockSpec((1, LANE, two_w), lambda k: (k, 0, 0)),
            pl.BlockSpec((1, two_w, LANE), lambda k: (k, 0, 0)),
            pl.BlockSpec((1, 1, half_w), lambda k: (k, 0, 0)),
            pl.BlockSpec((1, 1, half_w), lambda k: (k, 0, 0)),
            pl.BlockSpec((1, LANE), lambda k: (0, k)),
        ],
        out_specs=[
            pl.BlockSpec((ns, LANE), lambda k: (0, k)),
            pl.BlockSpec((ns, half_w), lambda k: (0, k)),
            pl.BlockSpec((ns, half_w), lambda k: (0, k)),
        ],
        out_shape=[
            jax.ShapeDtypeStruct((ns, w), F32),
            jax.ShapeDtypeStruct((ns, nb * half_w), F32),
            jax.ShapeDtypeStruct((ns, nb * half_w), F32),
        ],
        compiler_params=_params(("parallel",)),
        name="s5_sample",
    )(proj, h0_re.reshape(ns, -1), h0_im.reshape(ns, -1), bblk, cblk,
      lre.reshape(nb, 1, half_w), lim.reshape(nb, 1, half_w), d_skip.reshape(1, w))
    return z, hre.reshape(ns, nb * 8, S5_STATE), him.reshape(ns, nb * 8, S5_STATE)


def _sgu_kernel(u_ref, v_ref, g_ref, b_ref, w_ref, bias_ref, y_ref, vl_ref):
    u = jax.nn.gelu(u_ref[...])
    v = jax.nn.gelu(v_ref[...])
    mu = jnp.mean(v, axis=-1, keepdims=True)
    vc = v - mu
    vn = vc * lax.rsqrt(jnp.mean(vc * vc, axis=-1, keepdims=True) + EPS) * g_ref[...] + b_ref[...]
    vl_ref[...] = vn
    c = u.shape[0]
    dg = u.shape[1] // SGU_GROUPS
    causal = lax.broadcasted_iota(jnp.int32, (c, c), 0) >= lax.broadcasted_iota(jnp.int32, (c, c), 1)
    for g in range(SGU_GROUPS):
        ls = slice(g * dg, (g + 1) * dg)
        w = jnp.where(causal, w_ref[0, g], 0.0).astype(BF16)
        mixed = _dot(w, vn[:, ls].astype(BF16)) + bias_ref[0, g]
        y_ref[:, ls] = (u[:, ls] * mixed).astype(y_ref.dtype)


def sgu(proj, n_prompt, seq, ln_g, ln_b, w_s, b_s):
    t = proj.shape[0]
    wdt = proj.shape[1] // 3
    c = SGU_CHUNK
    dg = wdt // SGU_GROUPS
    n_chunks = t // c
    per_seq = seq // c
    n_prompt_chunks = n_prompt // c
    eye = jnp.eye(c, dtype=F32)
    w_sets = jnp.stack([w_s, w_s[:, :1, :1] * eye])
    bias_sets = jnp.stack([jnp.broadcast_to(b_s[:, :, None], (SGU_GROUPS, c, dg)),
                           jnp.broadcast_to(b_s[:, :1, None], (SGU_GROUPS, c, dg))])
    n_last = n_prompt // seq + (n_chunks - n_prompt_chunks)
    y, vl = pl.pallas_call(
        _sgu_kernel,
        grid=(n_chunks,),
        in_specs=[
            pl.BlockSpec((c, wdt), lambda i: (i, 1)),
            pl.BlockSpec((c, wdt), lambda i: (i, 2)),
            pl.BlockSpec((1, wdt), lambda i: (0, 0)),
            pl.BlockSpec((1, wdt), lambda i: (0, 0)),
            pl.BlockSpec((1, SGU_GROUPS, c, c), lambda i: (i // n_prompt_chunks, 0, 0, 0)),
            pl.BlockSpec((1, SGU_GROUPS, c, dg), lambda i: (i // n_prompt_chunks, 0, 0, 0)),
        ],
        out_specs=[
            pl.BlockSpec((c, wdt), lambda i: (i, 0)),
            pl.BlockSpec((c, wdt), lambda i: (i // per_seq, 0)),
        ],
        out_shape=[jax.ShapeDtypeStruct((t, wdt), BF16), jax.ShapeDtypeStruct((n_last * c, wdt), F32)],
        compiler_params=_params(("arbitrary",)),
        name="sgu",
    )(proj, proj, ln_g.reshape(1, wdt), ln_b.reshape(1, wdt), w_sets, bias_sets)
    return y, vl


def _extract16(arrays):
    l = arrays[0].shape[1]
    viota = lax.broadcasted_iota(jnp.int32, (PEER_TOPK, l), 0)

    def body(r, carry):
        out = []
        for s, vals, idxs in carry:
            iota = lax.broadcasted_iota(jnp.int32, s.shape, 0)
            m = jnp.max(s, axis=0, keepdims=True)
            idx = jnp.min(jnp.where(s == m, iota, s.shape[0]), axis=0, keepdims=True)
            out.append((jnp.where(iota == idx, -jnp.inf, s), jnp.where(viota == r, m, vals),
                        jnp.where(viota == r, idx, idxs)))
        return tuple(out)

    init = tuple((s, jnp.zeros((PEER_TOPK, l), F32), jnp.zeros((PEER_TOPK, l), jnp.int32)) for s in arrays)
    return lax.fori_loop(0, PEER_TOPK, body, init)


def _ranks(idxs, n):
    iota = lax.broadcasted_iota(jnp.int32, (n, idxs.shape[1]), 0)
    rank = jnp.full(iota.shape, PEER_TOPK, jnp.int32)
    for r in range(PEER_TOPK):
        rank = jnp.where(iota == idxs[r:r + 1], r, rank)
    return rank


def _peer_candidates(t1, t2):
    k = PEER_TOPK
    sub = lax.broadcasted_iota(jnp.int32, (8, t1.shape[1]), 0)
    groups = [t1[0:1] + t2[0:8], t1[0:1] + t2[8:16], t1[1:2] + t2[0:8]]
    slices = [slice(0, 16), slice(16, 24)]
    for i in range(2, 8):
        groups.append(jnp.where(sub < k // (i + 1), t1[i:i + 1] + t2[0:8], -jnp.inf))
        slices.append(slice(8 * (i + 1), 8 * (i + 2)))
    groups.append(t1[8:16] + t2[0:1])
    slices += [slice(64 + i, 65 + i) for i in range(8, k)]
    return jnp.concatenate(groups, axis=0), slices


def _peer_topk_kernel(ht_ref, wq_ref, keys_ref, r2_ref, e2_ref, n_ref, e1_ref):
    qt = _dot_tn(wq_ref[...], ht_ref[...]).astype(BF16)
    nk = keys_ref.shape[2]
    dq = keys_ref.shape[3]
    s1_all = _dot(keys_ref[0, 0].astype(BF16), qt[:dq])
    s2_all = _dot(keys_ref[0, 1].astype(BF16), qt[dq:])
    for c in range(ht_ref.shape[1] // LANE):
        cs = slice(c * LANE, (c + 1) * LANE)
        s1 = s1_all[:, cs]
        s2 = s2_all[:, cs]
        (_, t1, idx1), (_, t2, idx2) = _extract16([s1, s2])
        rank1 = _ranks(idx1, nk)
        rank2 = _ranks(idx2, nk)
        cand, cand_rows = _peer_candidates(t1, t2)
        (cand_left, _, _), = _extract16([cand])
        selected = (cand_left == -jnp.inf) & (cand > -jnp.inf)
        cmax = t1[0:1] + t2[0:1]
        zsum = jnp.sum(jnp.where(selected, jnp.exp(cand - cmax), 0.0), axis=0, keepdims=True)
        n = jnp.zeros((nk, LANE), F32)
        for i in range(PEER_TOPK):
            cnt = jnp.sum(jnp.where(selected[cand_rows[i]], 1.0, 0.0), axis=0, keepdims=True)
            n = jnp.where(rank1 == i, cnt, n)
        r2_ref[0, :, cs] = rank2.astype(F32).astype(r2_ref.dtype)
        n_ref[0, :, cs] = n.astype(n_ref.dtype)
        e1_ref[0, :, cs] = jnp.exp(s1 - t1[0:1]).astype(e1_ref.dtype)
        e2_ref[0, :, cs] = (jnp.exp(s2 - t2[0:1]) / zsum).astype(e2_ref.dtype)


def peer_topk(h_t, wq, keys):
    d, t = h_t.shape
    nh, _, nk, dq = keys.shape
    tt = TOK_TILE
    spec = pl.BlockSpec((1, nk, tt), lambda j, h: (h, 0, j))
    shp = jax.ShapeDtypeStruct((nh, nk, t), GATE_DTYPE)
    return pl.pallas_call(
        _peer_topk_kernel,
        grid=(t // tt, nh),
        in_specs=[
            pl.BlockSpec((d, tt), lambda j, h: (0, j)),
            pl.BlockSpec((d, 2 * dq), lambda j, h: (0, h)),
            pl.BlockSpec((1, 2, nk, dq), lambda j, h: (h, 0, 0, 0)),
        ],
        out_specs=[spec, spec, spec, spec],
        out_shape=[shp, shp, shp, shp],
        compiler_params=_params(("parallel", "arbitrary")),
        name="peer_topk",
    )(h_t, wq, keys)


def _peer_dense_kernel(ht_ref, u_ref, v_ref, r2_ref, e2_ref, n_ref, e1_ref, o_ref, s_a, s_b, act_scr,
                       *, na, nh, nb):
    i = pl.program_id(1)
    tt = ht_ref.shape[1]
    nk = r2_ref.shape[1]
    d = o_ref.shape[1]
    cb = 512
    tiles = [(a, c) for a in range(na) for c in range(tt // LANE)]
    n_parts = 2

    @pl.when(i == 0)
    def _():
        o_ref[...] = jnp.zeros_like(o_ref)
        s_b[...] = jnp.zeros_like(s_b)

    def step(s_prev, s_cur):
        rows = s_cur.shape[0] // n_parts
        for part in range(n_parts):
            ps = slice(part * rows, (part + 1) * rows)
            s_cur[ps, :] = _dot(u_ref[ps, :], ht_ref[...])
            for a, c in tiles[part * len(tiles) // n_parts:(part + 1) * len(tiles) // n_parts]:
                rs = slice(a * nk, (a + 1) * nk)
                cs = slice(c * LANE, (c + 1) * LANE)
                w = jnp.zeros((nk, LANE), GATE_DTYPE)
                for h in range(nh):
                    nrow = n_ref[h, 0, a:a + 1, cs]
                    e1row = e1_ref[h, 0, a:a + 1, cs]
                    w = w + jnp.where(r2_ref[h, :, cs] < nrow, e2_ref[h, :, cs] * e1row, jnp.zeros((), GATE_DTYPE))
                act_scr[cs, rs] = (jax.nn.gelu(s_prev[rs, cs]) * w.astype(F32)).T.astype(BF16)
        for r in range(d // cb):
            o_ref[:, r * cb:(r + 1) * cb] += _dot(act_scr[...], v_ref[:, r * cb:(r + 1) * cb])

    @pl.when(i % 2 == 0)
    def _():
        step(s_b, s_a)

    @pl.when(i % 2 == 1)
    def _():
        step(s_a, s_b)


def peer_dense(h_t, u, v, r2, e2, n, e1):
    d, t = h_t.shape
    e = u.shape[0]
    nh, nk, _ = r2.shape
    tt = TOK_TILE
    na = PEER_BLOCK_KEYS
    ne = na * nk
    nb = e // ne
    n4 = n.reshape(nh, nk // na, na, t)
    e14 = e1.reshape(nh, nk // na, na, t)
    once = pl.Buffered(1)

    def score_blk(i):
        return jnp.minimum(i, nb - 1)

    def value_blk(i):
        return jnp.maximum(i - 1, 0)

    return pl.pallas_call(
        functools.partial(_peer_dense_kernel, na=na, nh=nh, nb=nb),
        grid=(t // tt, nb + 1),
        in_specs=[
            pl.BlockSpec((d, tt), lambda j, i: (0, j), pipeline_mode=once),
            pl.BlockSpec((ne, d), lambda j, i: (score_blk(i), 0)),
            pl.BlockSpec((ne, d), lambda j, i: (value_blk(i), 0)),
            pl.BlockSpec((nh, nk, tt), lambda j, i: (0, 0, j), pipeline_mode=once),
            pl.BlockSpec((nh, nk, tt), lambda j, i: (0, 0, j), pipeline_mode=once),
            pl.BlockSpec((nh, 1, na, tt), lambda j, i: (0, value_blk(i), 0, j)),
            pl.BlockSpec((nh, 1, na, tt), lambda j, i: (0, value_blk(i), 0, j)),
        ],
        out_specs=pl.BlockSpec((tt, d), lambda j, i: (j, 0)),
        out_shape=jax.ShapeDtypeStruct((t, d), F32),
        scratch_shapes=[pltpu.VMEM((ne, tt), F32), pltpu.VMEM((ne, tt), F32), pltpu.VMEM((tt, ne), BF16)],
        compiler_params=_params(("parallel", "arbitrary")),
        name="peer_dense",
    )(h_t, u, v, r2, e2, n4, e14)


def _softplus(x):
    return jnp.maximum(x, 0.0) + jnp.log1p(jnp.exp(-jnp.abs(x)))


def _replicate(x, onehot):
    hi = x.astype(BF16)
    rest = x - hi.astype(F32)
    mid = rest.astype(BF16)
    lo = (rest - mid.astype(F32)).astype(BF16)
    sel = onehot.astype(BF16)
    return (_dot(hi, sel) + _dot(mid, sel)) + _dot(lo, sel)


def _gdn_gates_kernel(ab_ref, alog_ref, dtb_ref, tril_ref, eg_ref, eb_ref, brep_ref, gcrep_ref):
    ab = ab_ref[...]
    g = -jnp.exp(alog_ref[...]) * _softplus(ab + dtb_ref[...])
    beta = jax.nn.sigmoid(ab)
    gc = jnp.dot(tril_ref[...], g, preferred_element_type=F32, precision=lax.Precision.HIGHEST)
    gcrep_ref[...] = _replicate(gc, eg_ref[...])
    brep_ref[...] = _replicate(beta, eb_ref[...])


def gdn_gates(ab, a_log, dt_bias, tril):
    rows = ab.shape[0]
    tm = tril.shape[0]
    nh = a_log.shape[0]
    wide = nh * LANE
    pad = LANE - nh
    alog_p = jnp.pad(a_log, (0, pad)).reshape(1, LANE)
    dtb_p = jnp.pad(dt_bias, (0, pad)).reshape(1, LANE)
    head_of_col = jnp.arange(wide) // LANE
    lane = jnp.arange(LANE)[:, None]
    e_g = (lane == head_of_col[None, :]).astype(F32)
    e_b = (lane == head_of_col[None, :] + nh).astype(F32)
    tn = 1024
    return pl.pallas_call(
        _gdn_gates_kernel,
        grid=(rows // tm, wide // tn),
        in_specs=[
            pl.BlockSpec((tm, LANE), lambda i, j: (i, 0)),
            pl.BlockSpec((1, LANE), lambda i, j: (0, 0)),
            pl.BlockSpec((1, LANE), lambda i, j: (0, 0)),
            pl.BlockSpec((tm, tm), lambda i, j: (0, 0)),
            pl.BlockSpec((LANE, tn), lambda i, j: (0, j)),
            pl.BlockSpec((LANE, tn), lambda i, j: (0, j)),
        ],
        out_specs=[pl.BlockSpec((tm, tn), lambda i, j: (i, j)), pl.BlockSpec((tm, tn), lambda i, j: (i, j))],
        out_shape=[jax.ShapeDtypeStruct((rows, wide), F32), jax.ShapeDtypeStruct((rows, wide), F32)],
        compiler_params=_params(("parallel", "parallel")),
        name="gdn_gates",
    )(ab, alog_p, dtb_p, tril, e_g, e_b)


def _gdn_post_conv(y, o_ref, cb, n_qk_blocks, n_q_blocks):
    y = _silu(y)
    is_qk = cb < n_qk_blocks
    qscale = jnp.where(cb < n_q_blocks, GDN_DK ** -0.5, 1.0)
    for hh in range(y.shape[1] // GDN_DK):
        ls = slice(hh * GDN_DK, (hh + 1) * GDN_DK)
        seg = y[:, ls]
        rs = lax.rsqrt(jnp.sum(seg * seg, axis=-1, keepdims=True) + EPS)
        o_ref[:, ls] = seg * jnp.where(is_qk, rs * qscale, 1.0)


def _gdn_conv_kernel(x_ref, w_ref, o_ref, ext, *, tm, n_qk_blocks, n_q_blocks):
    cb = pl.program_id(0)
    tt = pl.program_id(2)

    @pl.when(tt == 0)
    def _():
        ext[0:8, :] = jnp.zeros((8, ext.shape[1]), F32)

    ext[8:8 + tm, :] = x_ref[...]
    w = w_ref[...]
    y = w[0:1] * ext[5:5 + tm, :]
    for tap in range(1, GDN_CONV):
        y = y + w[tap:tap + 1] * ext[5 + tap:5 + tap + tm, :]
    ext[0:8, :] = ext[tm:tm + 8, :]
    _gdn_post_conv(y, o_ref, cb, n_qk_blocks, n_q_blocks)


def gdn_conv_prompt(proj, n_prompt, seq, conv_w, n_ch, tm=512, tc=1024):
    bsz = n_prompt // seq
    nt = seq // tm
    qk = (2 * n_ch) // 3
    return pl.pallas_call(
        functools.partial(_gdn_conv_kernel, tm=tm, n_qk_blocks=qk // tc, n_q_blocks=qk // 2 // tc),
        grid=(n_ch // tc, bsz, nt),
        in_specs=[pl.BlockSpec((tm, tc), lambda cb, b, tt: (b * nt + tt, cb)),
                  pl.BlockSpec((GDN_CONV, tc), lambda cb, b, tt: (0, cb))],
        out_specs=pl.BlockSpec((tm, tc), lambda cb, b, tt: (b * nt + tt, cb)),
        out_shape=jax.ShapeDtypeStruct((n_prompt, n_ch), F32),
        scratch_shapes=[pltpu.VMEM((tm + 8, tc), F32)],
        compiler_params=_params(("parallel", "parallel", "arbitrary")),
        name="gdn_conv_prompt",
    )(proj, conv_w)


def _gdn_conv_sample_kernel(x_ref, buf_ref, w_ref, o_ref, *, n_qk_blocks, n_q_blocks):
    cb = pl.program_id(0)
    w = w_ref[...]
    y = w[0:1] * buf_ref[0]
    for tap in range(1, GDN_CONV - 1):
        y = y + w[tap:tap + 1] * buf_ref[tap]
    y = y + w[GDN_CONV - 1:GDN_CONV] * x_ref[...]
    _gdn_post_conv(y, o_ref, cb, n_qk_blocks, n_q_blocks)


def gdn_conv_sample(proj, n_prompt, buf_t, conv_w, n_ch, tc=1024):
    ns = buf_t.shape[1]
    row_blk = n_prompt // ns
    qk = (2 * n_ch) // 3
    return pl.pallas_call(
        functools.partial(_gdn_conv_sample_kernel, n_qk_blocks=qk // tc, n_q_blocks=qk // 2 // tc),
        grid=(n_ch // tc,),
        in_specs=[pl.BlockSpec((ns, tc), lambda cb: (row_blk, cb)),
                  pl.BlockSpec((GDN_CONV - 1, ns, tc), lambda cb: (0, 0, cb)),
                  pl.BlockSpec((GDN_CONV, tc), lambda cb: (0, cb))],
        out_specs=pl.BlockSpec((ns, tc), lambda cb: (0, cb)),
        out_shape=jax.ShapeDtypeStruct((ns, n_ch), F32),
        compiler_params=_params(("parallel",)),
        name="gdn_conv_sample",
    )(proj, buf_t, conv_w)


def _unit_lower_inverses(mats):
    c = mats[0].shape[0]
    row = lax.broadcasted_iota(jnp.int32, (c, c), 0)
    col = lax.broadcasted_iota(jnp.int32, (c, c), 1)
    eye = jnp.where(row == col, 1.0, 0.0)
    blk = 16
    ds = [jnp.where(row // blk == col // blk, a, 0.0) for a in mats]
    d2 = [_dot(d, d) for d in ds]
    d4 = [_dot(d, d) for d in d2]
    d8 = [_dot(d, d) for d in d4]
    ts = [_dot(eye - d, eye + x) for d, x in zip(ds, d2)]
    ts = [_dot(t, eye + x) for t, x in zip(ts, d4)]
    ts = [_dot(t, eye + x) for t, x in zip(ts, d8)]
    while blk < c:
        off_mask = (row // (2 * blk) == col // (2 * blk)) & (row // blk != col // blk)
        tmp = [_dot(t, jnp.where(off_mask, a, 0.0)) for t, a in zip(ts, mats)]
        tmp = [_dot(x, t) for x, t in zip(tmp, ts)]
        ts = [t - x for t, x in zip(ts, tmp)]
        blk *= 2
    return ts


def _gdn_chunks(qs, ks, vs, brs, gcs, zs, ng, ss):
    c = qs[0].shape[0]
    dv = vs[0].shape[1]
    row = lax.broadcasted_iota(jnp.int32, (c, c), 0)
    col = lax.broadcasted_iota(jnp.int32, (c, c), 1)
    kb = [k * b for k, b in zip(ks, brs)]
    vb = [v * b for v, b in zip(vs, brs)]
    eg = [jnp.exp(g) for g in gcs]
    glast = [g[c - 1:c, :] for g in gcs]
    kbg = [x * e for x, e in zip(kb, eg)]
    qg = [q * e for q, e in zip(qs, eg)]
    kdec = [k * jnp.exp(gl - g) for k, gl, g in zip(ks, glast, gcs)]
    decay = [jnp.exp(jnp.where(row >= col, g[:, :c] - g.T[:c, :], -jnp.inf)) for g in gcs]
    k16 = [k.astype(BF16) for k in ks]
    a = [_dot_nt(x.astype(BF16), k) * jnp.where(row > col, d, 0.0) for x, k, d in zip(kb, k16, decay)]
    attn = [_dot_nt(q.astype(BF16), k) * d for q, k, d in zip(qs, k16, decay)]
    ts = _unit_lower_inverses(a)
    uw = [_dot(t, jnp.concatenate([x, y], axis=1)) for t, x, y in zip(ts, vb, kbg)]
    s16 = [s.astype(BF16) for s in ss]
    v_new = [x[:, :dv] - _dot(x[:, dv:].astype(BF16), s) for x, s in zip(uw, s16)]
    vn16 = [x.astype(BF16) for x in v_new]
    o_state = [_dot(x.astype(BF16), s) for x, s in zip(qg, s16)]
    o_local = [_dot(x.astype(BF16), v) for x, v in zip(attn, vn16)]
    s_new = [s * jnp.exp(gl) + _dot_tn(x.astype(BF16), v) for s, gl, x, v in zip(ss, glast, kdec, vn16)]
    og = [_rms(x + y, ng) * _silu(z) for x, y, z in zip(o_state, o_local, zs)]
    return og, s_new


def _gdn_core_kernel(q_ref, k_ref, v_ref, b_ref, gc_ref, z_ref, ng_ref, og_ref, sout_ref, s_scr, *, hb, nck, c):
    n = pl.program_id(2)

    @pl.when(n == 0)
    def _():
        s_scr[...] = jnp.zeros_like(s_scr)

    ng = ng_ref[...]
    ss = [s_scr[hh] for hh in range(hb)]
    for ck in range(nck):
        rs = slice(ck * c, (ck + 1) * c)

        def heads(ref):
            return [ref[rs, hh * LANE:(hh + 1) * LANE] for hh in range(hb)]

        og, ss = _gdn_chunks(heads(q_ref), heads(k_ref), heads(v_ref), heads(b_ref), heads(gc_ref), heads(z_ref),
                             ng, ss)
        for hh in range(hb):
            og_ref[rs, hh * LANE:(hh + 1) * LANE] = og[hh].astype(og_ref.dtype)
    for hh in range(hb):
        s_scr[hh] = ss[hh]
        sout_ref[0, hh] = ss[hh]


def gdn_core_prompt(qkv, brep, gcrep, proj, n_prompt, seq, norm_g, hb=GDN_HEADS, nck=1):
    c = GDN_CHUNK
    nh = GDN_HEADS
    bsz = n_prompt // seq
    tm = c * nck
    nt = seq // tm
    bw = hb * LANE
    nhb = nh // hb

    def rows(b, h, n):
        return b * nt + n

    og, s_out = pl.pallas_call(
        functools.partial(_gdn_core_kernel, hb=hb, nck=nck, c=c),
        grid=(bsz, nhb, nt),
        in_specs=[
            pl.BlockSpec((tm, bw), lambda b, h, n: (rows(b, h, n), h)),
            pl.BlockSpec((tm, bw), lambda b, h, n: (rows(b, h, n), nhb + h)),
            pl.BlockSpec((tm, bw), lambda b, h, n: (rows(b, h, n), 2 * nhb + h)),
            pl.BlockSpec((tm, bw), lambda b, h, n: (rows(b, h, n), h)),
            pl.BlockSpec((tm, bw), lambda b, h, n: (rows(b, h, n), h)),
            pl.BlockSpec((tm, bw), lambda b, h, n: (rows(b, h, n), 3 * nhb + h)),
            pl.BlockSpec((1, LANE), lambda b, h, n: (0, 0)),
        ],
        out_specs=[
            pl.BlockSpec((tm, bw), lambda b, h, n: (rows(b, h, n), h)),
            pl.BlockSpec((1, hb, GDN_DK, LANE), lambda b, h, n: (b, h, 0, 0)),
        ],
        out_shape=[jax.ShapeDtypeStruct((n_prompt, nh * LANE), BF16),
                   jax.ShapeDtypeStruct((bsz, nh, GDN_DK, LANE), F32)],
        scratch_shapes=[pltpu.VMEM((hb, GDN_DK, LANE), F32)],
        compiler_params=_params(("parallel", "parallel", "arbitrary")),
        name="gdn_core_prompt",
    )(qkv, qkv, qkv, brep, gcrep, proj, norm_g.reshape(1, LANE))
    return og, s_out


def _gdn_sample_kernel(q_ref, k_ref, v_ref, b_ref, g_ref, z_ref, ng_ref, s_ref, og_ref, sout_ref, o_scr):
    ns = q_ref.shape[0]
    qt = q_ref[...].T
    kt = k_ref[...].T
    lane = lax.broadcasted_iota(jnp.int32, qt.shape, 1)

    def body(i, carry):
        pick = lane == i
        qcol = jnp.sum(jnp.where(pick, qt, 0.0), axis=1, keepdims=True)
        kcol = jnp.sum(jnp.where(pick, kt, 0.0), axis=1, keepdims=True)
        v = v_ref[pl.ds(i, 1), :]
        beta = b_ref[pl.ds(i, 1), :]
        eg = jnp.exp(g_ref[pl.ds(i, 1), :])
        sd = s_ref[i, 0] * eg
        ks = jnp.sum(sd * kcol, axis=0, keepdims=True)
        v_new = beta * (v - ks)
        s_new = sd + kcol * v_new
        sout_ref[i, 0] = s_new
        o_scr[pl.ds(i, 1), :] = jnp.sum(s_new * qcol, axis=0, keepdims=True)
        return carry

    lax.fori_loop(0, ns, body, 0, unroll=8)
    og_ref[...] = (_rms(o_scr[...], ng_ref[...]) * _silu(z_ref[...])).astype(og_ref.dtype)


def gdn_core_sample(qkv_s, brep_s, grep_s, proj, n_prompt, state, norm_g):
    ns, nh = state.shape[0], state.shape[1]
    row_blk = n_prompt // ns
    blk = pl.BlockSpec((ns, LANE), lambda h: (0, h))
    st = pl.BlockSpec((ns, 1, GDN_DK, LANE), lambda h: (0, h, 0, 0))
    return pl.pallas_call(
        _gdn_sample_kernel,
        grid=(nh,),
        in_specs=[
            blk,
            pl.BlockSpec((ns, LANE), lambda h: (0, nh + h)),
            pl.BlockSpec((ns, LANE), lambda h: (0, 2 * nh + h)),
            blk,
            blk,
            pl.BlockSpec((ns, LANE), lambda h: (row_blk, 3 * nh + h)),
            pl.BlockSpec((1, LANE), lambda h: (0, 0)),
            st,
        ],
        out_specs=[blk, st],
        out_shape=[jax.ShapeDtypeStruct((ns, nh * LANE), BF16), jax.ShapeDtypeStruct(state.shape, F32)],
        scratch_shapes=[pltpu.VMEM((ns, LANE), F32)],
        compiler_params=_params(("parallel",)),
        name="gdn_core_sample",
    )(qkv_s, qkv_s, qkv_s, brep_s, grep_s, proj, norm_g.reshape(1, LANE), state)


def _peer_ple(x, p16, norm_ffn, norm_ple, w_q, keys, emb_u, emb_v, ple_proj, ple_gate):
    h_t = rmsnorm(x, norm_ffn, BF16, transpose=True)
    r2, e2, n, e1 = peer_topk(h_t, w_q.astype(BF16), keys)
    o = peer_dense(h_t, emb_u.astype(BF16), emb_v.astype(BF16), r2, e2, n, e1)
    x2, hp = add_norm(x, o, norm_ple)
    return matmul(hp, ple_gate.astype(BF16), mode="ple", extras=(x2, p16, ple_proj.astype(BF16)))


def _chunk_tril(tm, c):
    r = jnp.arange(tm)
    return ((r[:, None] >= r[None, :]) & (r[:, None] // c == r[None, :] // c)).astype(F32)


def kernel(x_prompt, x_sample, p_prompt, p_sample, state_s5_re, state_s5_im, state_gdn, state_gdn_conv, norm_mix, norm_ffn, norm_ple, norm_final, ev_w_in, s5_a_re, s5_a_im, s5_log_dt, s5_b_re, s5_b_im, s5_c_re, s5_c_im, s5_d, s5_w_glu, sgu_ln_g, sgu_ln_b, sgu_w, sgu_b, ev_w_out, gdn_w_in, gdn_conv_w, gdn_a_log, gdn_dt_bias, gdn_norm_g, gdn_w_out, peer_w_q, peer_keys, peer_u, peer_v, ple_proj, ple_gate):
    bsz, seq, d = x_prompt.shape
    ns = x_sample.shape[0]
    n_prompt = bsz * seq
    x = jnp.concatenate([x_prompt.reshape(n_prompt, d), x_sample.reshape(ns, d)], axis=0)
    p16 = jnp.concatenate([p_prompt.reshape(2, n_prompt, -1), p_sample.reshape(2, ns, -1)], axis=1).astype(BF16)

    h = rmsnorm(x, norm_mix[0], BF16)
    proj = matmul(h, ev_w_in[0].astype(BF16))
    s5p = _s5_params(s5_a_re[0], s5_a_im[0], s5_log_dt[0], s5_b_re[0], s5_b_im[0], s5_c_re[0], s5_c_im[0])
    z_p, s5re_p, s5im_p = s5_prompt(proj, n_prompt, seq, s5p, s5_d[0])
    z_s, s5re_s, s5im_s = s5_sample(proj, n_prompt, state_s5_re[0], state_s5_im[0], s5p, s5_d[0])
    z = jnp.concatenate([z_p, z_s], axis=0)
    ya = matmul(z.astype(BF16), s5_w_glu[0].astype(BF16), mode="glu", extras=(z,), out_dtype=BF16)
    yb, v_last = sgu(proj, n_prompt, seq, sgu_ln_g[0], sgu_ln_b[0], sgu_w[0], sgu_b[0])
    x = matmul(jnp.concatenate([ya, yb], axis=1), ev_w_out[0].astype(BF16), mode="resid", extras=(x,))
    x = _peer_ple(x, p16[0], norm_ffn[0], norm_ple[0], peer_w_q[0], peer_keys[0], peer_u[0], peer_v[0],
                  ple_proj[0], ple_gate[0])

    nh = GDN_HEADS
    n_ch = gdn_conv_w.shape[-1]
    n_qkvz = n_ch + nh * LANE
    h = rmsnorm(x, norm_mix[1], BF16)
    w_in = gdn_w_in[0]
    proj = matmul(h, w_in[:, :n_qkvz].astype(BF16))
    w_ab = jnp.pad(w_in[:, n_qkvz:], ((0, 0), (0, LANE - 2 * nh))).astype(BF16)
    ab = matmul(h, w_ab)
    tm_g = 512
    brep_p, gcrep_p = gdn_gates(ab[:n_prompt], gdn_a_log[0], gdn_dt_bias[0], _chunk_tril(tm_g, GDN_CHUNK))
    brep_s, grep_s = gdn_gates(ab[n_prompt:], gdn_a_log[0], gdn_dt_bias[0], jnp.eye(ns, dtype=F32))
    qkv_p = gdn_conv_prompt(proj, n_prompt, seq, gdn_conv_w[0], n_ch)
    buf = state_gdn_conv[0]
    qkv_s = gdn_conv_sample(proj, n_prompt, jnp.swapaxes(buf, 0, 1), gdn_conv_w[0], n_ch)
    og_p, gdn_p = gdn_core_prompt(qkv_p, brep_p, gcrep_p, proj, n_prompt, seq, gdn_norm_g[0])
    og_s, gdn_s = gdn_core_sample(qkv_s, brep_s, grep_s, proj, n_prompt, state_gdn[0], gdn_norm_g[0])
    x = matmul(jnp.concatenate([og_p, og_s], axis=0), gdn_w_out[0].astype(BF16), mode="resid", extras=(x,))
    x = _peer_ple(x, p16[1], norm_ffn[1], norm_ple[1], peer_w_q[1], peer_keys[1], peer_u[1], peer_v[1],
                  ple_proj[1], ple_gate[1])

    y = rmsnorm(x, norm_final, F32)

    conv_p = jnp.stack([proj[(b + 1) * seq - (GDN_CONV - 1):(b + 1) * seq, :n_ch] for b in range(bsz)])
    conv_s = jnp.concatenate([buf[:, 1:], proj[n_prompt:, None, :n_ch]], axis=1)
    n_v = bsz * SGU_CHUNK
    return (
        y[:n_prompt].reshape(bsz, seq, d),
        y[n_prompt:].reshape(ns, 1, d),
        s5re_p[None], s5im_p[None], s5re_s[None], s5im_s[None],
        v_last[:n_v].reshape(1, bsz, SGU_CHUNK, -1),
        v_last[n_v:].reshape(1, ns, 1, -1),
        gdn_p[None], gdn_s[None],
        conv_p[None], conv_s[None],
    )
```

```python
import functools

import jax
import jax.numpy as jnp
from jax import lax
from jax.experimental import pallas as pl
from jax.experimental.pallas import tpu as pltpu

F32 = jnp.float32
BF16 = jnp.bfloat16
EPS = 1e-6
LANE = 128
V7X_VMEM_BYTES = 64 * 1024 * 1024
VMEM_LIMIT = V7X_VMEM_BYTES - 8 * 1024 * 1024

S5_GROUP = 16
S5_STATE = 64
SGU_GROUPS = 8
SGU_CHUNK = 128
GDN_HEADS = 32
GDN_DK = 128
GDN_CONV = 4
GDN_CHUNK = 64
PEER_HEADS = 8
PEER_NKEYS = 128
PEER_TOPK = 16
PEER_BLOCK_KEYS = 4
GATE_DTYPE = jnp.float32

TOK_TILE = 640
ROW_TILE = 128


def _params(sem, vmem=VMEM_LIMIT):
    return pltpu.CompilerParams(dimension_semantics=sem, vmem_limit_bytes=vmem)


def _dot(a, b):
    return jnp.dot(a, b, preferred_element_type=F32)


def _dot_nt(a, b):
    return lax.dot_general(a, b, (((1,), (1,)), ((), ())), preferred_element_type=F32)


def _dot_tn(a, b):
    return lax.dot_general(a, b, (((0,), (0,)), ((), ())), preferred_element_type=F32)


def _rms(x, g):
    return x * lax.rsqrt(jnp.mean(x * x, axis=-1, keepdims=True) + EPS) * g


def _silu(x):
    return x * jax.nn.sigmoid(x)


def _rmsnorm_kernel(x_ref, g_ref, o_ref, *, transpose):
    y = _rms(x_ref[...], g_ref[...])
    if transpose:
        y = y.T
    o_ref[...] = y.astype(o_ref.dtype)


def rmsnorm(x, g, out_dtype, transpose=False):
    t, d = x.shape
    tm = ROW_TILE
    if transpose:
        out_shape, out_spec = (d, t), pl.BlockSpec((d, tm), lambda i: (0, i))
    else:
        out_shape, out_spec = (t, d), pl.BlockSpec((tm, d), lambda i: (i, 0))
    return pl.pallas_call(
        functools.partial(_rmsnorm_kernel, transpose=transpose),
        grid=(t // tm,),
        in_specs=[pl.BlockSpec((tm, d), lambda i: (i, 0)), pl.BlockSpec((1, d), lambda i: (0, 0))],
        out_specs=out_spec,
        out_shape=jax.ShapeDtypeStruct(out_shape, out_dtype),
        compiler_params=_params(("parallel",)),
        name="rmsnorm_t" if transpose else "rmsnorm",
    )(x, g.reshape(1, d))


def _rmsnorm_split_kernel(x_ref, g_ref, head_ref, tail_ref, *, n_head_blocks):
    i = pl.program_id(0)
    y = _rms(x_ref[...], g_ref[...])

    @pl.when(i < n_head_blocks)
    def _():
        head_ref[...] = y

    @pl.when(i >= n_head_blocks)
    def _():
        tail_ref[...] = y


def rmsnorm_split(x, g, n_head):
    t, d = x.shape
    tm = ROW_TILE
    nhb = n_head // tm
    return pl.pallas_call(
        functools.partial(_rmsnorm_split_kernel, n_head_blocks=nhb),
        grid=(t // tm,),
        in_specs=[pl.BlockSpec((tm, d), lambda i: (i, 0)), pl.BlockSpec((1, d), lambda i: (0, 0))],
        out_specs=[pl.BlockSpec((tm, d), lambda i: (jnp.minimum(i, nhb - 1), 0)),
                   pl.BlockSpec((tm, d), lambda i: (jnp.maximum(i - nhb, 0), 0))],
        out_shape=[jax.ShapeDtypeStruct((n_head, d), F32), jax.ShapeDtypeStruct((t - n_head, d), F32)],
        compiler_params=_params(("arbitrary",)),
        name="rmsnorm_split",
    )(x, g.reshape(1, d))


def _add_norm_kernel(x_ref, o_ref, g_ref, x2_ref, h_ref):
    x2 = x_ref[...] + o_ref[...]
    x2_ref[...] = x2
    h_ref[...] = _rms(x2, g_ref[...]).astype(h_ref.dtype)


def add_norm(x, o, g):
    t, d = x.shape
    tm = ROW_TILE
    row = pl.BlockSpec((tm, d), lambda i: (i, 0))
    return pl.pallas_call(
        _add_norm_kernel,
        grid=(t // tm,),
        in_specs=[row, row, pl.BlockSpec((1, d), lambda i: (0, 0))],
        out_specs=[row, row],
        out_shape=[jax.ShapeDtypeStruct((t, d), F32), jax.ShapeDtypeStruct((t, d), BF16)],
        compiler_params=_params(("parallel",)),
        name="add_norm",
    )(x, o, g.reshape(1, d))


def _mm_kernel(*refs, mode):
    x_ref, w_ref = refs[0], refs[1]
    o_ref = refs[-1]
    if mode == "resid2":
        x2_ref = refs[3]
        k1 = x_ref.shape[1]
        acc = refs[2][...] + (_dot(x_ref[...], w_ref[:k1, :]) + _dot(x2_ref[...], w_ref[k1:, :]))
        o_ref[...] = acc.astype(o_ref.dtype)
        return
    acc = _dot(x_ref[...], w_ref[...])
    if mode == "resid":
        acc = refs[2][...] + acc
    elif mode == "glu":
        z = refs[2][...]
        acc = z * jax.nn.sigmoid(acc)
    elif mode == "ple":
        r_ref, p_ref, pw_ref = refs[2], refs[3], refs[4]
        acc = r_ref[...] + jax.nn.sigmoid(acc) * _dot(p_ref[...], pw_ref[...])
    o_ref[...] = acc.astype(o_ref.dtype)


def matmul(x, w, *, layer=0, n_out=None, mode="plain", extras=(), out_dtype=F32, tn=1024):
    m, k = x.shape
    n = w.shape[2] if n_out is None else n_out
    tm = TOK_TILE
    tn = min(tn, n)
    in_specs = [pl.BlockSpec((tm, k), lambda j, i: (i, 0)),
                pl.BlockSpec((None, w.shape[1], tn), lambda j, i: (layer, 0, j))]
    if mode in ("resid", "resid2", "glu", "ple"):
        in_specs.append(pl.BlockSpec((tm, tn), lambda j, i: (i, j)))
    if mode == "resid2":
        in_specs.append(pl.BlockSpec((tm, extras[1].shape[1]), lambda j, i: (i, 0)))
    if mode == "ple":
        kp = extras[1].shape[1]
        in_specs += [pl.BlockSpec((tm, kp), lambda j, i: (i, 0)),
                     pl.BlockSpec((None, kp, tn), lambda j, i: (layer, 0, j))]
    return pl.pallas_call(
        functools.partial(_mm_kernel, mode=mode),
        grid=(n // tn, m // tm),
        in_specs=in_specs,
        out_specs=pl.BlockSpec((tm, tn), lambda j, i: (i, j)),
        out_shape=jax.ShapeDtypeStruct((m, n), out_dtype),
        compiler_params=_params(("parallel", "parallel")),
        name="mm_" + mode,
    )(x, w, *extras)


def _s5_prompt_kernel(x0, x1, x2, x3, bcat_ref, ccat_ref, lre_ref, lim_ref, d_ref,
                      z_ref, hre_out, him_out, u_scr, bu_scr, h_scr, y_scr, st_re, st_im, *, c):
    tc = pl.program_id(1)

    @pl.when(tc == 0)
    def _():
        st_re[...] = jnp.zeros_like(st_re)
        st_im[...] = jnp.zeros_like(st_im)

    xs = (x0, x1, x2, x3)
    half_w = st_re.shape[1]
    zeros = jnp.zeros((c, LANE), F32)
    for b in range(4):
        for half in range(2):
            s = 2 * b + half
            u_scr[half, pl.ds(s, c, stride=8), :] = xs[b][:, half * LANE:(half + 1) * LANE]
            u_scr[1 - half, pl.ds(s, c, stride=8), :] = zeros
    bu_scr[...] = _dot(jnp.concatenate([u_scr[0], u_scr[1]], axis=1), bcat_ref[0])
    lr = lre_ref[0]
    li = lim_ref[0]
    hr = st_re[...]
    hi = st_im[...]
    for t in range(c):
        rows = slice(8 * t, 8 * t + 8)
        nr = lr * hr - li * hi + bu_scr[rows, :half_w]
        ni = lr * hi + li * hr + bu_scr[rows, half_w:]
        h_scr[rows, :half_w] = nr
        h_scr[rows, half_w:] = ni
        hr, hi = nr, ni
    st_re[...] = hr
    st_im[...] = hi
    hre_out[0] = hr
    him_out[0] = hi
    y = _dot(h_scr[...], ccat_ref[0])
    odd = lax.broadcasted_iota(jnp.int32, (8 * c, LANE), 0) % 2 == 1
    y_scr[...] = jnp.where(odd, y[:, LANE:], y[:, :LANE])
    for b in range(4):
        for half in range(2):
            s = 2 * b + half
            ls = slice(half * LANE, (half + 1) * LANE)
            ys = y_scr[pl.ds(s, c, stride=8), :]
            z_ref[b, :, ls] = jax.nn.gelu(ys + d_ref[:, ls] * xs[b][:, ls])


def _s5_sample_kernel(x_ref, h0re_ref, h0im_ref, bblk_ref, cblk_ref, lre_ref, lim_ref, d_ref,
                      z_ref, hre_out, him_out):
    x = x_ref[...]
    half_w = h0re_ref.shape[1]
    bu = _dot(x, bblk_ref[0])
    lr = lre_ref[0]
    li = lim_ref[0]
    h0r = h0re_ref[...]
    h0i = h0im_ref[...]
    nr = lr * h0r - li * h0i + bu[:, :half_w]
    ni = lr * h0i + li * h0r + bu[:, half_w:]
    hre_out[...] = nr
    him_out[...] = ni
    y = _dot(nr, cblk_ref[0, :half_w, :]) + _dot(ni, cblk_ref[0, half_w:, :]) + d_ref[...] * x
    z_ref[...] = jax.nn.gelu(y)


def _s5_params(a_re, a_im, log_dt, b_re, b_im, c_re, c_im):
    g, p = a_re.shape
    nb = g // 8
    lam = lax.complex(a_re, a_im)
    dt = jnp.exp(log_dt)[:, None]
    lam_bar = jnp.exp(lam * dt)
    b_bar = ((lam_bar - 1.0) / lam)[..., None] * lax.complex(b_re, b_im)
    eye = jnp.eye(8, dtype=F32)

    def bmat(v):
        v = jnp.swapaxes(v, 1, 2).reshape(nb, 8, S5_GROUP, p)
        return jnp.einsum("kjcp,jl->kjclp", v, eye).reshape(nb, 8 * S5_GROUP, 8 * p)

    def cmat(v):
        v = jnp.swapaxes(v, 1, 2).reshape(nb, 8, p, S5_GROUP)
        return jnp.einsum("kjpc,jl->kjplc", v, eye).reshape(nb, 8 * p, 8 * S5_GROUP)

    bblk = jnp.concatenate([bmat(jnp.real(b_bar)), bmat(jnp.imag(b_bar))], axis=2)
    cblk = jnp.concatenate([cmat(c_re), cmat(-c_im)], axis=1)
    lre = jnp.real(lam_bar).reshape(nb, 8 * p)
    lim = jnp.imag(lam_bar).reshape(nb, 8 * p)
    return bblk, cblk, lre, lim


def s5_prompt(proj, n_prompt, seq, params, d_skip, c=128):
    bblk, cblk, lre, lim = params
    nb, _, two_w = bblk.shape
    half_w = two_w // 2
    bsz = n_prompt // seq
    assert bsz == 4
    w = nb * 8 * S5_GROUP
    ngb = nb // 2
    nt = seq // c
    lre_t = jnp.tile(lre.reshape(ngb, 2, half_w), (1, bsz, 1))
    lim_t = jnp.tile(lim.reshape(ngb, 2, half_w), (1, bsz, 1))
    bcat = bblk.reshape(ngb, 2 * LANE, two_w)
    ccat = jnp.swapaxes(cblk.reshape(ngb, 2, two_w, LANE), 1, 2).reshape(ngb, two_w, 2 * LANE)
    x_specs = [pl.BlockSpec((c, 2 * LANE), functools.partial(lambda gb, tc, b: (b * nt + tc, gb), b=b))
               for b in range(bsz)]
    z, hre, him = pl.pallas_call(
        functools.partial(_s5_prompt_kernel, c=c),
        grid=(ngb, nt),
        in_specs=x_specs + [
            pl.BlockSpec((1, 2 * LANE, two_w), lambda gb, tc: (gb, 0, 0)),
            pl.BlockSpec((1, two_w, 2 * LANE), lambda gb, tc: (gb, 0, 0)),
            pl.BlockSpec((1, 8, half_w), lambda gb, tc: (gb, 0, 0)),
            pl.BlockSpec((1, 8, half_w), lambda gb, tc: (gb, 0, 0)),
            pl.BlockSpec((1, 2 * LANE), lambda gb, tc: (0, gb)),
        ],
        out_specs=[
            pl.BlockSpec((bsz, c, 2 * LANE), lambda gb, tc: (0, tc, gb)),
            pl.BlockSpec((1, 8, half_w), lambda gb, tc: (gb, 0, 0)),
            pl.BlockSpec((1, 8, half_w), lambda gb, tc: (gb, 0, 0)),
        ],
        out_shape=[
            jax.ShapeDtypeStruct((bsz, seq, w), F32),
            jax.ShapeDtypeStruct((ngb, 8, half_w), F32),
            jax.ShapeDtypeStruct((ngb, 8, half_w), F32),
        ],
        scratch_shapes=[pltpu.VMEM((2, 8 * c, LANE), F32), pltpu.VMEM((8 * c, two_w), F32),
                        pltpu.VMEM((8 * c, two_w), F32), pltpu.VMEM((8 * c, LANE), F32),
                        pltpu.VMEM((8, half_w), F32), pltpu.VMEM((8, half_w), F32)],
        compiler_params=_params(("parallel", "arbitrary")),
        name="s5_prompt",
    )(proj, proj, proj, proj, bcat, ccat, lre_t, lim_t, d_skip.reshape(1, w))

    def states(h):
        h = h.reshape(ngb, bsz, 2, 8, S5_STATE)
        return jnp.transpose(h, (1, 0, 2, 3, 4)).reshape(bsz, nb * 8, S5_STATE)

    return z.reshape(n_prompt, w), states(hre), states(him)


def s5_sample(proj, n_prompt, h0_re, h0_im, params, d_skip):
    bblk, cblk, lre, lim = params
    nb, _, two_w = bblk.shape
    half_w = two_w // 2
    ns = h0_re.shape[0]
    w = nb * 8 * S5_GROUP
    row_blk = n_prompt // ns
    z, hre, him = pl.pallas_call(
        _s5_sample_kernel,
        grid=(nb,),
        in_specs=[
            pl.BlockSpec((ns, LANE), lambda k: (row_blk, k)),
            pl.BlockSpec((ns, half_w), lambda k: (0, k)),
            pl.BlockSpec((ns, half_w), lambda k: (0, k)),
            pl.BlockSpec((1, LANE, two_w), lambda k: (k, 0, 0)),
            pl.BlockSpec((1, two_w, LANE), lambda k: (k, 0, 0)),
            pl.BlockSpec((1, 1, half_w), lambda k: (k, 0, 0)),
            pl.BlockSpec((1, 1, half_w), lambda k: (k, 0, 0)),
            pl.BlockSpec((1, LANE), lambda k: (0, k)),
        ],
        out_specs=[
            pl.BlockSpec((ns, LANE), lambda k: (0, k)),
            pl.BlockSpec((ns, half_w), lambda k: (0, k)),
            pl.BlockSpec((ns, half_w), lambda k: (0, k)),
        ],
        out_shape=[
            jax.ShapeDtypeStruct((ns, w), F32),
            jax.ShapeDtypeStruct((ns, nb * half_w), F32),
            jax.ShapeDtypeStruct((ns, nb * half_w), F32),
        ],
        compiler_params=_params(("parallel",)),
        name="s5_sample",
    )(proj, h0_re.reshape(ns, -1), h0_im.reshape(ns, -1), bblk, cblk,
      lre.reshape(nb, 1, half_w), lim.reshape(nb, 1, half_w), d_skip.reshape(1, w))
    return z, hre.reshape(ns, nb * 8, S5_STATE), him.reshape(ns, nb * 8, S5_STATE)


def _sgu_kernel(u_ref, v_ref, g_ref, b_ref, w_ref, bias_ref, y_ref, vl_ref):
    u = jax.nn.gelu(u_ref[...])
    v = jax.nn.gelu(v_ref[...])
    mu = jnp.mean(v, axis=-1, keepdims=True)
    vc = v - mu
    vn = vc * lax.rsqrt(jnp.mean(vc * vc, axis=-1, keepdims=True) + EPS) * g_ref[...] + b_ref[...]
    vl_ref[...] = vn
    c = u.shape[0]
    dg = u.shape[1] // SGU_GROUPS
    causal = lax.broadcasted_iota(jnp.int32, (c, c), 0) >= lax.broadcasted_iota(jnp.int32, (c, c), 1)
    for g in range(SGU_GROUPS):
        ls = slice(g * dg, (g + 1) * dg)
        w = jnp.where(causal, w_ref[0, g], 0.0).astype(BF16)
        mixed = _dot(w, vn[:, ls].astype(BF16)) + bias_ref[0, g]
        y_ref[:, ls] = (u[:, ls] * mixed).astype(y_ref.dtype)


def sgu(proj, n_prompt, seq, ln_g, ln_b, w_s, b_s):
    t = proj.shape[0]
    wdt = proj.shape[1] // 3
    c = SGU_CHUNK
    dg = wdt // SGU_GROUPS
    n_chunks = t // c
    per_seq = seq // c
    n_prompt_chunks = n_prompt // c
    eye = jnp.eye(c, dtype=F32)
    w_sets = jnp.stack([w_s, w_s[:, :1, :1] * eye])
    bias_sets = jnp.stack([jnp.broadcast_to(b_s[:, :, None], (SGU_GROUPS, c, dg)),
                           jnp.broadcast_to(b_s[:, :1, None], (SGU_GROUPS, c, dg))])
    n_last = n_prompt // seq + (n_chunks - n_prompt_chunks)
    y, vl = pl.pallas_call(
        _sgu_kernel,
        grid=(n_chunks,),
        in_specs=[
            pl.BlockSpec((c, wdt), lambda i: (i, 1)),
            pl.BlockSpec((c, wdt), lambda i: (i, 2)),
            pl.BlockSpec((1, wdt), lambda i: (0, 0)),
            pl.BlockSpec((1, wdt), lambda i: (0, 0)),
            pl.BlockSpec((1, SGU_GROUPS, c, c), lambda i: (i // n_prompt_chunks, 0, 0, 0)),
            pl.BlockSpec((1, SGU_GROUPS, c, dg), lambda i: (i // n_prompt_chunks, 0, 0, 0)),
        ],
        out_specs=[
            pl.BlockSpec((c, wdt), lambda i: (i, 0)),
            pl.BlockSpec((c, wdt), lambda i: (i // per_seq, 0)),
        ],
        out_shape=[jax.ShapeDtypeStruct((t, wdt), BF16), jax.ShapeDtypeStruct((n_last * c, wdt), F32)],
        compiler_params=_params(("arbitrary",)),
        name="sgu",
    )(proj, proj, ln_g.reshape(1, wdt), ln_b.reshape(1, wdt), w_sets, bias_sets)
    return y, vl


def _extract16(arrays):
    l = arrays[0].shape[1]
    viota = lax.broadcasted_iota(jnp.int32, (PEER_TOPK, l), 0)

    def body(r, carry):
        out = []
        for s, vals, idxs in carry:
            iota = lax.broadcasted_iota(jnp.int32, s.shape, 0)
            m = jnp.max(s, axis=0, keepdims=True)
            idx = jnp.min(jnp.where(s == m, iota, s.shape[0]), axis=0, keepdims=True)
            out.append((jnp.where(iota == idx, -jnp.inf, s), jnp.where(viota == r, m, vals),
                        jnp.where(viota == r, idx, idxs)))
        return tuple(out)

    init = tuple((s, jnp.zeros((PEER_TOPK, l), F32), jnp.zeros((PEER_TOPK, l), jnp.int32)) for s in arrays)
    return lax.fori_loop(0, PEER_TOPK, body, init)


def _ranks(idxs, n):
    iota = lax.broadcasted_iota(jnp.int32, (n, idxs.shape[1]), 0)
    rank = jnp.full(iota.shape, PEER_TOPK, jnp.int32)
    for r in range(PEER_TOPK):
        rank = jnp.where(iota == idxs[r:r + 1], r, rank)
    return rank


def _peer_candidates(t1, t2):
    k = PEER_TOPK
    sub = lax.broadcasted_iota(jnp.int32, (8, t1.shape[1]), 0)
    groups = [t1[0:1] + t2[0:8], t1[0:1] + t2[8:16], t1[1:2] + t2[0:8]]
    slices = [slice(0, 16), slice(16, 24)]
    for i in range(2, 8):
        groups.append(jnp.where(sub < k // (i + 1), t1[i:i + 1] + t2[0:8], -jnp.inf))
        slices.append(slice(8 * (i + 1), 8 * (i + 2)))
    groups.append(t1[8:16] + t2[0:1])
    slices += [slice(64 + i, 65 + i) for i in range(8, k)]
    return jnp.concatenate(groups, axis=0), slices


def _peer_topk_kernel(ht_ref, wq_ref, keys_ref, r2_ref, e2_ref, n_ref, e1_ref):
    qt = _dot_tn(wq_ref[...], ht_ref[...]).astype(BF16)
    nk = keys_ref.shape[2]
    dq = keys_ref.shape[3]
    s1_all = _dot(keys_ref[0, 0].astype(BF16), qt[:dq])
    s2_all = _dot(keys_ref[0, 1].astype(BF16), qt[dq:])
    for c in range(ht_ref.shape[1] // LANE):
        cs = slice(c * LANE, (c + 1) * LANE)
        s1 = s1_all[:, cs]
        s2 = s2_all[:, cs]
        (_, t1, idx1), (_, t2, idx2) = _extract16([s1, s2])
        rank1 = _ranks(idx1, nk)
        rank2 = _ranks(idx2, nk)
        cand, cand_rows = _peer_candidates(t1, t2)
        (cand_left, _, _), = _extract16([cand])
        selected = (cand_left == -jnp.inf) & (cand > -jnp.inf)
        cmax = t1[0:1] + t2[0:1]
        zsum = jnp.sum(jnp.where(selected, jnp.exp(cand - cmax), 0.0), axis=0, keepdims=True)
        n = jnp.zeros((nk, LANE), F32)
        for i in range(PEER_TOPK):
            cnt = jnp.sum(jnp.where(selected[cand_rows[i]], 1.0, 0.0), axis=0, keepdims=True)
            n = jnp.where(rank1 == i, cnt, n)
        r2_ref[0, :, cs] = rank2.astype(F32).astype(r2_ref.dtype)
        n_ref[0, :, cs] = n.astype(n_ref.dtype)
        e1_ref[0, :, cs] = jnp.exp(s1 - t1[0:1]).astype(e1_ref.dtype)
        e2_ref[0, :, cs] = (jnp.exp(s2 - t2[0:1]) / zsum).astype(e2_ref.dtype)


def peer_topk(h_t, wq, keys, layer):
    d, t = h_t.shape
    nh, _, nk, dq = keys.shape
    tt = TOK_TILE
    spec = pl.BlockSpec((1, nk, tt), lambda j, h: (h, 0, j))
    shp = jax.ShapeDtypeStruct((nh, nk, t), GATE_DTYPE)
    return pl.pallas_call(
        _peer_topk_kernel,
        grid=(t // tt, nh),
        in_specs=[
            pl.BlockSpec((d, tt), lambda j, h: (0, j)),
            pl.BlockSpec((None, d, 2 * dq), lambda j, h: (layer, 0, h)),
            pl.BlockSpec((1, 2, nk, dq), lambda j, h: (h, 0, 0, 0)),
        ],
        out_specs=[spec, spec, spec, spec],
        out_shape=[shp, shp, shp, shp],
        compiler_params=_params(("parallel", "arbitrary")),
        name="peer_topk",
    )(h_t, wq, keys)


def _peer_dense_kernel(ht_ref, u_ref, v_ref, r2_ref, e2_ref, n_ref, e1_ref, o_ref, s_a, s_b, act_scr,
                       *, na, nh, nb):
    i = pl.program_id(1)
    tt = ht_ref.shape[1]
    nk = r2_ref.shape[1]
    d = o_ref.shape[1]
    cb = 512
    tiles = [(a, c) for a in range(na) for c in range(tt // LANE)]
    n_parts = 2

    @pl.when(i == 0)
    def _():
        o_ref[...] = jnp.zeros_like(o_ref)
        s_b[...] = jnp.zeros_like(s_b)

    def step(s_prev, s_cur):
        rows = s_cur.shape[0] // n_parts
        for part in range(n_parts):
            ps = slice(part * rows, (part + 1) * rows)
            s_cur[ps, :] = _dot(u_ref[ps, :], ht_ref[...])
            for a, c in tiles[part * len(tiles) // n_parts:(part + 1) * len(tiles) // n_parts]:
                rs = slice(a * nk, (a + 1) * nk)
                cs = slice(c * LANE, (c + 1) * LANE)
                w = jnp.zeros((nk, LANE), GATE_DTYPE)
                for h in range(nh):
                    nrow = n_ref[h, 0, a:a + 1, cs]
                    e1row = e1_ref[h, 0, a:a + 1, cs]
                    w = w + jnp.where(r2_ref[h, :, cs] < nrow, e2_ref[h, :, cs] * e1row, jnp.zeros((), GATE_DTYPE))
                act_scr[cs, rs] = (jax.nn.gelu(s_prev[rs, cs]) * w.astype(F32)).T.astype(BF16)
        for r in range(d // cb):
            o_ref[:, r * cb:(r + 1) * cb] += _dot(act_scr[...], v_ref[:, r * cb:(r + 1) * cb])

    @pl.when(i % 2 == 0)
    def _():
        step(s_b, s_a)

    @pl.when(i % 2 == 1)
    def _():
        step(s_a, s_b)


def peer_dense(h_t, u, v, r2, e2, n, e1, layer):
    d, t = h_t.shape
    e = u.shape[1]
    nh, nk, _ = r2.shape
    tt = TOK_TILE
    na = PEER_BLOCK_KEYS
    ne = na * nk
    nb = e // ne
    n4 = n.reshape(nh, nk // na, na, t)
    e14 = e1.reshape(nh, nk // na, na, t)
    once = pl.Buffered(1)

    def score_blk(i):
        return jnp.minimum(i, nb - 1)

    def value_blk(i):
        return jnp.maximum(i - 1, 0)

    return pl.pallas_call(
        functools.partial(_peer_dense_kernel, na=na, nh=nh, nb=nb),
        grid=(t // tt, nb + 1),
        in_specs=[
            pl.BlockSpec((d, tt), lambda j, i: (0, j), pipeline_mode=once),
            pl.BlockSpec((None, ne, d), lambda j, i: (layer, score_blk(i), 0)),
            pl.BlockSpec((None, ne, d), lambda j, i: (layer, value_blk(i), 0)),
            pl.BlockSpec((nh, nk, tt), lambda j, i: (0, 0, j), pipeline_mode=once),
            pl.BlockSpec((nh, nk, tt), lambda j, i: (0, 0, j), pipeline_mode=once),
            pl.BlockSpec((nh, 1, na, tt), lambda j, i: (0, value_blk(i), 0, j)),
            pl.BlockSpec((nh, 1, na, tt), lambda j, i: (0, value_blk(i), 0, j)),
        ],
        out_specs=pl.BlockSpec((tt, d), lambda j, i: (j, 0)),
        out_shape=jax.ShapeDtypeStruct((t, d), F32),
        scratch_shapes=[pltpu.VMEM((ne, tt), F32), pltpu.VMEM((ne, tt), F32), pltpu.VMEM((tt, ne), BF16)],
        compiler_params=_params(("parallel", "arbitrary")),
        name="peer_dense",
    )(h_t, u, v, r2, e2, n4, e14)


def _softplus(x):
    return jnp.maximum(x, 0.0) + jnp.log1p(jnp.exp(-jnp.abs(x)))


def _replicate(x, onehot):
    hi = x.astype(BF16)
    rest = x - hi.astype(F32)
    mid = rest.astype(BF16)
    lo = (rest - mid.astype(F32)).astype(BF16)
    sel = onehot.astype(BF16)
    return (_dot(hi, sel) + _dot(mid, sel)) + _dot(lo, sel)


def _gdn_gates_kernel(ab_ref, alog_ref, dtb_ref, tril_ref, eg_ref, eb_ref, brep_ref, gcrep_ref):
    ab = ab_ref[...]
    g = -jnp.exp(alog_ref[...]) * _softplus(ab + dtb_ref[...])
    beta = jax.nn.sigmoid(ab)
    gc = jnp.dot(tril_ref[...], g, preferred_element_type=F32, precision=lax.Precision.HIGHEST)
    gcrep_ref[...] = _replicate(gc, eg_ref[...])
    brep_ref[...] = _replicate(beta, eb_ref[...])


def gdn_gates(ab, a_log, dt_bias, tril):
    rows = ab.shape[0]
    tm = tril.shape[0]
    nh = a_log.shape[0]
    wide = nh * LANE
    pad = LANE - nh
    alog_p = jnp.pad(a_log, (0, pad)).reshape(1, LANE)
    dtb_p = jnp.pad(dt_bias, (0, pad)).reshape(1, LANE)
    head_of_col = jnp.arange(wide) // LANE
    lane = jnp.arange(LANE)[:, None]
    e_g = (lane == head_of_col[None, :]).astype(F32)
    e_b = (lane == head_of_col[None, :] + nh).astype(F32)
    tn = 1024
    return pl.pallas_call(
        _gdn_gates_kernel,
        grid=(rows // tm, wide // tn),
        in_specs=[
            pl.BlockSpec((tm, LANE), lambda i, j: (i, 0)),
            pl.BlockSpec((1, LANE), lambda i, j: (0, 0)),
            pl.BlockSpec((1, LANE), lambda i, j: (0, 0)),
            pl.BlockSpec((tm, tm), lambda i, j: (0, 0)),
            pl.BlockSpec((LANE, tn), lambda i, j: (0, j)),
            pl.BlockSpec((LANE, tn), lambda i, j: (0, j)),
        ],
        out_specs=[pl.BlockSpec((tm, tn), lambda i, j: (i, j)), pl.BlockSpec((tm, tn), lambda i, j: (i, j))],
        out_shape=[jax.ShapeDtypeStruct((rows, wide), F32), jax.ShapeDtypeStruct((rows, wide), F32)],
        compiler_params=_params(("parallel", "parallel")),
        name="gdn_gates",
    )(ab, alog_p, dtb_p, tril, e_g, e_b)


def _gdn_post_conv(y, o_ref, cb, n_qk_blocks, n_q_blocks):
    y = _silu(y)
    is_qk = cb < n_qk_blocks
    qscale = jnp.where(cb < n_q_blocks, GDN_DK ** -0.5, 1.0)
    for hh in range(y.shape[1] // GDN_DK):
        ls = slice(hh * GDN_DK, (hh + 1) * GDN_DK)
        seg = y[:, ls]
        rs = lax.rsqrt(jnp.sum(seg * seg, axis=-1, keepdims=True) + EPS)
        o_ref[:, ls] = seg * jnp.where(is_qk, rs * qscale, 1.0)


def _gdn_conv_kernel(x_ref, w_ref, o_ref, ext, *, tm, n_qk_blocks, n_q_blocks):
    cb = pl.program_id(0)
    tt = pl.program_id(2)

    @pl.when(tt == 0)
    def _():
        ext[0:8, :] = jnp.zeros((8, ext.shape[1]), F32)

    ext[8:8 + tm, :] = x_ref[...]
    w = w_ref[...]
    y = w[0:1] * ext[5:5 + tm, :]
    for tap in range(1, GDN_CONV):
        y = y + w[tap:tap + 1] * ext[5 + tap:5 + tap + tm, :]
    ext[0:8, :] = ext[tm:tm + 8, :]
    _gdn_post_conv(y, o_ref, cb, n_qk_blocks, n_q_blocks)


def gdn_conv_prompt(proj, n_prompt, seq, conv_w, n_ch, tm=512, tc=1024):
    bsz = n_prompt // seq
    nt = seq // tm
    qk = (2 * n_ch) // 3
    return pl.pallas_call(
        functools.partial(_gdn_conv_kernel, tm=tm, n_qk_blocks=qk // tc, n_q_blocks=qk // 2 // tc),
        grid=(n_ch // tc, bsz, nt),
        in_specs=[pl.BlockSpec((tm, tc), lambda cb, b, tt: (b * nt + tt, cb)),
                  pl.BlockSpec((GDN_CONV, tc), lambda cb, b, tt: (0, cb))],
        out_specs=pl.BlockSpec((tm, tc), lambda cb, b, tt: (b * nt + tt, cb)),
        out_shape=jax.ShapeDtypeStruct((n_prompt, n_ch), F32),
        scratch_shapes=[pltpu.VMEM((tm + 8, tc), F32)],
        compiler_params=_params(("parallel", "parallel", "arbitrary")),
        name="gdn_conv_prompt",
    )(proj, conv_w)


def _gdn_conv_sample_kernel(x_ref, buf_ref, w_ref, o_ref, *, n_qk_blocks, n_q_blocks):
    cb = pl.program_id(0)
    w = w_ref[...]
    y = w[0:1] * buf_ref[0]
    for tap in range(1, GDN_CONV - 1):
        y = y + w[tap:tap + 1] * buf_ref[tap]
    y = y + w[GDN_CONV - 1:GDN_CONV] * x_ref[...]
    _gdn_post_conv(y, o_ref, cb, n_qk_blocks, n_q_blocks)


def gdn_conv_sample(proj, n_prompt, buf_t, conv_w, n_ch, tc=1024):
    ns = buf_t.shape[1]
    row_blk = n_prompt // ns
    qk = (2 * n_ch) // 3
    return pl.pallas_call(
        functools.partial(_gdn_conv_sample_kernel, n_qk_blocks=qk // tc, n_q_blocks=qk // 2 // tc),
        grid=(n_ch // tc,),
        in_specs=[pl.BlockSpec((ns, tc), lambda cb: (row_blk, cb)),
                  pl.BlockSpec((GDN_CONV - 1, ns, tc), lambda cb: (0, 0, cb)),
                  pl.BlockSpec((GDN_CONV, tc), lambda cb: (0, cb))],
        out_specs=pl.BlockSpec((ns, tc), lambda cb: (0, cb)),
        out_shape=jax.ShapeDtypeStruct((ns, n_ch), F32),
        compiler_params=_params(("parallel",)),
        name="gdn_conv_sample",
    )(proj, buf_t, conv_w)


def _unit_lower_inverses(mats):
    c = mats[0].shape[0]
    row = lax.broadcasted_iota(jnp.int32, (c, c), 0)
    col = lax.broadcasted_iota(jnp.int32, (c, c), 1)
    eye = jnp.where(row == col, 1.0, 0.0)
    blk = 16
    ds = [jnp.where(row // blk == col // blk, a, 0.0) for a in mats]
    d2 = [_dot(d, d) for d in ds]
    d4 = [_dot(d, d) for d in d2]
    d8 = [_dot(d, d) for d in d4]
    ts = [_dot(eye - d, eye + x) for d, x in zip(ds, d2)]
    ts = [_dot(t, eye + x) for t, x in zip(ts, d4)]
    ts = [_dot(t, eye + x) for t, x in zip(ts, d8)]
    while blk < c:
        off_mask = (row // (2 * blk) == col // (2 * blk)) & (row // blk != col // blk)
        tmp = [_dot(t, jnp.where(off_mask, a, 0.0)) for t, a in zip(ts, mats)]
        tmp = [_dot(x, t) for x, t in zip(tmp, ts)]
        ts = [t - x for t, x in zip(ts, tmp)]
        blk *= 2
    return ts


def _gdn_chunks(qs, ks, vs, brs, gcs, zs, ng, ss):
    c = qs[0].shape[0]
    dv = vs[0].shape[1]
    row = lax.broadcasted_iota(jnp.int32, (c, c), 0)
    col = lax.broadcasted_iota(jnp.int32, (c, c), 1)
    kb = [k * b for k, b in zip(ks, brs)]
    vb = [v * b for v, b in zip(vs, brs)]
    eg = [jnp.exp(g) for g in gcs]
    glast = [g[c - 1:c, :] for g in gcs]
    kbg = [x * e for x, e in zip(kb, eg)]
    qg = [q * e for q, e in zip(qs, eg)]
    kdec = [k * jnp.exp(gl - g) for k, gl, g in zip(ks, glast, gcs)]
    decay = [jnp.exp(jnp.where(row >= col, g[:, :c] - g.T[:c, :], -jnp.inf)) for g in gcs]
    k16 = [k.astype(BF16) for k in ks]
    a = [_dot_nt(x.astype(BF16), k) * jnp.where(row > col, d, 0.0) for x, k, d in zip(kb, k16, decay)]
    attn = [_dot_nt(q.astype(BF16), k) * d for q, k, d in zip(qs, k16, decay)]
    ts = _unit_lower_inverses(a)
    uw = [_dot(t, jnp.concatenate([x, y], axis=1)) for t, x, y in zip(ts, vb, kbg)]
    s16 = [s.astype(BF16) for s in ss]
    v_new = [x[:, :dv] - _dot(x[:, dv:].astype(BF16), s) for x, s in zip(uw, s16)]
    vn16 = [x.astype(BF16) for x in v_new]
    o_state = [_dot(x.astype(BF16), s) for x, s in zip(qg, s16)]
    o_local = [_dot(x.astype(BF16), v) for x, v in zip(attn, vn16)]
    s_new = [s * jnp.exp(gl) + _dot_tn(x.astype(BF16), v) for s, gl, x, v in zip(ss, glast, kdec, vn16)]
    og = [_rms(x + y, ng) * _silu(z) for x, y, z in zip(o_state, o_local, zs)]
    return og, s_new


def _gdn_core_kernel(q_ref, k_ref, v_ref, b_ref, gc_ref, z_ref, ng_ref, og_ref, sout_ref, s_scr, *, hb, nck, c):
    n = pl.program_id(2)

    @pl.when(n == 0)
    def _():
        s_scr[...] = jnp.zeros_like(s_scr)

    ng = ng_ref[...]
    ss = [s_scr[hh] for hh in range(hb)]
    for ck in range(nck):
        rs = slice(ck * c, (ck + 1) * c)

        def heads(ref):
            return [ref[rs, hh * LANE:(hh + 1) * LANE] for hh in range(hb)]

        og, ss = _gdn_chunks(heads(q_ref), heads(k_ref), heads(v_ref), heads(b_ref), heads(gc_ref), heads(z_ref),
                             ng, ss)
        for hh in range(hb):
            og_ref[rs, hh * LANE:(hh + 1) * LANE] = og[hh].astype(og_ref.dtype)
    for hh in range(hb):
        s_scr[hh] = ss[hh]
        sout_ref[0, hh] = ss[hh]


def gdn_core_prompt(qkv, brep, gcrep, proj, n_prompt, seq, norm_g, hb=GDN_HEADS, nck=1):
    c = GDN_CHUNK
    nh = GDN_HEADS
    bsz = n_prompt // seq
    tm = c * nck
    nt = seq // tm
    bw = hb * LANE
    nhb = nh // hb

    def rows(b, h, n):
        return b * nt + n

    og, s_out = pl.pallas_call(
        functools.partial(_gdn_core_kernel, hb=hb, nck=nck, c=c),
        grid=(bsz, nhb, nt),
        in_specs=[
            pl.BlockSpec((tm, bw), lambda b, h, n: (rows(b, h, n), h)),
            pl.BlockSpec((tm, bw), lambda b, h, n: (rows(b, h, n), nhb + h)),
            pl.BlockSpec((tm, bw), lambda b, h, n: (rows(b, h, n), 2 * nhb + h)),
            pl.BlockSpec((tm, bw), lambda b, h, n: (rows(b, h, n), h)),
            pl.BlockSpec((tm, bw), lambda b, h, n: (rows(b, h, n), h)),
            pl.BlockSpec((tm, bw), lambda b, h, n: (rows(b, h, n), 3 * nhb + h)),
            pl.BlockSpec((1, LANE), lambda b, h, n: (0, 0)),
        ],
        out_specs=[
            pl.BlockSpec((tm, bw), lambda b, h, n: (rows(b, h, n), h)),
            pl.BlockSpec((1, hb, GDN_DK, LANE), lambda b, h, n: (b, h, 0, 0)),
        ],
        out_shape=[jax.ShapeDtypeStruct((n_prompt, nh * LANE), BF16),
                   jax.ShapeDtypeStruct((bsz, nh, GDN_DK, LANE), F32)],
        scratch_shapes=[pltpu.VMEM((hb, GDN_DK, LANE), F32)],
        compiler_params=_params(("parallel", "parallel", "arbitrary")),
        name="gdn_core_prompt",
    )(qkv, qkv, qkv, brep, gcrep, proj, norm_g.reshape(1, LANE))
    return og, s_out


def _gdn_sample_kernel(q_ref, k_ref, v_ref, b_ref, g_ref, z_ref, ng_ref, s_ref, og_ref, sout_ref, o_scr):
    ns = q_ref.shape[0]
    qt = q_ref[...].T
    kt = k_ref[...].T
    lane = lax.broadcasted_iota(jnp.int32, qt.shape, 1)

    def body(i, carry):
        pick = lane == i
        qcol = jnp.sum(jnp.where(pick, qt, 0.0), axis=1, keepdims=True)
        kcol = jnp.sum(jnp.where(pick, kt, 0.0), axis=1, keepdims=True)
        v = v_ref[pl.ds(i, 1), :]
        beta = b_ref[pl.ds(i, 1), :]
        eg = jnp.exp(g_ref[pl.ds(i, 1), :])
        sd = s_ref[i, 0] * eg
        ks = jnp.sum(sd * kcol, axis=0, keepdims=True)
        v_new = beta * (v - ks)
        s_new = sd + kcol * v_new
        sout_ref[i, 0] = s_new
        o_scr[pl.ds(i, 1), :] = jnp.sum(s_new * qcol, axis=0, keepdims=True)
        return carry

    lax.fori_loop(0, ns, body, 0, unroll=8)
    og_ref[...] = (_rms(o_scr[...], ng_ref[...]) * _silu(z_ref[...])).astype(og_ref.dtype)


def gdn_core_sample(qkv_s, brep_s, grep_s, proj, n_prompt, state, norm_g):
    ns, nh = state.shape[0], state.shape[1]
    row_blk = n_prompt // ns
    blk = pl.BlockSpec((ns, LANE), lambda h: (0, h))
    st = pl.BlockSpec((ns, 1, GDN_DK, LANE), lambda h: (0, h, 0, 0))
    return pl.pallas_call(
        _gdn_sample_kernel,
        grid=(nh,),
        in_specs=[
            blk,
            pl.BlockSpec((ns, LANE), lambda h: (0, nh + h)),
            pl.BlockSpec((ns, LANE), lambda h: (0, 2 * nh + h)),
            blk,
            blk,
            pl.BlockSpec((ns, LANE), lambda h: (row_blk, 3 * nh + h)),
            pl.BlockSpec((1, LANE), lambda h: (0, 0)),
            st,
        ],
        out_specs=[blk, st],
        out_shape=[jax.ShapeDtypeStruct((ns, nh * LANE), BF16), jax.ShapeDtypeStruct(state.shape, F32)],
        scratch_shapes=[pltpu.VMEM((ns, LANE), F32)],
        compiler_params=_params(("parallel",)),
        name="gdn_core_sample",
    )(qkv_s, qkv_s, qkv_s, brep_s, grep_s, proj, norm_g.reshape(1, LANE), state)


def _peer_ple(x, p16, layer, norm_ffn, norm_ple, w_q16, keys, emb_u16, emb_v16, ple_proj16, ple_gate16):
    h_t = rmsnorm(x, norm_ffn[layer], BF16, transpose=True)
    r2, e2, n, e1 = peer_topk(h_t, w_q16, keys[layer], layer)
    o = peer_dense(h_t, emb_u16, emb_v16, r2, e2, n, e1, layer)
    x2, hp = add_norm(x, o, norm_ple[layer])
    return matmul(hp, ple_gate16, layer=layer, mode="ple", extras=(x2, p16[layer], ple_proj16))


def _chunk_tril(tm, c):
    r = jnp.arange(tm)
    return ((r[:, None] >= r[None, :]) & (r[:, None] // c == r[None, :] // c)).astype(F32)


def kernel(x_prompt, x_sample, p_prompt, p_sample, state_s5_re, state_s5_im, state_gdn, state_gdn_conv, norm_mix, norm_ffn, norm_ple, norm_final, ev_w_in, s5_a_re, s5_a_im, s5_log_dt, s5_b_re, s5_b_im, s5_c_re, s5_c_im, s5_d, s5_w_glu, sgu_ln_g, sgu_ln_b, sgu_w, sgu_b, ev_w_out, gdn_w_in, gdn_conv_w, gdn_a_log, gdn_dt_bias, gdn_norm_g, gdn_w_out, peer_w_q, peer_keys, peer_u, peer_v, ple_proj, ple_gate):
    bsz, seq, d = x_prompt.shape
    ns = x_sample.shape[0]
    n_prompt = bsz * seq
    x = jnp.concatenate([x_prompt.reshape(n_prompt, d), x_sample.reshape(ns, d)], axis=0)
    p16 = jnp.concatenate([p_prompt.reshape(2, n_prompt, -1), p_sample.reshape(2, ns, -1)], axis=1).astype(BF16)

    peer_w = (peer_w_q.astype(BF16), peer_keys, peer_u.astype(BF16), peer_v.astype(BF16),
              ple_proj.astype(BF16), ple_gate.astype(BF16))

    h = rmsnorm(x, norm_mix[0], BF16)
    proj = matmul(h, ev_w_in.astype(BF16))
    s5p = _s5_params(s5_a_re[0], s5_a_im[0], s5_log_dt[0], s5_b_re[0], s5_b_im[0], s5_c_re[0], s5_c_im[0])
    z_p, s5re_p, s5im_p = s5_prompt(proj, n_prompt, seq, s5p, s5_d[0])
    z_s, s5re_s, s5im_s = s5_sample(proj, n_prompt, state_s5_re[0], state_s5_im[0], s5p, s5_d[0])
    z = jnp.concatenate([z_p, z_s], axis=0)
    ya = matmul(z.astype(BF16), s5_w_glu.astype(BF16), mode="glu", extras=(z,), out_dtype=BF16)
    yb, v_last = sgu(proj, n_prompt, seq, sgu_ln_g[0], sgu_ln_b[0], sgu_w[0], sgu_b[0])
    x = matmul(ya, ev_w_out.astype(BF16), mode="resid2", extras=(x, yb))
    x = _peer_ple(x, p16, 0, norm_ffn, norm_ple, *peer_w)

    nh = GDN_HEADS
    n_ch = gdn_conv_w.shape[-1]
    n_qkvz = n_ch + nh * LANE
    h = rmsnorm(x, norm_mix[1], BF16)
    proj = matmul(h, gdn_w_in.astype(BF16), n_out=n_qkvz)
    w_ab = jnp.pad(gdn_w_in[:, :, n_qkvz:], ((0, 0), (0, 0), (0, LANE - 2 * nh))).astype(BF16)
    ab = matmul(h, w_ab)
    tm_g = 512
    brep_p, gcrep_p = gdn_gates(ab[:n_prompt], gdn_a_log[0], gdn_dt_bias[0], _chunk_tril(tm_g, GDN_CHUNK))
    brep_s, grep_s = gdn_gates(ab[n_prompt:], gdn_a_log[0], gdn_dt_bias[0], jnp.eye(ns, dtype=F32))
    qkv_p = gdn_conv_prompt(proj, n_prompt, seq, gdn_conv_w[0], n_ch)
    buf = state_gdn_conv[0]
    qkv_s = gdn_conv_sample(proj, n_prompt, jnp.swapaxes(buf, 0, 1), gdn_conv_w[0], n_ch)
    og_p, gdn_p = gdn_core_prompt(qkv_p, brep_p, gcrep_p, proj, n_prompt, seq, gdn_norm_g[0])
    og_s, gdn_s = gdn_core_sample(qkv_s, brep_s, grep_s, proj, n_prompt, state_gdn[0], gdn_norm_g[0])
    x = matmul(jnp.concatenate([og_p, og_s], axis=0), gdn_w_out.astype(BF16), mode="resid", extras=(x,))
    x = _peer_ple(x, p16, 1, norm_ffn, norm_ple, *peer_w)

    y_p, y_s = rmsnorm_split(x, norm_final, n_prompt)

    conv_p = jnp.stack([proj[(b + 1) * seq - (GDN_CONV - 1):(b + 1) * seq, :n_ch] for b in range(bsz)])
    conv_s = jnp.concatenate([buf[:, 1:], proj[n_prompt:, None, :n_ch]], axis=1)
    n_v = bsz * SGU_CHUNK
    return (
        y_p.reshape(bsz, seq, d),
        y_s.reshape(ns, 1, d),
        s5re_p[None], s5im_p[None], s5re_s[None], s5im_s[None],
        v_last[:n_v].reshape(1, bsz, SGU_CHUNK, -1),
        v_last[n_v:].reshape(1, ns, 1, -1),
        gdn_p[None], gdn_s[None],
        conv_p[None], conv_s[None],
    )
```

```python
import functools

import jax
import jax.numpy as jnp
from jax import lax
from jax.experimental import pallas as pl
from jax.experimental.pallas import tpu as pltpu

F32 = jnp.float32
BF16 = jnp.bfloat16
EPS = 1e-6
LANE = 128
V7X_VMEM_BYTES = 64 * 1024 * 1024
VMEM_LIMIT = V7X_VMEM_BYTES - 8 * 1024 * 1024

S5_GROUP = 16
S5_STATE = 64
SGU_GROUPS = 8
SGU_CHUNK = 128
GDN_HEADS = 32
GDN_DK = 128
GDN_CONV = 4
GDN_CHUNK = 64
PEER_HEADS = 8
PEER_NKEYS = 128
PEER_TOPK = 16
PEER_BLOCK_KEYS = 4
GATE_DTYPE = jnp.float32

TOK_TILE = 640
ROW_TILE = 128


def _params(sem, vmem=VMEM_LIMIT):
    return pltpu.CompilerParams(dimension_semantics=sem, vmem_limit_bytes=vmem)


def _dot(a, b):
    return jnp.dot(a, b, preferred_element_type=F32)


def _dot_nt(a, b):
    return lax.dot_general(a, b, (((1,), (1,)), ((), ())), preferred_element_type=F32)


def _dot_tn(a, b):
    return lax.dot_general(a, b, (((0,), (0,)), ((), ())), preferred_element_type=F32)


def _rms(x, g):
    return x * lax.rsqrt(jnp.mean(x * x, axis=-1, keepdims=True) + EPS) * g


def _silu(x):
    return x * jax.nn.sigmoid(x)


def _rmsnorm_kernel(x_ref, g_ref, o_ref, *, transpose):
    y = _rms(x_ref[...], g_ref[...])
    if transpose:
        y = y.T
    o_ref[...] = y.astype(o_ref.dtype)


def rmsnorm(x, g, out_dtype, transpose=False):
    t, d = x.shape
    tm = ROW_TILE
    if transpose:
        out_shape, out_spec = (d, t), pl.BlockSpec((d, tm), lambda i: (0, i))
    else:
        out_shape, out_spec = (t, d), pl.BlockSpec((tm, d), lambda i: (i, 0))
    return pl.pallas_call(
        functools.partial(_rmsnorm_kernel, transpose=transpose),
        grid=(t // tm,),
        in_specs=[pl.BlockSpec((tm, d), lambda i: (i, 0)), pl.BlockSpec((1, d), lambda i: (0, 0))],
        out_specs=out_spec,
        out_shape=jax.ShapeDtypeStruct(out_shape, out_dtype),
        compiler_params=_params(("parallel",)),
        name="rmsnorm_t" if transpose else "rmsnorm",
    )(x, g.reshape(1, d))


def _rmsnorm_split_kernel(x_ref, g_ref, head_ref, tail_ref, *, n_head_blocks):
    i = pl.program_id(0)
    y = _rms(x_ref[...], g_ref[...])

    @pl.when(i < n_head_blocks)
    def _():
        head_ref[...] = y

    @pl.when(i >= n_head_blocks)
    def _():
        tail_ref[...] = y


def rmsnorm_split(x, g, n_head):
    t, d = x.shape
    tm = ROW_TILE
    nhb = n_head // tm
    return pl.pallas_call(
        functools.partial(_rmsnorm_split_kernel, n_head_blocks=nhb),
        grid=(t // tm,),
        in_specs=[pl.BlockSpec((tm, d), lambda i: (i, 0)), pl.BlockSpec((1, d), lambda i: (0, 0))],
        out_specs=[pl.BlockSpec((tm, d), lambda i: (jnp.minimum(i, nhb - 1), 0)),
                   pl.BlockSpec((tm, d), lambda i: (jnp.maximum(i - nhb, 0), 0))],
        out_shape=[jax.ShapeDtypeStruct((n_head, d), F32), jax.ShapeDtypeStruct((t - n_head, d), F32)],
        compiler_params=_params(("arbitrary",)),
        name="rmsnorm_split",
    )(x, g.reshape(1, d))


def _add_norm_kernel(x_ref, o_ref, g_ref, x2_ref, h_ref):
    x2 = x_ref[...] + o_ref[...]
    x2_ref[...] = x2
    h_ref[...] = _rms(x2, g_ref[...]).astype(h_ref.dtype)


def add_norm(x, o, g):
    t, d = x.shape
    tm = ROW_TILE
    row = pl.BlockSpec((tm, d), lambda i: (i, 0))
    return pl.pallas_call(
        _add_norm_kernel,
        grid=(t // tm,),
        in_specs=[row, row, pl.BlockSpec((1, d), lambda i: (0, 0))],
        out_specs=[row, row],
        out_shape=[jax.ShapeDtypeStruct((t, d), F32), jax.ShapeDtypeStruct((t, d), BF16)],
        compiler_params=_params(("parallel",)),
        name="add_norm",
    )(x, o, g.reshape(1, d))


def _mm_kernel(*refs, mode):
    x_ref, w_ref = refs[0], refs[1]
    o_ref = refs[-1]
    if mode == "resid2":
        x2_ref = refs[3]
        k1 = x_ref.shape[1]
        acc = refs[2][...] + (_dot(x_ref[...], w_ref[:k1, :]) + _dot(x2_ref[...], w_ref[k1:, :]))
        o_ref[...] = acc.astype(o_ref.dtype)
        return
    acc = _dot(x_ref[...], w_ref[...])
    if mode == "resid":
        acc = refs[2][...] + acc
    elif mode == "glu":
        z = refs[2][...]
        acc = z * jax.nn.sigmoid(acc)
    elif mode == "ple":
        r_ref, p_ref, pw_ref = refs[2], refs[3], refs[4]
        acc = r_ref[...] + jax.nn.sigmoid(acc) * _dot(p_ref[...], pw_ref[...])
    o_ref[...] = acc.astype(o_ref.dtype)


def matmul(x, w, *, layer=0, n_out=None, mode="plain", extras=(), out_dtype=F32, tn=1024):
    m, k = x.shape
    n = w.shape[2] if n_out is None else n_out
    tm = TOK_TILE
    tn = min(tn, n)
    in_specs = [pl.BlockSpec((tm, k), lambda j, i: (i, 0)),
                pl.BlockSpec((None, w.shape[1], tn), lambda j, i: (layer, 0, j))]
    if mode in ("resid", "resid2", "glu", "ple"):
        in_specs.append(pl.BlockSpec((tm, tn), lambda j, i: (i, j)))
    if mode == "resid2":
        in_specs.append(pl.BlockSpec((tm, extras[1].shape[1]), lambda j, i: (i, 0)))
    if mode == "ple":
        kp = extras[1].shape[1]
        in_specs += [pl.BlockSpec((tm, kp), lambda j, i: (i, 0)),
                     pl.BlockSpec((None, kp, tn), lambda j, i: (layer, 0, j))]
    return pl.pallas_call(
        functools.partial(_mm_kernel, mode=mode),
        grid=(n // tn, m // tm),
        in_specs=in_specs,
        out_specs=pl.BlockSpec((tm, tn), lambda j, i: (i, j)),
        out_shape=jax.ShapeDtypeStruct((m, n), out_dtype),
        compiler_params=_params(("parallel", "parallel")),
        name="mm_" + mode,
    )(x, w, *extras)


def _s5_prompt_kernel(x0, x1, x2, x3, bcat_ref, ccat_ref, lre_ref, lim_ref, d_ref,
                      z_ref, hre_out, him_out, u_scr, bu_scr, h_scr, y_scr, st_re, st_im, *, c):
    tc = pl.program_id(1)

    @pl.when(tc == 0)
    def _():
        st_re[...] = jnp.zeros_like(st_re)
        st_im[...] = jnp.zeros_like(st_im)

    xs = (x0, x1, x2, x3)
    half_w = st_re.shape[1]
    zeros = jnp.zeros((c, LANE), F32)
    for b in range(4):
        for half in range(2):
            s = 2 * b + half
            u_scr[half, pl.ds(s, c, stride=8), :] = xs[b][:, half * LANE:(half + 1) * LANE]
            u_scr[1 - half, pl.ds(s, c, stride=8), :] = zeros
    bu_scr[...] = _dot(jnp.concatenate([u_scr[0], u_scr[1]], axis=1), bcat_ref[0])
    lr = lre_ref[0]
    li = lim_ref[0]
    hr = st_re[...]
    hi = st_im[...]
    for t in range(c):
        rows = slice(8 * t, 8 * t + 8)
        nr = lr * hr - li * hi + bu_scr[rows, :half_w]
        ni = lr * hi + li * hr + bu_scr[rows, half_w:]
        h_scr[rows, :half_w] = nr
        h_scr[rows, half_w:] = ni
        hr, hi = nr, ni
    st_re[...] = hr
    st_im[...] = hi
    hre_out[0] = hr
    him_out[0] = hi
    y = _dot(h_scr[...], ccat_ref[0])
    odd = lax.broadcasted_iota(jnp.int32, (8 * c, LANE), 0) % 2 == 1
    y_scr[...] = jnp.where(odd, y[:, LANE:], y[:, :LANE])
    for b in range(4):
        for half in range(2):
            s = 2 * b + half
            ls = slice(half * LANE, (half + 1) * LANE)
            ys = y_scr[pl.ds(s, c, stride=8), :]
            z_ref[b, :, ls] = jax.nn.gelu(ys + d_ref[:, ls] * xs[b][:, ls])


def _s5_sample_kernel(x_ref, h0re_ref, h0im_ref, bblk_ref, cblk_ref, lre_ref, lim_ref, d_ref,
                      z_ref, hre_out, him_out):
    x = x_ref[...]
    half_w = h0re_ref.shape[1]
    bu = _dot(x, bblk_ref[0])
    lr = lre_ref[0]
    li = lim_ref[0]
    h0r = h0re_ref[...]
    h0i = h0im_ref[...]
    nr = lr * h0r - li * h0i + bu[:, :half_w]
    ni = lr * h0i + li * h0r + bu[:, half_w:]
    hre_out[...] = nr
    him_out[...] = ni
    y = _dot(nr, cblk_ref[0, :half_w, :]) + _dot(ni, cblk_ref[0, half_w:, :]) + d_ref[...] * x
    z_ref[...] = jax.nn.gelu(y)


def _s5_params(a_re, a_im, log_dt, b_re, b_im, c_re, c_im):
    g, p = a_re.shape
    nb = g // 8
    lam = lax.complex(a_re, a_im)
    dt = jnp.exp(log_dt)[:, None]
    lam_bar = jnp.exp(lam * dt)
    b_bar = ((lam_bar - 1.0) / lam)[..., None] * lax.complex(b_re, b_im)
    eye = jnp.eye(8, dtype=F32)

    def bmat(v):
        v = jnp.swapaxes(v, 1, 2).reshape(nb, 8, S5_GROUP, p)
        return jnp.einsum("kjcp,jl->kjclp", v, eye).reshape(nb, 8 * S5_GROUP, 8 * p)

    def cmat(v):
        v = jnp.swapaxes(v, 1, 2).reshape(nb, 8, p, S5_GROUP)
        return jnp.einsum("kjpc,jl->kjplc", v, eye).reshape(nb, 8 * p, 8 * S5_GROUP)

    bblk = jnp.concatenate([bmat(jnp.real(b_bar)), bmat(jnp.imag(b_bar))], axis=2)
    cblk = jnp.concatenate([cmat(c_re), cmat(-c_im)], axis=1)
    lre = jnp.real(lam_bar).reshape(nb, 8 * p)
    lim = jnp.imag(lam_bar).reshape(nb, 8 * p)
    return bblk, cblk, lre, lim


def s5_prompt(proj, n_prompt, seq, params, d_skip, c=128):
    bblk, cblk, lre, lim = params
    nb, _, two_w = bblk.shape
    half_w = two_w // 2
    bsz = n_prompt // seq
    assert bsz == 4
    w = nb * 8 * S5_GROUP
    ngb = nb // 2
    nt = seq // c
    lre_t = jnp.tile(lre.reshape(ngb, 2, half_w), (1, bsz, 1))
    lim_t = jnp.tile(lim.reshape(ngb, 2, half_w), (1, bsz, 1))
    bcat = bblk.reshape(ngb, 2 * LANE, two_w)
    ccat = jnp.swapaxes(cblk.reshape(ngb, 2, two_w, LANE), 1, 2).reshape(ngb, two_w, 2 * LANE)
    x_specs = [pl.BlockSpec((c, 2 * LANE), functools.partial(lambda gb, tc, b: (b * nt + tc, gb), b=b))
               for b in range(bsz)]
    z, hre, him = pl.pallas_call(
        functools.partial(_s5_prompt_kernel, c=c),
        grid=(ngb, nt),
        in_specs=x_specs + [
            pl.BlockSpec((1, 2 * LANE, two_w), lambda gb, tc: (gb, 0, 0)),
            pl.BlockSpec((1, two_w, 2 * LANE), lambda gb, tc: (gb, 0, 0)),
            pl.BlockSpec((1, 8, half_w), lambda gb, tc: (gb, 0, 0)),
            pl.BlockSpec((1, 8, half_w), lambda gb, tc: (gb, 0, 0)),
            pl.BlockSpec((1, 2 * LANE), lambda gb, tc: (0, gb)),
        ],
        out_specs=[
            pl.BlockSpec((bsz, c, 2 * LANE), lambda gb, tc: (0, tc, gb)),
            pl.BlockSpec((1, 8, half_w), lambda gb, tc: (gb, 0, 0)),
            pl.BlockSpec((1, 8, half_w), lambda gb, tc: (gb, 0, 0)),
        ],
        out_shape=[
            jax.ShapeDtypeStruct((bsz, seq, w), F32),
            jax.ShapeDtypeStruct((ngb, 8, half_w), F32),
            jax.ShapeDtypeStruct((ngb, 8, half_w), F32),
        ],
        scratch_shapes=[pltpu.VMEM((2, 8 * c, LANE), F32), pltpu.VMEM((8 * c, two_w), F32),
                        pltpu.VMEM((8 * c, two_w), F32), pltpu.VMEM((8 * c, LANE), F32),
                        pltpu.VMEM((8, half_w), F32), pltpu.VMEM((8, half_w), F32)],
        compiler_params=_params(("parallel", "arbitrary")),
        name="s5_prompt",
    )(proj, proj, proj, proj, bcat, ccat, lre_t, lim_t, d_skip.reshape(1, w))

    def states(h):
        h = h.reshape(ngb, bsz, 2, 8, S5_STATE)
        return jnp.transpose(h, (1, 0, 2, 3, 4)).reshape(bsz, nb * 8, S5_STATE)

    return z.reshape(n_prompt, w), states(hre), states(him)


def s5_sample(proj, n_prompt, h0_re, h0_im, params, d_skip):
    bblk, cblk, lre, lim = params
    nb, _, two_w = bblk.shape
    half_w = two_w // 2
    ns = h0_re.shape[0]
    w = nb * 8 * S5_GROUP
    row_blk = n_prompt // ns
    z, hre, him = pl.pallas_call(
        _s5_sample_kernel,
        grid=(nb,),
        in_specs=[
            pl.BlockSpec((ns, LANE), lambda k: (row_blk, k)),
            pl.BlockSpec((ns, half_w), lambda k: (0, k)),
            pl.BlockSpec((ns, half_w), lambda k: (0, k)),
            pl.BlockSpec((1, LANE, two_w), lambda k: (k, 0, 0)),
            pl.BlockSpec((1, two_w, LANE), lambda k: (k, 0, 0)),
            pl.BlockSpec((1, 1, half_w), lambda k: (k, 0, 0)),
            pl.BlockSpec((1, 1, half_w), lambda k: (k, 0, 0)),
            pl.BlockSpec((1, LANE), lambda k: (0, k)),
        ],
        out_specs=[
            pl.BlockSpec((ns, LANE), lambda k: (0, k)),
            pl.BlockSpec((ns, half_w), lambda k: (0, k)),
            pl.BlockSpec((ns, half_w), lambda k: (0, k)),
        ],
        out_shape=[
            jax.ShapeDtypeStruct((ns, w), F32),
            jax.ShapeDtypeStruct((ns, nb * half_w), F32),
            jax.ShapeDtypeStruct((ns, nb * half_w), F32),
        ],
        compiler_params=_params(("parallel",)),
        name="s5_sample",
    )(proj, h0_re.reshape(ns, -1), h0_im.reshape(ns, -1), bblk, cblk,
      lre.reshape(nb, 1, half_w), lim.reshape(nb, 1, half_w), d_skip.reshape(1, w))
    return z, hre.reshape(ns, nb * 8, S5_STATE), him.reshape(ns, nb * 8, S5_STATE)


def _sgu_kernel(u_ref, v_ref, g_ref, b_ref, w_ref, bias_ref, y_ref, vl_ref):
    u = jax.nn.gelu(u_ref[...])
    v = jax.nn.gelu(v_ref[...])
    mu = jnp.mean(v, axis=-1, keepdims=True)
    vc = v - mu
    vn = vc * lax.rsqrt(jnp.mean(vc * vc, axis=-1, keepdims=True) + EPS) * g_ref[...] + b_ref[...]
    vl_ref[...] = vn
    c = u.shape[0]
    dg = u.shape[1] // SGU_GROUPS
    causal = lax.broadcasted_iota(jnp.int32, (c, c), 0) >= lax.broadcasted_iota(jnp.int32, (c, c), 1)
    for g in range(SGU_GROUPS):
        ls = slice(g * dg, (g + 1) * dg)
        w = jnp.where(causal, w_ref[0, g], 0.0).astype(BF16)
        mixed = _dot(w, vn[:, ls].astype(BF16)) + bias_ref[0, g]
        y_ref[:, ls] = (u[:, ls] * mixed).astype(y_ref.dtype)


def sgu(proj, n_prompt, seq, ln_g, ln_b, w_s, b_s):
    t = proj.shape[0]
    wdt = proj.shape[1] // 3
    c = SGU_CHUNK
    dg = wdt // SGU_GROUPS
    n_chunks = t // c
    per_seq = seq // c
    n_prompt_chunks = n_prompt // c
    eye = jnp.eye(c, dtype=F32)
    w_sets = jnp.stack([w_s, w_s[:, :1, :1] * eye])
    bias_sets = jnp.stack([jnp.broadcast_to(b_s[:, :, None], (SGU_GROUPS, c, dg)),
                           jnp.broadcast_to(b_s[:, :1, None], (SGU_GROUPS, c, dg))])
    n_last = n_prompt // seq + (n_chunks - n_prompt_chunks)
    y, vl = pl.pallas_call(
        _sgu_kernel,
        grid=(n_chunks,),
        in_specs=[
            pl.BlockSpec((c, wdt), lambda i: (i, 1)),
            pl.BlockSpec((c, wdt), lambda i: (i, 2)),
            pl.BlockSpec((1, wdt), lambda i: (0, 0)),
            pl.BlockSpec((1, wdt), lambda i: (0, 0)),
            pl.BlockSpec((1, SGU_GROUPS, c, c), lambda i: (i // n_prompt_chunks, 0, 0, 0)),
            pl.BlockSpec((1, SGU_GROUPS, c, dg), lambda i: (i // n_prompt_chunks, 0, 0, 0)),
        ],
        out_specs=[
            pl.BlockSpec((c, wdt), lambda i: (i, 0)),
            pl.BlockSpec((c, wdt), lambda i: (i // per_seq, 0)),
        ],
        out_shape=[jax.ShapeDtypeStruct((t, wdt), BF16), jax.ShapeDtypeStruct((n_last * c, wdt), F32)],
        compiler_params=_params(("arbitrary",)),
        name="sgu",
    )(proj, proj, ln_g.reshape(1, wdt), ln_b.reshape(1, wdt), w_sets, bias_sets)
    return y, vl


def _extract16(arrays):
    l = arrays[0].shape[1]
    viota = lax.broadcasted_iota(jnp.int32, (PEER_TOPK, l), 0)

    def body(r, carry):
        out = []
        for s, vals, idxs in carry:
            iota = lax.broadcasted_iota(jnp.int32, s.shape, 0)
            m = jnp.max(s, axis=0, keepdims=True)
            idx = jnp.min(jnp.where(s == m, iota, s.shape[0]), axis=0, keepdims=True)
            out.append((jnp.where(iota == idx, -jnp.inf, s), jnp.where(viota == r, m, vals),
                        jnp.where(viota == r, idx, idxs)))
        return tuple(out)

    init = tuple((s, jnp.zeros((PEER_TOPK, l), F32), jnp.zeros((PEER_TOPK, l), jnp.int32)) for s in arrays)
    return lax.fori_loop(0, PEER_TOPK, body, init)


def _ranks(idxs, n):
    iota = lax.broadcasted_iota(jnp.int32, (n, idxs.shape[1]), 0)
    rank = jnp.full(iota.shape, PEER_TOPK, jnp.int32)
    for r in range(PEER_TOPK):
        rank = jnp.where(iota == idxs[r:r + 1], r, rank)
    return rank


def _peer_candidates(t1, t2):
    k = PEER_TOPK
    sub = lax.broadcasted_iota(jnp.int32, (8, t1.shape[1]), 0)
    groups = [t1[0:1] + t2[0:8], t1[0:1] + t2[8:16], t1[1:2] + t2[0:8]]
    slices = [slice(0, 16), slice(16, 24)]
    for i in range(2, 8):
        groups.append(jnp.where(sub < k // (i + 1), t1[i:i + 1] + t2[0:8], -jnp.inf))
        slices.append(slice(8 * (i + 1), 8 * (i + 2)))
    groups.append(t1[8:16] + t2[0:1])
    slices += [slice(64 + i, 65 + i) for i in range(8, k)]
    return jnp.concatenate(groups, axis=0), slices


def _peer_topk_kernel(ht_ref, wq_ref, keys_ref, r2_ref, e2_ref, n_ref, e1_ref):
    qt = _dot_tn(wq_ref[...], ht_ref[...]).astype(BF16)
    nk = keys_ref.shape[2]
    dq = keys_ref.shape[3]
    s1_all = _dot(keys_ref[0, 0].astype(BF16), qt[:dq])
    s2_all = _dot(keys_ref[0, 1].astype(BF16), qt[dq:])
    for c in range(ht_ref.shape[1] // LANE):
        cs = slice(c * LANE, (c + 1) * LANE)
        s1 = s1_all[:, cs]
        s2 = s2_all[:, cs]
        (_, t1, idx1), (_, t2, idx2) = _extract16([s1, s2])
        rank1 = _ranks(idx1, nk)
        rank2 = _ranks(idx2, nk)
        cand, cand_rows = _peer_candidates(t1, t2)
        (cand_left, _, _), = _extract16([cand])
        selected = (cand_left == -jnp.inf) & (cand > -jnp.inf)
        cmax = t1[0:1] + t2[0:1]
        zsum = jnp.sum(jnp.where(selected, jnp.exp(cand - cmax), 0.0), axis=0, keepdims=True)
        n = jnp.zeros((nk, LANE), F32)
        for i in range(PEER_TOPK):
            cnt = jnp.sum(jnp.where(selected[cand_rows[i]], 1.0, 0.0), axis=0, keepdims=True)
            n = jnp.where(rank1 == i, cnt, n)
        r2_ref[0, :, cs] = rank2.astype(F32).astype(r2_ref.dtype)
        e1 = jnp.exp(s1 - t1[0:1])
        kb = n_ref.shape[2]
        for a in range(nk):
            n_ref[0, a // kb, a % kb:a % kb + 1, cs] = n[a:a + 1]
            e1_ref[0, a // kb, a % kb:a % kb + 1, cs] = e1[a:a + 1]
        e2_ref[0, :, cs] = (jnp.exp(s2 - t2[0:1]) / zsum).astype(e2_ref.dtype)


def peer_topk(h_t, wq, keys, layer):
    d, t = h_t.shape
    nh, _, nk, dq = keys.shape
    tt = TOK_TILE
    kb = PEER_BLOCK_KEYS
    spec = pl.BlockSpec((1, nk, tt), lambda j, h: (h, 0, j))
    shp = jax.ShapeDtypeStruct((nh, nk, t), GATE_DTYPE)
    spec_a = pl.BlockSpec((1, nk // kb, kb, tt), lambda j, h: (h, 0, 0, j))
    shp_a = jax.ShapeDtypeStruct((nh, nk // kb, kb, t), GATE_DTYPE)
    return pl.pallas_call(
        _peer_topk_kernel,
        grid=(t // tt, nh),
        in_specs=[
            pl.BlockSpec((d, tt), lambda j, h: (0, j)),
            pl.BlockSpec((None, d, 2 * dq), lambda j, h: (layer, 0, h)),
            pl.BlockSpec((1, 2, nk, dq), lambda j, h: (h, 0, 0, 0)),
        ],
        out_specs=[spec, spec, spec_a, spec_a],
        out_shape=[shp, shp, shp_a, shp_a],
        compiler_params=_params(("parallel", "arbitrary")),
        name="peer_topk",
    )(h_t, wq, keys)


def _peer_dense_kernel(ht_ref, u_ref, v_ref, r2_ref, e2_ref, n_ref, e1_ref, o_ref, s_a, s_b, act_scr,
                       *, na, nh, nb):
    i = pl.program_id(1)
    tt = ht_ref.shape[1]
    nk = r2_ref.shape[1]
    d = o_ref.shape[1]
    cb = 512
    tiles = [(a, c) for a in range(na) for c in range(tt // LANE)]
    n_parts = 2

    @pl.when(i == 0)
    def _():
        o_ref[...] = jnp.zeros_like(o_ref)
        s_b[...] = jnp.zeros_like(s_b)

    def step(s_prev, s_cur):
        rows = s_cur.shape[0] // n_parts
        for part in range(n_parts):
            ps = slice(part * rows, (part + 1) * rows)
            s_cur[ps, :] = _dot(u_ref[ps, :], ht_ref[...])
            for a, c in tiles[part * len(tiles) // n_parts:(part + 1) * len(tiles) // n_parts]:
                rs = slice(a * nk, (a + 1) * nk)
                cs = slice(c * LANE, (c + 1) * LANE)
                w = jnp.zeros((nk, LANE), F32)
                for h in range(nh):
                    nrow = n_ref[h, 0, a:a + 1, cs]
                    e1row = e1_ref[h, 0, a:a + 1, cs]
                    w = w + jnp.where(r2_ref[h, :, cs] < nrow, e2_ref[h, :, cs] * e1row, 0.0)
                act_scr[cs, rs] = (jax.nn.gelu(s_prev[rs, cs]) * w).T.astype(BF16)
        for r in range(d // cb):
            o_ref[:, r * cb:(r + 1) * cb] += _dot(act_scr[...], v_ref[:, r * cb:(r + 1) * cb])

    @pl.when(i % 2 == 0)
    def _():
        step(s_b, s_a)

    @pl.when(i % 2 == 1)
    def _():
        step(s_a, s_b)


def peer_dense(h_t, u, v, r2, e2, n, e1, layer):
    d, t = h_t.shape
    e = u.shape[1]
    nh, nk, _ = r2.shape
    tt = TOK_TILE
    na = PEER_BLOCK_KEYS
    ne = na * nk
    nb = e // ne
    assert n.shape == (nh, nk // na, na, t) and e1.shape == n.shape
    once = pl.Buffered(1)

    def score_blk(i):
        return jnp.minimum(i, nb - 1)

    def value_blk(i):
        return jnp.maximum(i - 1, 0)

    key_spec = pl.BlockSpec((nh, 1, na, tt), lambda j, i: (0, value_blk(i), 0, j))
    return pl.pallas_call(
        functools.partial(_peer_dense_kernel, na=na, nh=nh, nb=nb),
        grid=(t // tt, nb + 1),
        in_specs=[
            pl.BlockSpec((d, tt), lambda j, i: (0, j), pipeline_mode=once),
            pl.BlockSpec((None, ne, d), lambda j, i: (layer, score_blk(i), 0)),
            pl.BlockSpec((None, ne, d), lambda j, i: (layer, value_blk(i), 0)),
            pl.BlockSpec((nh, nk, tt), lambda j, i: (0, 0, j), pipeline_mode=once),
            pl.BlockSpec((nh, nk, tt), lambda j, i: (0, 0, j), pipeline_mode=once),
            key_spec,
            key_spec,
        ],
        out_specs=pl.BlockSpec((tt, d), lambda j, i: (j, 0)),
        out_shape=jax.ShapeDtypeStruct((t, d), F32),
        scratch_shapes=[pltpu.VMEM((ne, tt), F32), pltpu.VMEM((ne, tt), F32), pltpu.VMEM((tt, ne), BF16)],
        compiler_params=_params(("parallel", "arbitrary")),
        name="peer_dense",
    )(h_t, u, v, r2, e2, n, e1)


def _softplus(x):
    return jnp.maximum(x, 0.0) + jnp.log1p(jnp.exp(-jnp.abs(x)))


def _replicate(x, onehot):
    hi = x.astype(BF16)
    rest = x - hi.astype(F32)
    mid = rest.astype(BF16)
    lo = (rest - mid.astype(F32)).astype(BF16)
    sel = onehot.astype(BF16)
    return (_dot(hi, sel) + _dot(mid, sel)) + _dot(lo, sel)


def _gdn_gates_kernel(ab_ref, alog_ref, dtb_ref, tril_ref, eg_ref, eb_ref, brep_ref, gcrep_ref):
    ab = ab_ref[...]
    g = -jnp.exp(alog_ref[...]) * _softplus(ab + dtb_ref[...])
    beta = jax.nn.sigmoid(ab)
    gc = jnp.dot(tril_ref[...], g, preferred_element_type=F32, precision=lax.Precision.HIGHEST)
    gcrep_ref[...] = _replicate(gc, eg_ref[...])
    brep_ref[...] = _replicate(beta, eb_ref[...])


def gdn_gates(ab, a_log, dt_bias, tril):
    rows = ab.shape[0]
    tm = tril.shape[0]
    nh = a_log.shape[0]
    wide = nh * LANE
    pad = LANE - nh
    alog_p = jnp.pad(a_log, (0, pad)).reshape(1, LANE)
    dtb_p = jnp.pad(dt_bias, (0, pad)).reshape(1, LANE)
    head_of_col = jnp.arange(wide) // LANE
    lane = jnp.arange(LANE)[:, None]
    e_g = (lane == head_of_col[None, :]).astype(F32)
    e_b = (lane == head_of_col[None, :] + nh).astype(F32)
    tn = 1024
    return pl.pallas_call(
        _gdn_gates_kernel,
        grid=(rows // tm, wide // tn),
        in_specs=[
            pl.BlockSpec((tm, LANE), lambda i, j: (i, 0)),
            pl.BlockSpec((1, LANE), lambda i, j: (0, 0)),
            pl.BlockSpec((1, LANE), lambda i, j: (0, 0)),
            pl.BlockSpec((tm, tm), lambda i, j: (0, 0)),
            pl.BlockSpec((LANE, tn), lambda i, j: (0, j)),
            pl.BlockSpec((LANE, tn), lambda i, j: (0, j)),
        ],
        out_specs=[pl.BlockSpec((tm, tn), lambda i, j: (i, j)), pl.BlockSpec((tm, tn), lambda i, j: (i, j))],
        out_shape=[jax.ShapeDtypeStruct((rows, wide), F32), jax.ShapeDtypeStruct((rows, wide), F32)],
        compiler_params=_params(("parallel", "parallel")),
        name="gdn_gates",
    )(ab, alog_p, dtb_p, tril, e_g, e_b)


def _gdn_post_conv(y, o_ref, cb, n_qk_blocks, n_q_blocks):
    y = _silu(y)
    is_qk = cb < n_qk_blocks
    qscale = jnp.where(cb < n_q_blocks, GDN_DK ** -0.5, 1.0)
    for hh in range(y.shape[1] // GDN_DK):
        ls = slice(hh * GDN_DK, (hh + 1) * GDN_DK)
        seg = y[:, ls]
        rs = lax.rsqrt(jnp.sum(seg * seg, axis=-1, keepdims=True) + EPS)
        o_ref[:, ls] = (seg * jnp.where(is_qk, rs * qscale, 1.0)).astype(o_ref.dtype)


def _gdn_conv_kernel(x_ref, w_ref, o_ref, ext, *, tm, n_qk_blocks, n_q_blocks):
    cb = pl.program_id(0)
    tt = pl.program_id(2)

    @pl.when(tt == 0)
    def _():
        ext[0:8, :] = jnp.zeros((8, ext.shape[1]), F32)

    ext[8:8 + tm, :] = x_ref[...].astype(F32)
    w = w_ref[...]
    y = w[0:1] * ext[5:5 + tm, :]
    for tap in range(1, GDN_CONV):
        y = y + w[tap:tap + 1] * ext[5 + tap:5 + tap + tm, :]
    ext[0:8, :] = ext[tm:tm + 8, :]
    _gdn_post_conv(y, o_ref, cb, n_qk_blocks, n_q_blocks)


def gdn_conv_prompt(proj, n_prompt, seq, conv_w, n_ch, tm=512, tc=1024):
    bsz = n_prompt // seq
    nt = seq // tm
    qk = (2 * n_ch) // 3
    return pl.pallas_call(
        functools.partial(_gdn_conv_kernel, tm=tm, n_qk_blocks=qk // tc, n_q_blocks=qk // 2 // tc),
        grid=(n_ch // tc, bsz, nt),
        in_specs=[pl.BlockSpec((tm, tc), lambda cb, b, tt: (b * nt + tt, cb)),
                  pl.BlockSpec((GDN_CONV, tc), lambda cb, b, tt: (0, cb))],
        out_specs=pl.BlockSpec((tm, tc), lambda cb, b, tt: (b * nt + tt, cb)),
        out_shape=jax.ShapeDtypeStruct((n_prompt, n_ch), BF16),
        scratch_shapes=[pltpu.VMEM((tm + 8, tc), F32)],
        compiler_params=_params(("parallel", "parallel", "arbitrary")),
        name="gdn_conv_prompt",
    )(proj, conv_w)


def _gdn_conv_sample_kernel(x_ref, buf_ref, w_ref, o_ref, *, n_qk_blocks, n_q_blocks):
    cb = pl.program_id(0)
    w = w_ref[...]
    y = w[0:1] * buf_ref[0]
    for tap in range(1, GDN_CONV - 1):
        y = y + w[tap:tap + 1] * buf_ref[tap]
    y = y + w[GDN_CONV - 1:GDN_CONV] * x_ref[...].astype(F32)
    _gdn_post_conv(y, o_ref, cb, n_qk_blocks, n_q_blocks)


def gdn_conv_sample(proj, n_prompt, buf_t, conv_w, n_ch, tc=1024):
    ns = buf_t.shape[1]
    row_blk = n_prompt // ns
    qk = (2 * n_ch) // 3
    return pl.pallas_call(
        functools.partial(_gdn_conv_sample_kernel, n_qk_blocks=qk // tc, n_q_blocks=qk // 2 // tc),
        grid=(n_ch // tc,),
        in_specs=[pl.BlockSpec((ns, tc), lambda cb: (row_blk, cb)),
                  pl.BlockSpec((GDN_CONV - 1, ns, tc), lambda cb: (0, 0, cb)),
                  pl.BlockSpec((GDN_CONV, tc), lambda cb: (0, cb))],
        out_specs=pl.BlockSpec((ns, tc), lambda cb: (0, cb)),
        out_shape=jax.ShapeDtypeStruct((ns, n_ch), F32),
        compiler_params=_params(("parallel",)),
        name="gdn_conv_sample",
    )(proj, buf_t, conv_w)


def _unit_lower_inverses(mats):
    c = mats[0].shape[0]
    row = lax.broadcasted_iota(jnp.int32, (c, c), 0)
    col = lax.broadcasted_iota(jnp.int32, (c, c), 1)
    eye = jnp.where(row == col, 1.0, 0.0)
    blk = 16
    ds = [jnp.where(row // blk == col // blk, a, 0.0) for a in mats]
    d2 = [_dot(d, d) for d in ds]
    d4 = [_dot(d, d) for d in d2]
    d8 = [_dot(d, d) for d in d4]
    ts = [_dot(eye - d, eye + x) for d, x in zip(ds, d2)]
    ts = [_dot(t, eye + x) for t, x in zip(ts, d4)]
    ts = [_dot(t, eye + x) for t, x in zip(ts, d8)]
    while blk < c:
        off_mask = (row // (2 * blk) == col // (2 * blk)) & (row // blk != col // blk)
        tmp = [_dot(t, jnp.where(off_mask, a, 0.0)) for t, a in zip(ts, mats)]
        tmp = [_dot(x, t) for x, t in zip(tmp, ts)]
        ts = [t - x for t, x in zip(ts, tmp)]
        blk *= 2
    return ts


def _gdn_chunks(qs, ks, vs, brs, gcs, zs, ng, ss):
    c = qs[0].shape[0]
    dv = vs[0].shape[1]
    row = lax.broadcasted_iota(jnp.int32, (c, c), 0)
    col = lax.broadcasted_iota(jnp.int32, (c, c), 1)
    kb = [k * b for k, b in zip(ks, brs)]
    vb = [v * b for v, b in zip(vs, brs)]
    eg = [jnp.exp(g) for g in gcs]
    glast = [g[c - 1:c, :] for g in gcs]
    kbg = [x * e for x, e in zip(kb, eg)]
    qg = [q * e for q, e in zip(qs, eg)]
    kdec = [k * jnp.exp(gl - g) for k, gl, g in zip(ks, glast, gcs)]
    decay = [jnp.exp(jnp.where(row >= col, g[:, :c] - g.T[:c, :], -jnp.inf)) for g in gcs]
    k16 = [k.astype(BF16) for k in ks]
    a = [_dot_nt(x.astype(BF16), k) * jnp.where(row > col, d, 0.0) for x, k, d in zip(kb, k16, decay)]
    attn = [_dot_nt(q.astype(BF16), k) * d for q, k, d in zip(qs, k16, decay)]
    ts = _unit_lower_inverses(a)
    uw = [_dot(t, jnp.concatenate([x, y], axis=1)) for t, x, y in zip(ts, vb, kbg)]
    s16 = [s.astype(BF16) for s in ss]
    v_new = [x[:, :dv] - _dot(x[:, dv:].astype(BF16), s) for x, s in zip(uw, s16)]
    vn16 = [x.astype(BF16) for x in v_new]
    o_state = [_dot(x.astype(BF16), s) for x, s in zip(qg, s16)]
    o_local = [_dot(x.astype(BF16), v) for x, v in zip(attn, vn16)]
    s_new = [s * jnp.exp(gl) + _dot_tn(x.astype(BF16), v) for s, gl, x, v in zip(ss, glast, kdec, vn16)]
    og = [_rms(x + y, ng) * _silu(z) for x, y, z in zip(o_state, o_local, zs)]
    return og, s_new


def _gdn_core_kernel(q_ref, k_ref, v_ref, b_ref, gc_ref, z_ref, ng_ref, og_ref, sout_ref, s_scr, *, hb, nck, c):
    n = pl.program_id(2)

    @pl.when(n == 0)
    def _():
        s_scr[...] = jnp.zeros_like(s_scr)

    ng = ng_ref[...]
    ss = [s_scr[hh] for hh in range(hb)]
    for ck in range(nck):
        rs = slice(ck * c, (ck + 1) * c)

        def heads(ref):
            return [ref[rs, hh * LANE:(hh + 1) * LANE].astype(F32) for hh in range(hb)]

        og, ss = _gdn_chunks(heads(q_ref), heads(k_ref), heads(v_ref), heads(b_ref), heads(gc_ref), heads(z_ref),
                             ng, ss)
        for hh in range(hb):
            og_ref[rs, hh * LANE:(hh + 1) * LANE] = og[hh].astype(og_ref.dtype)
    for hh in range(hb):
        s_scr[hh] = ss[hh]
        sout_ref[0, hh] = ss[hh]


def gdn_core_prompt(qkv, brep, gcrep, proj, n_prompt, seq, norm_g, hb=GDN_HEADS, nck=1):
    c = GDN_CHUNK
    nh = GDN_HEADS
    bsz = n_prompt // seq
    tm = c * nck
    nt = seq // tm
    bw = hb * LANE
    nhb = nh // hb

    def rows(b, h, n):
        return b * nt + n

    og, s_out = pl.pallas_call(
        functools.partial(_gdn_core_kernel, hb=hb, nck=nck, c=c),
        grid=(bsz, nhb, nt),
        in_specs=[
            pl.BlockSpec((tm, bw), lambda b, h, n: (rows(b, h, n), h)),
            pl.BlockSpec((tm, bw), lambda b, h, n: (rows(b, h, n), nhb + h)),
            pl.BlockSpec((tm, bw), lambda b, h, n: (rows(b, h, n), 2 * nhb + h)),
            pl.BlockSpec((tm, bw), lambda b, h, n: (rows(b, h, n), h)),
            pl.BlockSpec((tm, bw), lambda b, h, n: (rows(b, h, n), h)),
            pl.BlockSpec((tm, bw), lambda b, h, n: (rows(b, h, n), 3 * nhb + h)),
            pl.BlockSpec((1, LANE), lambda b, h, n: (0, 0)),
        ],
        out_specs=[
            pl.BlockSpec((tm, bw), lambda b, h, n: (rows(b, h, n), h)),
            pl.BlockSpec((1, hb, GDN_DK, LANE), lambda b, h, n: (b, h, 0, 0)),
        ],
        out_shape=[jax.ShapeDtypeStruct((n_prompt, nh * LANE), BF16),
                   jax.ShapeDtypeStruct((bsz, nh, GDN_DK, LANE), F32)],
        scratch_shapes=[pltpu.VMEM((hb, GDN_DK, LANE), F32)],
        compiler_params=_params(("parallel", "parallel", "arbitrary")),
        name="gdn_core_prompt",
    )(qkv, qkv, qkv, brep, gcrep, proj, norm_g.reshape(1, LANE))
    return og, s_out


def _gdn_sample_kernel(q_ref, k_ref, v_ref, b_ref, g_ref, z_ref, ng_ref, s_ref, og_ref, sout_ref, o_scr):
    ns = q_ref.shape[0]
    qt = q_ref[...].T
    kt = k_ref[...].T
    lane = lax.broadcasted_iota(jnp.int32, qt.shape, 1)

    def body(i, carry):
        pick = lane == i
        qcol = jnp.sum(jnp.where(pick, qt, 0.0), axis=1, keepdims=True)
        kcol = jnp.sum(jnp.where(pick, kt, 0.0), axis=1, keepdims=True)
        v = v_ref[pl.ds(i, 1), :]
        beta = b_ref[pl.ds(i, 1), :]
        eg = jnp.exp(g_ref[pl.ds(i, 1), :])
        sd = s_ref[i, 0] * eg
        ks = jnp.sum(sd * kcol, axis=0, keepdims=True)
        v_new = beta * (v - ks)
        s_new = sd + kcol * v_new
        sout_ref[i, 0] = s_new
        o_scr[pl.ds(i, 1), :] = jnp.sum(s_new * qcol, axis=0, keepdims=True)
        return carry

    lax.fori_loop(0, ns, body, 0, unroll=8)
    og_ref[...] = (_rms(o_scr[...], ng_ref[...]) * _silu(z_ref[...].astype(F32))).astype(og_ref.dtype)


def gdn_core_sample(qkv_s, brep_s, grep_s, proj, n_prompt, state, norm_g):
    ns, nh = state.shape[0], state.shape[1]
    row_blk = n_prompt // ns
    blk = pl.BlockSpec((ns, LANE), lambda h: (0, h))
    st = pl.BlockSpec((ns, 1, GDN_DK, LANE), lambda h: (0, h, 0, 0))
    return pl.pallas_call(
        _gdn_sample_kernel,
        grid=(nh,),
        in_specs=[
            blk,
            pl.BlockSpec((ns, LANE), lambda h: (0, nh + h)),
            pl.BlockSpec((ns, LANE), lambda h: (0, 2 * nh + h)),
            blk,
            blk,
            pl.BlockSpec((ns, LANE), lambda h: (row_blk, 3 * nh + h)),
            pl.BlockSpec((1, LANE), lambda h: (0, 0)),
            st,
        ],
        out_specs=[blk, st],
        out_shape=[jax.ShapeDtypeStruct((ns, nh * LANE), BF16), jax.ShapeDtypeStruct(state.shape, F32)],
        scratch_shapes=[pltpu.VMEM((ns, LANE), F32)],
        compiler_params=_params(("parallel",)),
        name="gdn_core_sample",
    )(qkv_s, qkv_s, qkv_s, brep_s, grep_s, proj, norm_g.reshape(1, LANE), state)


def _peer_ple(x, p16, layer, norm_ffn, norm_ple, w_q16, keys, emb_u16, emb_v16, ple_proj16, ple_gate16):
    h_t = rmsnorm(x, norm_ffn[layer], BF16, transpose=True)
    r2, e2, n, e1 = peer_topk(h_t, w_q16, keys[layer], layer)
    o = peer_dense(h_t, emb_u16, emb_v16, r2, e2, n, e1, layer)
    x2, hp = add_norm(x, o, norm_ple[layer])
    return matmul(hp, ple_gate16, layer=layer, mode="ple", extras=(x2, p16[layer], ple_proj16))


def _chunk_tril(tm, c):
    r = jnp.arange(tm)
    return ((r[:, None] >= r[None, :]) & (r[:, None] // c == r[None, :] // c)).astype(F32)


def kernel(x_prompt, x_sample, p_prompt, p_sample, state_s5_re, state_s5_im, state_gdn, state_gdn_conv, norm_mix, norm_ffn, norm_ple, norm_final, ev_w_in, s5_a_re, s5_a_im, s5_log_dt, s5_b_re, s5_b_im, s5_c_re, s5_c_im, s5_d, s5_w_glu, sgu_ln_g, sgu_ln_b, sgu_w, sgu_b, ev_w_out, gdn_w_in, gdn_conv_w, gdn_a_log, gdn_dt_bias, gdn_norm_g, gdn_w_out, peer_w_q, peer_keys, peer_u, peer_v, ple_proj, ple_gate):
    bsz, seq, d = x_prompt.shape
    ns = x_sample.shape[0]
    n_prompt = bsz * seq
    x = jnp.concatenate([x_prompt.reshape(n_prompt, d), x_sample.reshape(ns, d)], axis=0)
    p16 = jnp.concatenate([p_prompt.reshape(2, n_prompt, -1), p_sample.reshape(2, ns, -1)], axis=1).astype(BF16)

    peer_w = (peer_w_q.astype(BF16), peer_keys, peer_u.astype(BF16), peer_v.astype(BF16),
              ple_proj.astype(BF16), ple_gate.astype(BF16))

    h = rmsnorm(x, norm_mix[0], BF16)
    proj = matmul(h, ev_w_in.astype(BF16))
    s5p = _s5_params(s5_a_re[0], s5_a_im[0], s5_log_dt[0], s5_b_re[0], s5_b_im[0], s5_c_re[0], s5_c_im[0])
    z_p, s5re_p, s5im_p = s5_prompt(proj, n_prompt, seq, s5p, s5_d[0])
    z_s, s5re_s, s5im_s = s5_sample(proj, n_prompt, state_s5_re[0], state_s5_im[0], s5p, s5_d[0])
    z = jnp.concatenate([z_p, z_s], axis=0)
    ya = matmul(z.astype(BF16), s5_w_glu.astype(BF16), mode="glu", extras=(z,), out_dtype=BF16)
    yb, v_last = sgu(proj, n_prompt, seq, sgu_ln_g[0], sgu_ln_b[0], sgu_w[0], sgu_b[0])
    x = matmul(ya, ev_w_out.astype(BF16), mode="resid2", extras=(x, yb))
    x = _peer_ple(x, p16, 0, norm_ffn, norm_ple, *peer_w)

    nh = GDN_HEADS
    n_ch = gdn_conv_w.shape[-1]
    n_qkvz = n_ch + nh * LANE
    h = rmsnorm(x, norm_mix[1], BF16)
    proj = matmul(h, gdn_w_in.astype(BF16), n_out=n_qkvz, out_dtype=BF16)
    w_ab = jnp.pad(gdn_w_in[:, :, n_qkvz:], ((0, 0), (0, 0), (0, LANE - 2 * nh))).astype(BF16)
    ab = matmul(h, w_ab)
    tm_g = 512
    brep_p, gcrep_p = gdn_gates(ab[:n_prompt], gdn_a_log[0], gdn_dt_bias[0], _chunk_tril(tm_g, GDN_CHUNK))
    brep_s, grep_s = gdn_gates(ab[n_prompt:], gdn_a_log[0], gdn_dt_bias[0], jnp.eye(ns, dtype=F32))
    qkv_p = gdn_conv_prompt(proj, n_prompt, seq, gdn_conv_w[0], n_ch)
    buf = state_gdn_conv[0]
    qkv_s = gdn_conv_sample(proj, n_prompt, jnp.swapaxes(buf, 0, 1), gdn_conv_w[0], n_ch)
    og_p, gdn_p = gdn_core_prompt(qkv_p, brep_p, gcrep_p, proj, n_prompt, seq, gdn_norm_g[0])
    og_s, gdn_s = gdn_core_sample(qkv_s, brep_s, grep_s, proj, n_prompt, state_gdn[0], gdn_norm_g[0])
    x = matmul(jnp.concatenate([og_p, og_s], axis=0), gdn_w_out.astype(BF16), mode="resid", extras=(x,))
    x = _peer_ple(x, p16, 1, norm_ffn, norm_ple, *peer_w)

    y_p, y_s = rmsnorm_split(x, norm_final, n_prompt)

    conv_p = jnp.stack([proj[(b + 1) * seq - (GDN_CONV - 1):(b + 1) * seq, :n_ch] for b in range(bsz)]).astype(F32)
    conv_s = jnp.concatenate([buf[:, 1:], proj[n_prompt:, None, :n_ch].astype(F32)], axis=1)
    n_v = bsz * SGU_CHUNK
    return (
        y_p.reshape(bsz, seq, d),
        y_s.reshape(ns, 1, d),
        s5re_p[None], s5im_p[None], s5re_s[None], s5im_s[None],
        v_last[:n_v].reshape(1, bsz, SGU_CHUNK, -1),
        v_last[n_v:].reshape(1, ns, 1, -1),
        gdn_p[None], gdn_s[None],
        conv_p[None], conv_s[None],
    )
```

```python
import functools

import jax
import jax.numpy as jnp
from jax import lax
from jax.experimental import pallas as pl
from jax.experimental.pallas import tpu as pltpu

F32 = jnp.float32
BF16 = jnp.bfloat16
EPS = 1e-6
LANE = 128
V7X_VMEM_BYTES = 64 * 1024 * 1024
VMEM_LIMIT = V7X_VMEM_BYTES - 8 * 1024 * 1024

S5_GROUP = 16
S5_STATE = 64
SGU_GROUPS = 8
SGU_CHUNK = 128
GDN_HEADS = 32
GDN_DK = 128
GDN_CONV = 4
GDN_CHUNK = 64
PEER_HEADS = 8
PEER_NKEYS = 128
PEER_TOPK = 16
PEER_BLOCK_KEYS = 4
GATE_DTYPE = jnp.float32

TOK_TILE = 640
ROW_TILE = 128


def _params(sem, vmem=VMEM_LIMIT):
    return pltpu.CompilerParams(dimension_semantics=sem, vmem_limit_bytes=vmem)


def _dot(a, b):
    return jnp.dot(a, b, preferred_element_type=F32)


def _dot_nt(a, b):
    return lax.dot_general(a, b, (((1,), (1,)), ((), ())), preferred_element_type=F32)


def _dot_tn(a, b):
    return lax.dot_general(a, b, (((0,), (0,)), ((), ())), preferred_element_type=F32)


def _rms(x, g):
    return x * lax.rsqrt(jnp.mean(x * x, axis=-1, keepdims=True) + EPS) * g


def _silu(x):
    return x * jax.nn.sigmoid(x)


def _rmsnorm_kernel(x_ref, g_ref, o_ref, *, transpose):
    y = _rms(x_ref[...], g_ref[...])
    if transpose:
        y = y.T
    o_ref[...] = y.astype(o_ref.dtype)


def rmsnorm(x, g, out_dtype, transpose=False):
    t, d = x.shape
    tm = ROW_TILE
    if transpose:
        out_shape, out_spec = (d, t), pl.BlockSpec((d, tm), lambda i: (0, i))
    else:
        out_shape, out_spec = (t, d), pl.BlockSpec((tm, d), lambda i: (i, 0))
    return pl.pallas_call(
        functools.partial(_rmsnorm_kernel, transpose=transpose),
        grid=(t // tm,),
        in_specs=[pl.BlockSpec((tm, d), lambda i: (i, 0)), pl.BlockSpec((1, d), lambda i: (0, 0))],
        out_specs=out_spec,
        out_shape=jax.ShapeDtypeStruct(out_shape, out_dtype),
        compiler_params=_params(("parallel",)),
        name="rmsnorm_t" if transpose else "rmsnorm",
    )(x, g.reshape(1, d))


def _rmsnorm_split_kernel(x_ref, g_ref, head_ref, tail_ref, *, n_head_blocks):
    i = pl.program_id(0)
    y = _rms(x_ref[...], g_ref[...])

    @pl.when(i < n_head_blocks)
    def _():
        head_ref[...] = y

    @pl.when(i >= n_head_blocks)
    def _():
        tail_ref[...] = y


def rmsnorm_split(x, g, n_head):
    t, d = x.shape
    tm = ROW_TILE
    nhb = n_head // tm
    return pl.pallas_call(
        functools.partial(_rmsnorm_split_kernel, n_head_blocks=nhb),
        grid=(t // tm,),
        in_specs=[pl.BlockSpec((tm, d), lambda i: (i, 0)), pl.BlockSpec((1, d), lambda i: (0, 0))],
        out_specs=[pl.BlockSpec((tm, d), lambda i: (jnp.minimum(i, nhb - 1), 0)),
                   pl.BlockSpec((tm, d), lambda i: (jnp.maximum(i - nhb, 0), 0))],
        out_shape=[jax.ShapeDtypeStruct((n_head, d), F32), jax.ShapeDtypeStruct((t - n_head, d), F32)],
        compiler_params=_params(("arbitrary",)),
        name="rmsnorm_split",
    )(x, g.reshape(1, d))


def _add_norm_kernel(x_ref, o_ref, g_ref, x2_ref, h_ref):
    x2 = x_ref[...] + o_ref[...]
    x2_ref[...] = x2
    h_ref[...] = _rms(x2, g_ref[...]).astype(h_ref.dtype)


def add_norm(x, o, g):
    t, d = x.shape
    tm = ROW_TILE
    row = pl.BlockSpec((tm, d), lambda i: (i, 0))
    return pl.pallas_call(
        _add_norm_kernel,
        grid=(t // tm,),
        in_specs=[row, row, pl.BlockSpec((1, d), lambda i: (0, 0))],
        out_specs=[row, row],
        out_shape=[jax.ShapeDtypeStruct((t, d), F32), jax.ShapeDtypeStruct((t, d), BF16)],
        compiler_params=_params(("parallel",)),
        name="add_norm",
    )(x, o, g.reshape(1, d))


def _mm_kernel(*refs, mode):
    x_ref, w_ref = refs[0], refs[1]
    o_ref = refs[-1]
    if mode == "resid2":
        x2_ref = refs[3]
        k1 = x_ref.shape[1]
        acc = refs[2][...] + (_dot(x_ref[...], w_ref[:k1, :]) + _dot(x2_ref[...], w_ref[k1:, :]))
        o_ref[...] = acc.astype(o_ref.dtype)
        return
    acc = _dot(x_ref[...], w_ref[...])
    if mode == "resid":
        acc = refs[2][...] + acc
    elif mode == "glu":
        z = refs[2][...]
        acc = z * jax.nn.sigmoid(acc)
    elif mode == "ple":
        r_ref, p_ref, pw_ref = refs[2], refs[3], refs[4]
        acc = r_ref[...] + jax.nn.sigmoid(acc) * _dot(p_ref[...], pw_ref[...])
    o_ref[...] = acc.astype(o_ref.dtype)


def matmul(x, w, *, layer=0, n_out=None, mode="plain", extras=(), out_dtype=F32, tn=1024):
    m, k = x.shape
    n = w.shape[2] if n_out is None else n_out
    tm = TOK_TILE
    tn = min(tn, n)
    in_specs = [pl.BlockSpec((tm, k), lambda j, i: (i, 0)),
                pl.BlockSpec((None, w.shape[1], tn), lambda j, i: (layer, 0, j))]
    if mode in ("resid", "resid2", "glu", "ple"):
        in_specs.append(pl.BlockSpec((tm, tn), lambda j, i: (i, j)))
    if mode == "resid2":
        in_specs.append(pl.BlockSpec((tm, extras[1].shape[1]), lambda j, i: (i, 0)))
    if mode == "ple":
        kp = extras[1].shape[1]
        in_specs += [pl.BlockSpec((tm, kp), lambda j, i: (i, 0)),
                     pl.BlockSpec((None, kp, tn), lambda j, i: (layer, 0, j))]
    return pl.pallas_call(
        functools.partial(_mm_kernel, mode=mode),
        grid=(n // tn, m // tm),
        in_specs=in_specs,
        out_specs=pl.BlockSpec((tm, tn), lambda j, i: (i, j)),
        out_shape=jax.ShapeDtypeStruct((m, n), out_dtype),
        compiler_params=_params(("parallel", "parallel")),
        name="mm_" + mode,
    )(x, w, *extras)


def _s5_prompt_kernel(x0, x1, x2, x3, bcat_ref, ccat_ref, lre_ref, lim_ref, d_ref,
                      z_ref, hre_out, him_out, u_scr, bu_scr, h_scr, y_scr, st_re, st_im, *, c):
    tc = pl.program_id(1)

    @pl.when(tc == 0)
    def _():
        st_re[...] = jnp.zeros_like(st_re)
        st_im[...] = jnp.zeros_like(st_im)

    xs = (x0, x1, x2, x3)
    half_w = st_re.shape[1]
    zeros = jnp.zeros((c, LANE), F32)
    for b in range(4):
        for half in range(2):
            s = 2 * b + half
            u_scr[half, pl.ds(s, c, stride=8), :] = xs[b][:, half * LANE:(half + 1) * LANE]
            u_scr[1 - half, pl.ds(s, c, stride=8), :] = zeros
    bu_scr[...] = _dot(jnp.concatenate([u_scr[0], u_scr[1]], axis=1), bcat_ref[0])
    lr = lre_ref[0]
    li = lim_ref[0]
    hr = st_re[...]
    hi = st_im[...]
    for t in range(c):
        rows = slice(8 * t, 8 * t + 8)
        nr = lr * hr - li * hi + bu_scr[rows, :half_w]
        ni = lr * hi + li * hr + bu_scr[rows, half_w:]
        h_scr[rows, :half_w] = nr
        h_scr[rows, half_w:] = ni
        hr, hi = nr, ni
    st_re[...] = hr
    st_im[...] = hi
    hre_out[0] = hr
    him_out[0] = hi
    y = _dot(h_scr[...], ccat_ref[0])
    odd = lax.broadcasted_iota(jnp.int32, (8 * c, LANE), 0) % 2 == 1
    y_scr[...] = jnp.where(odd, y[:, LANE:], y[:, :LANE])
    for b in range(4):
        for half in range(2):
            s = 2 * b + half
            ls = slice(half * LANE, (half + 1) * LANE)
            ys = y_scr[pl.ds(s, c, stride=8), :]
            z_ref[b, :, ls] = jax.nn.gelu(ys + d_ref[:, ls] * xs[b][:, ls])


def _s5_sample_kernel(x_ref, h0re_ref, h0im_ref, bblk_ref, cblk_ref, lre_ref, lim_ref, d_ref,
                      z_ref, hre_out, him_out):
    x = x_ref[...]
    half_w = h0re_ref.shape[1]
    bu = _dot(x, bblk_ref[0])
    lr = lre_ref[0]
    li = lim_ref[0]
    h0r = h0re_ref[...]
    h0i = h0im_ref[...]
    nr = lr * h0r - li * h0i + bu[:, :half_w]
    ni = lr * h0i + li * h0r + bu[:, half_w:]
    hre_out[...] = nr
    him_out[...] = ni
    y = _dot(nr, cblk_ref[0, :half_w, :]) + _dot(ni, cblk_ref[0, half_w:, :]) + d_ref[...] * x
    z_ref[...] = jax.nn.gelu(y)


def _s5_params(a_re, a_im, log_dt, b_re, b_im, c_re, c_im):
    g, p = a_re.shape
    nb = g // 8
    lam = lax.complex(a_re, a_im)
    dt = jnp.exp(log_dt)[:, None]
    lam_bar = jnp.exp(lam * dt)
    b_bar = ((lam_bar - 1.0) / lam)[..., None] * lax.complex(b_re, b_im)
    eye = jnp.eye(8, dtype=F32)

    def bmat(v):
        v = jnp.swapaxes(v, 1, 2).reshape(nb, 8, S5_GROUP, p)
        return jnp.einsum("kjcp,jl->kjclp", v, eye).reshape(nb, 8 * S5_GROUP, 8 * p)

    def cmat(v):
        v = jnp.swapaxes(v, 1, 2).reshape(nb, 8, p, S5_GROUP)
        return jnp.einsum("kjpc,jl->kjplc", v, eye).reshape(nb, 8 * p, 8 * S5_GROUP)

    bblk = jnp.concatenate([bmat(jnp.real(b_bar)), bmat(jnp.imag(b_bar))], axis=2)
    cblk = jnp.concatenate([cmat(c_re), cmat(-c_im)], axis=1)
    lre = jnp.real(lam_bar).reshape(nb, 8 * p)
    lim = jnp.imag(lam_bar).reshape(nb, 8 * p)
    return bblk, cblk, lre, lim


def s5_prompt(proj, n_prompt, seq, params, d_skip, c=128):
    bblk, cblk, lre, lim = params
    nb, _, two_w = bblk.shape
    half_w = two_w // 2
    bsz = n_prompt // seq
    assert bsz == 4
    w = nb * 8 * S5_GROUP
    ngb = nb // 2
    nt = seq // c
    lre_t = jnp.tile(lre.reshape(ngb, 2, half_w), (1, bsz, 1))
    lim_t = jnp.tile(lim.reshape(ngb, 2, half_w), (1, bsz, 1))
    bcat = bblk.reshape(ngb, 2 * LANE, two_w)
    ccat = jnp.swapaxes(cblk.reshape(ngb, 2, two_w, LANE), 1, 2).reshape(ngb, two_w, 2 * LANE)
    x_specs = [pl.BlockSpec((c, 2 * LANE), functools.partial(lambda gb, tc, b: (b * nt + tc, gb), b=b))
               for b in range(bsz)]
    z, hre, him = pl.pallas_call(
        functools.partial(_s5_prompt_kernel, c=c),
        grid=(ngb, nt),
        in_specs=x_specs + [
            pl.BlockSpec((1, 2 * LANE, two_w), lambda gb, tc: (gb, 0, 0)),
            pl.BlockSpec((1, two_w, 2 * LANE), lambda gb, tc: (gb, 0, 0)),
            pl.BlockSpec((1, 8, half_w), lambda gb, tc: (gb, 0, 0)),
            pl.BlockSpec((1, 8, half_w), lambda gb, tc: (gb, 0, 0)),
            pl.BlockSpec((1, 2 * LANE), lambda gb, tc: (0, gb)),
        ],
        out_specs=[
            pl.BlockSpec((bsz, c, 2 * LANE), lambda gb, tc: (0, tc, gb)),
            pl.BlockSpec((1, 8, half_w), lambda gb, tc: (gb, 0, 0)),
            pl.BlockSpec((1, 8, half_w), lambda gb, tc: (gb, 0, 0)),
        ],
        out_shape=[
            jax.ShapeDtypeStruct((bsz, seq, w), F32),
            jax.ShapeDtypeStruct((ngb, 8, half_w), F32),
            jax.ShapeDtypeStruct((ngb, 8, half_w), F32),
        ],
        scratch_shapes=[pltpu.VMEM((2, 8 * c, LANE), F32), pltpu.VMEM((8 * c, two_w), F32),
                        pltpu.VMEM((8 * c, two_w), F32), pltpu.VMEM((8 * c, LANE), F32),
                        pltpu.VMEM((8, half_w), F32), pltpu.VMEM((8, half_w), F32)],
        compiler_params=_params(("parallel", "arbitrary")),
        name="s5_prompt",
    )(proj, proj, proj, proj, bcat, ccat, lre_t, lim_t, d_skip.reshape(1, w))

    def states(h):
        h = h.reshape(ngb, bsz, 2, 8, S5_STATE)
        return jnp.transpose(h, (1, 0, 2, 3, 4)).reshape(bsz, nb * 8, S5_STATE)

    return z.reshape(n_prompt, w), states(hre), states(him)


def s5_sample(proj, n_prompt, h0_re, h0_im, params, d_skip):
    bblk, cblk, lre, lim = params
    nb, _, two_w = bblk.shape
    half_w = two_w // 2
    ns = h0_re.shape[0]
    w = nb * 8 * S5_GROUP
    row_blk = n_prompt // ns
    z, hre, him = pl.pallas_call(
        _s5_sample_kernel,
        grid=(nb,),
        in_specs=[
            pl.BlockSpec((ns, LANE), lambda k: (row_blk, k)),
            pl.BlockSpec((ns, half_w), lambda k: (0, k)),
            pl.BlockSpec((ns, half_w), lambda k: (0, k)),
            pl.BlockSpec((1, LANE, two_w), lambda k: (k, 0, 0)),
            pl.BlockSpec((1, two_w, LANE), lambda k: (k, 0, 0)),
            pl.BlockSpec((1, 1, half_w), lambda k: (k, 0, 0)),
            pl.BlockSpec((1, 1, half_w), lambda k: (k, 0, 0)),
            pl.BlockSpec((1, LANE), lambda k: (0, k)),
        ],
        out_specs=[
            pl.BlockSpec((ns, LANE), lambda k: (0, k)),
            pl.BlockSpec((ns, half_w), lambda k: (0, k)),
            pl.BlockSpec((ns, half_w), lambda k: (0, k)),
        ],
        out_shape=[
            jax.ShapeDtypeStruct((ns, w), F32),
            jax.ShapeDtypeStruct((ns, nb * half_w), F32),
            jax.ShapeDtypeStruct((ns, nb * half_w), F32),
        ],
        compiler_params=_params(("parallel",)),
        name="s5_sample",
    )(proj, h0_re.reshape(ns, -1), h0_im.reshape(ns, -1), bblk, cblk,
      lre.reshape(nb, 1, half_w), lim.reshape(nb, 1, half_w), d_skip.reshape(1, w))
    return z, hre.reshape(ns, nb * 8, S5_STATE), him.reshape(ns, nb * 8, S5_STATE)


def _sgu_kernel(u_ref, v_ref, g_ref, b_ref, w_ref, bias_ref, y_ref, vl_ref):
    u = jax.nn.gelu(u_ref[...])
    v = jax.nn.gelu(v_ref[...])
    mu = jnp.mean(v, axis=-1, keepdims=True)
    vc = v - mu
    vn = vc * lax.rsqrt(jnp.mean(vc * vc, axis=-1, keepdims=True) + EPS) * g_ref[...] + b_ref[...]
    vl_ref[...] = vn
    c = u.shape[0]
    dg = u.shape[1] // SGU_GROUPS
    causal = lax.broadcasted_iota(jnp.int32, (c, c), 0) >= lax.broadcasted_iota(jnp.int32, (c, c), 1)
    for g in range(SGU_GROUPS):
        ls = slice(g * dg, (g + 1) * dg)
        w = jnp.where(causal, w_ref[0, g], 0.0).astype(BF16)
        mixed = _dot(w, vn[:, ls].astype(BF16)) + bias_ref[0, g]
        y_ref[:, ls] = (u[:, ls] * mixed).astype(y_ref.dtype)


def sgu(proj, n_prompt, seq, ln_g, ln_b, w_s, b_s):
    t = proj.shape[0]
    wdt = proj.shape[1] // 3
    c = SGU_CHUNK
    dg = wdt // SGU_GROUPS
    n_chunks = t // c
    per_seq = seq // c
    n_prompt_chunks = n_prompt // c
    eye = jnp.eye(c, dtype=F32)
    w_sets = jnp.stack([w_s, w_s[:, :1, :1] * eye])
    bias_sets = jnp.stack([jnp.broadcast_to(b_s[:, :, None], (SGU_GROUPS, c, dg)),
                           jnp.broadcast_to(b_s[:, :1, None], (SGU_GROUPS, c, dg))])
    n_last = n_prompt // seq + (n_chunks - n_prompt_chunks)
    y, vl = pl.pallas_call(
        _sgu_kernel,
        grid=(n_chunks,),
        in_specs=[
            pl.BlockSpec((c, wdt), lambda i: (i, 1)),
            pl.BlockSpec((c, wdt), lambda i: (i, 2)),
            pl.BlockSpec((1, wdt), lambda i: (0, 0)),
            pl.BlockSpec((1, wdt), lambda i: (0, 0)),
            pl.BlockSpec((1, SGU_GROUPS, c, c), lambda i: (i // n_prompt_chunks, 0, 0, 0)),
            pl.BlockSpec((1, SGU_GROUPS, c, dg), lambda i: (i // n_prompt_chunks, 0, 0, 0)),
        ],
        out_specs=[
            pl.BlockSpec((c, wdt), lambda i: (i, 0)),
            pl.BlockSpec((c, wdt), lambda i: (i // per_seq, 0)),
        ],
        out_shape=[jax.ShapeDtypeStruct((t, wdt), BF16), jax.ShapeDtypeStruct((n_last * c, wdt), F32)],
        compiler_params=_params(("arbitrary",)),
        name="sgu",
    )(proj, proj, ln_g.reshape(1, wdt), ln_b.reshape(1, wdt), w_sets, bias_sets)
    return y, vl


def _extract16(arrays):
    l = arrays[0].shape[1]
    viota = lax.broadcasted_iota(jnp.int32, (PEER_TOPK, l), 0)

    def body(r, carry):
        out = []
        for s, vals, idxs in carry:
            iota = lax.broadcasted_iota(jnp.int32, s.shape, 0)
            m = jnp.max(s, axis=0, keepdims=True)
            idx = jnp.min(jnp.where(s == m, iota, s.shape[0]), axis=0, keepdims=True)
            out.append((jnp.where(iota == idx, -jnp.inf, s), jnp.where(viota == r, m, vals),
                        jnp.where(viota == r, idx, idxs)))
        return tuple(out)

    init = tuple((s, jnp.zeros((PEER_TOPK, l), F32), jnp.zeros((PEER_TOPK, l), jnp.int32)) for s in arrays)
    return lax.fori_loop(0, PEER_TOPK, body, init)


def _ranks(idxs, n):
    iota = lax.broadcasted_iota(jnp.int32, (n, idxs.shape[1]), 0)
    rank = jnp.full(iota.shape, PEER_TOPK, jnp.int32)
    for r in range(PEER_TOPK):
        rank = jnp.where(iota == idxs[r:r + 1], r, rank)
    return rank


def _peer_candidates(t1, t2):
    k = PEER_TOPK
    sub = lax.broadcasted_iota(jnp.int32, (8, t1.shape[1]), 0)
    groups = [t1[0:1] + t2[0:8], t1[0:1] + t2[8:16], t1[1:2] + t2[0:8]]
    slices = [slice(0, 16), slice(16, 24)]
    for i in range(2, 8):
        groups.append(jnp.where(sub < k // (i + 1), t1[i:i + 1] + t2[0:8], -jnp.inf))
        slices.append(slice(8 * (i + 1), 8 * (i + 2)))
    groups.append(t1[8:16] + t2[0:1])
    slices += [slice(64 + i, 65 + i) for i in range(8, k)]
    return jnp.concatenate(groups, axis=0), slices


def _peer_topk_kernel(ht_ref, wq_ref, keys_ref, r2_ref, e2_ref, n_ref, e1_ref):
    qt = _dot_tn(wq_ref[...], ht_ref[...]).astype(BF16)
    nk = keys_ref.shape[2]
    dq = keys_ref.shape[3]
    s1_all = _dot(keys_ref[0, 0].astype(BF16), qt[:dq])
    s2_all = _dot(keys_ref[0, 1].astype(BF16), qt[dq:])
    for c in range(ht_ref.shape[1] // LANE):
        cs = slice(c * LANE, (c + 1) * LANE)
        s1 = s1_all[:, cs]
        s2 = s2_all[:, cs]
        (_, t1, idx1), (_, t2, idx2) = _extract16([s1, s2])
        rank1 = _ranks(idx1, nk)
        rank2 = _ranks(idx2, nk)
        cand, cand_rows = _peer_candidates(t1, t2)
        (cand_left, _, _), = _extract16([cand])
        selected = (cand_left == -jnp.inf) & (cand > -jnp.inf)
        cmax = t1[0:1] + t2[0:1]
        zsum = jnp.sum(jnp.where(selected, jnp.exp(cand - cmax), 0.0), axis=0, keepdims=True)
        n = jnp.zeros((nk, LANE), F32)
        for i in range(PEER_TOPK):
            cnt = jnp.sum(jnp.where(selected[cand_rows[i]], 1.0, 0.0), axis=0, keepdims=True)
            n = jnp.where(rank1 == i, cnt, n)
        r2_ref[0, :, cs] = rank2.astype(F32).astype(r2_ref.dtype)
        e1 = jnp.exp(s1 - t1[0:1])
        kb = n_ref.shape[2]
        for a in range(nk):
            n_ref[0, a // kb, a % kb:a % kb + 1, cs] = n[a:a + 1]
            e1_ref[0, a // kb, a % kb:a % kb + 1, cs] = e1[a:a + 1]
        e2_ref[0, :, cs] = (jnp.exp(s2 - t2[0:1]) / zsum).astype(e2_ref.dtype)


def peer_topk(h_t, wq, keys, layer):
    d, t = h_t.shape
    nh, _, nk, dq = keys.shape
    tt = TOK_TILE
    kb = PEER_BLOCK_KEYS
    spec = pl.BlockSpec((1, nk, tt), lambda j, h: (h, 0, j))
    shp = jax.ShapeDtypeStruct((nh, nk, t), GATE_DTYPE)
    spec_a = pl.BlockSpec((1, nk // kb, kb, tt), lambda j, h: (h, 0, 0, j))
    shp_a = jax.ShapeDtypeStruct((nh, nk // kb, kb, t), GATE_DTYPE)
    return pl.pallas_call(
        _peer_topk_kernel,
        grid=(t // tt, nh),
        in_specs=[
            pl.BlockSpec((d, tt), lambda j, h: (0, j)),
            pl.BlockSpec((None, d, 2 * dq), lambda j, h: (layer, 0, h)),
            pl.BlockSpec((1, 2, nk, dq), lambda j, h: (h, 0, 0, 0)),
        ],
        out_specs=[spec, spec, spec_a, spec_a],
        out_shape=[shp, shp, shp_a, shp_a],
        compiler_params=_params(("parallel", "arbitrary")),
        name="peer_topk",
    )(h_t, wq, keys)


def _peer_dense_kernel(ht_ref, u_ref, v_ref, r2_ref, e2_ref, n_ref, e1_ref, o_ref, s_a, s_b, act_scr,
                       *, na, nh, nb):
    i = pl.program_id(1)
    tt = ht_ref.shape[1]
    nk = r2_ref.shape[1]
    d = o_ref.shape[1]
    cb = 512
    tiles = [(a, c) for a in range(na) for c in range(tt // LANE)]

    @pl.when(i == 0)
    def _():
        o_ref[...] = jnp.zeros_like(o_ref)

    def step(s_prev, s_cur):
        @pl.when(i < nb)
        def _():
            s_cur[...] = _dot(u_ref[...], ht_ref[...])

        @pl.when(i > 0)
        def _():
            for a, c in tiles:
                rs = slice(a * nk, (a + 1) * nk)
                cs = slice(c * LANE, (c + 1) * LANE)
                w = jnp.zeros((nk, LANE), F32)
                for h in range(nh):
                    nrow = n_ref[h, 0, a:a + 1, cs]
                    e1row = e1_ref[h, 0, a:a + 1, cs]
                    w = w + jnp.where(r2_ref[h, :, cs] < nrow, e2_ref[h, :, cs] * e1row, 0.0)
                act_scr[cs, rs] = (jax.nn.gelu(s_prev[rs, cs]) * w).T.astype(BF16)
            for r in range(d // cb):
                o_ref[:, r * cb:(r + 1) * cb] += _dot(act_scr[...], v_ref[:, r * cb:(r + 1) * cb])

    @pl.when(i % 2 == 0)
    def _():
        step(s_b, s_a)

    @pl.when(i % 2 == 1)
    def _():
        step(s_a, s_b)


def peer_dense(h_t, u, v, r2, e2, n, e1, layer):
    d, t = h_t.shape
    e = u.shape[1]
    nh, nk, _ = r2.shape
    tt = TOK_TILE
    na = PEER_BLOCK_KEYS
    ne = na * nk
    nb = e // ne
    assert n.shape == (nh, nk // na, na, t) and e1.shape == n.shape
    once = pl.Buffered(1)

    def score_blk(i):
        return jnp.minimum(i, nb - 1)

    def value_blk(i):
        return jnp.maximum(i - 1, 0)

    key_spec = pl.BlockSpec((nh, 1, na, tt), lambda j, i: (0, value_blk(i), 0, j))
    return pl.pallas_call(
        functools.partial(_peer_dense_kernel, na=na, nh=nh, nb=nb),
        grid=(t // tt, nb + 1),
        in_specs=[
            pl.BlockSpec((d, tt), lambda j, i: (0, j), pipeline_mode=once),
            pl.BlockSpec((None, ne, d), lambda j, i: (layer, score_blk(i), 0)),
            pl.BlockSpec((None, ne, d), lambda j, i: (layer, value_blk(i), 0)),
            pl.BlockSpec((nh, nk, tt), lambda j, i: (0, 0, j), pipeline_mode=once),
            pl.BlockSpec((nh, nk, tt), lambda j, i: (0, 0, j), pipeline_mode=once),
            key_spec,
            key_spec,
        ],
        out_specs=pl.BlockSpec((tt, d), lambda j, i: (j, 0)),
        out_shape=jax.ShapeDtypeStruct((t, d), F32),
        scratch_shapes=[pltpu.VMEM((ne, tt), F32), pltpu.VMEM((ne, tt), F32), pltpu.VMEM((tt, ne), BF16)],
        compiler_params=_params(("parallel", "arbitrary")),
        name="peer_dense",
    )(h_t, u, v, r2, e2, n, e1)


def _softplus(x):
    return jnp.maximum(x, 0.0) + jnp.log1p(jnp.exp(-jnp.abs(x)))


def _replicate(x, onehot):
    hi = x.astype(BF16)
    rest = x - hi.astype(F32)
    mid = rest.astype(BF16)
    lo = (rest - mid.astype(F32)).astype(BF16)
    sel = onehot.astype(BF16)
    return (_dot(hi, sel) + _dot(mid, sel)) + _dot(lo, sel)


def _gdn_gates_kernel(ab_ref, alog_ref, dtb_ref, tril_ref, eg_ref, eb_ref, brep_ref, gcrep_ref):
    ab = ab_ref[...]
    g = -jnp.exp(alog_ref[...]) * _softplus(ab + dtb_ref[...])
    beta = jax.nn.sigmoid(ab)
    gc = jnp.dot(tril_ref[...], g, preferred_element_type=F32, precision=lax.Precision.HIGHEST)
    gcrep_ref[...] = _replicate(gc, eg_ref[...])
    brep_ref[...] = _replicate(beta, eb_ref[...])


def gdn_gates(ab, a_log, dt_bias, tril):
    rows = ab.shape[0]
    tm = tril.shape[0]
    nh = a_log.shape[0]
    wide = nh * LANE
    pad = LANE - nh
    alog_p = jnp.pad(a_log, (0, pad)).reshape(1, LANE)
    dtb_p = jnp.pad(dt_bias, (0, pad)).reshape(1, LANE)
    head_of_col = jnp.arange(wide) // LANE
    lane = jnp.arange(LANE)[:, None]
    e_g = (lane == head_of_col[None, :]).astype(F32)
    e_b = (lane == head_of_col[None, :] + nh).astype(F32)
    tn = 1024
    return pl.pallas_call(
        _gdn_gates_kernel,
        grid=(rows // tm, wide // tn),
        in_specs=[
            pl.BlockSpec((tm, LANE), lambda i, j: (i, 0)),
            pl.BlockSpec((1, LANE), lambda i, j: (0, 0)),
            pl.BlockSpec((1, LANE), lambda i, j: (0, 0)),
            pl.BlockSpec((tm, tm), lambda i, j: (0, 0)),
            pl.BlockSpec((LANE, tn), lambda i, j: (0, j)),
            pl.BlockSpec((LANE, tn), lambda i, j: (0, j)),
        ],
        out_specs=[pl.BlockSpec((tm, tn), lambda i, j: (i, j)), pl.BlockSpec((tm, tn), lambda i, j: (i, j))],
        out_shape=[jax.ShapeDtypeStruct((rows, wide), F32), jax.ShapeDtypeStruct((rows, wide), F32)],
        compiler_params=_params(("parallel", "parallel")),
        name="gdn_gates",
    )(ab, alog_p, dtb_p, tril, e_g, e_b)


def _gdn_post_conv(y, o_ref, cb, n_qk_blocks, n_q_blocks):
    y = _silu(y)
    is_qk = cb < n_qk_blocks
    qscale = jnp.where(cb < n_q_blocks, GDN_DK ** -0.5, 1.0)
    for hh in range(y.shape[1] // GDN_DK):
        ls = slice(hh * GDN_DK, (hh + 1) * GDN_DK)
        seg = y[:, ls]
        rs = lax.rsqrt(jnp.sum(seg * seg, axis=-1, keepdims=True) + EPS)
        o_ref[:, ls] = (seg * jnp.where(is_qk, rs * qscale, 1.0)).astype(o_ref.dtype)


def _gdn_conv_kernel(x_ref, w_ref, o_ref, ext, *, tm, n_qk_blocks, n_q_blocks):
    cb = pl.program_id(0)
    tt = pl.program_id(2)

    @pl.when(tt == 0)
    def _():
        ext[0:8, :] = jnp.zeros((8, ext.shape[1]), F32)

    ext[8:8 + tm, :] = x_ref[...].astype(F32)
    w = w_ref[...]
    y = w[0:1] * ext[5:5 + tm, :]
    for tap in range(1, GDN_CONV):
        y = y + w[tap:tap + 1] * ext[5 + tap:5 + tap + tm, :]
    ext[0:8, :] = ext[tm:tm + 8, :]
    _gdn_post_conv(y, o_ref, cb, n_qk_blocks, n_q_blocks)


def gdn_conv_prompt(proj, n_prompt, seq, conv_w, n_ch, tm=512, tc=1024):
    bsz = n_prompt // seq
    nt = seq // tm
    qk = (2 * n_ch) // 3
    return pl.pallas_call(
        functools.partial(_gdn_conv_kernel, tm=tm, n_qk_blocks=qk // tc, n_q_blocks=qk // 2 // tc),
        grid=(n_ch // tc, bsz, nt),
        in_specs=[pl.BlockSpec((tm, tc), lambda cb, b, tt: (b * nt + tt, cb)),
                  pl.BlockSpec((GDN_CONV, tc), lambda cb, b, tt: (0, cb))],
        out_specs=pl.BlockSpec((tm, tc), lambda cb, b, tt: (b * nt + tt, cb)),
        out_shape=jax.ShapeDtypeStruct((n_prompt, n_ch), BF16),
        scratch_shapes=[pltpu.VMEM((tm + 8, tc), F32)],
        compiler_params=_params(("parallel", "parallel", "arbitrary")),
        name="gdn_conv_prompt",
    )(proj, conv_w)


def _gdn_conv_sample_kernel(x_ref, buf_ref, w_ref, o_ref, *, n_qk_blocks, n_q_blocks):
    cb = pl.program_id(0)
    w = w_ref[...]
    y = w[0:1] * buf_ref[0]
    for tap in range(1, GDN_CONV - 1):
        y = y + w[tap:tap + 1] * buf_ref[tap]
    y = y + w[GDN_CONV - 1:GDN_CONV] * x_ref[...].astype(F32)
    _gdn_post_conv(y, o_ref, cb, n_qk_blocks, n_q_blocks)


def gdn_conv_sample(proj, n_prompt, buf_t, conv_w, n_ch, tc=1024):
    ns = buf_t.shape[1]
    row_blk = n_prompt // ns
    qk = (2 * n_ch) // 3
    return pl.pallas_call(
        functools.partial(_gdn_conv_sample_kernel, n_qk_blocks=qk // tc, n_q_blocks=qk // 2 // tc),
        grid=(n_ch // tc,),
        in_specs=[pl.BlockSpec((ns, tc), lambda cb: (row_blk, cb)),
                  pl.BlockSpec((GDN_CONV - 1, ns, tc), lambda cb: (0, 0, cb)),
                  pl.BlockSpec((GDN_CONV, tc), lambda cb: (0, cb))],
        out_specs=pl.BlockSpec((ns, tc), lambda cb: (0, cb)),
        out_shape=jax.ShapeDtypeStruct((ns, n_ch), F32),
        compiler_params=_params(("parallel",)),
        name="gdn_conv_sample",
    )(proj, buf_t, conv_w)


def _unit_lower_inverses(mats):
    c = mats[0].shape[0]
    row = lax.broadcasted_iota(jnp.int32, (c, c), 0)
    col = lax.broadcasted_iota(jnp.int32, (c, c), 1)
    eye = jnp.where(row == col, 1.0, 0.0)
    blk = 16
    ds = [jnp.where(row // blk == col // blk, a, 0.0) for a in mats]
    d2 = [_dot(d, d) for d in ds]
    d4 = [_dot(d, d) for d in d2]
    d8 = [_dot(d, d) for d in d4]
    ts = [_dot(eye - d, eye + x) for d, x in zip(ds, d2)]
    ts = [_dot(t, eye + x) for t, x in zip(ts, d4)]
    ts = [_dot(t, eye + x) for t, x in zip(ts, d8)]
    while blk < c:
        off_mask = (row // (2 * blk) == col // (2 * blk)) & (row // blk != col // blk)
        tmp = [_dot(t, jnp.where(off_mask, a, 0.0)) for t, a in zip(ts, mats)]
        tmp = [_dot(x, t) for x, t in zip(tmp, ts)]
        ts = [t - x for t, x in zip(ts, tmp)]
        blk *= 2
    return ts


def _gdn_chunks(qs, ks, vs, brs, gcs, zs, ng, ss):
    c = qs[0].shape[0]
    dv = vs[0].shape[1]
    row = lax.broadcasted_iota(jnp.int32, (c, c), 0)
    col = lax.broadcasted_iota(jnp.int32, (c, c), 1)
    kb = [k * b for k, b in zip(ks, brs)]
    vb = [v * b for v, b in zip(vs, brs)]
    eg = [jnp.exp(g) for g in gcs]
    glast = [g[c - 1:c, :] for g in gcs]
    kbg = [x * e for x, e in zip(kb, eg)]
    qg = [q * e for q, e in zip(qs, eg)]
    kdec = [k * jnp.exp(gl - g) for k, gl, g in zip(ks, glast, gcs)]
    decay = [jnp.exp(jnp.where(row >= col, g[:, :c] - g.T[:c, :], -jnp.inf)) for g in gcs]
    k16 = [k.astype(BF16) for k in ks]
    a = [_dot_nt(x.astype(BF16), k) * jnp.where(row > col, d, 0.0) for x, k, d in zip(kb, k16, decay)]
    attn = [_dot_nt(q.astype(BF16), k) * d for q, k, d in zip(qs, k16, decay)]
    ts = _unit_lower_inverses(a)
    uw = [_dot(t, jnp.concatenate([x, y], axis=1)) for t, x, y in zip(ts, vb, kbg)]
    s16 = [s.astype(BF16) for s in ss]
    v_new = [x[:, :dv] - _dot(x[:, dv:].astype(BF16), s) for x, s in zip(uw, s16)]
    vn16 = [x.astype(BF16) for x in v_new]
    o_state = [_dot(x.astype(BF16), s) for x, s in zip(qg, s16)]
    o_local = [_dot(x.astype(BF16), v) for x, v in zip(attn, vn16)]
    s_new = [s * jnp.exp(gl) + _dot_tn(x.astype(BF16), v) for s, gl, x, v in zip(ss, glast, kdec, vn16)]
    og = [_rms(x + y, ng) * _silu(z) for x, y, z in zip(o_state, o_local, zs)]
    return og, s_new


def _gdn_core_kernel(q_ref, k_ref, v_ref, b_ref, gc_ref, z_ref, ng_ref, og_ref, sout_ref, s_scr, *, hb, nck, c):
    n = pl.program_id(2)

    @pl.when(n == 0)
    def _():
        s_scr[...] = jnp.zeros_like(s_scr)

    ng = ng_ref[...]
    ss = [s_scr[hh] for hh in range(hb)]
    for ck in range(nck):
        rs = slice(ck * c, (ck + 1) * c)

        def heads(ref):
            return [ref[rs, hh * LANE:(hh + 1) * LANE].astype(F32) for hh in range(hb)]

        og, ss = _gdn_chunks(heads(q_ref), heads(k_ref), heads(v_ref), heads(b_ref), heads(gc_ref), heads(z_ref),
                             ng, ss)
        for hh in range(hb):
            og_ref[rs, hh * LANE:(hh + 1) * LANE] = og[hh].astype(og_ref.dtype)
    for hh in range(hb):
        s_scr[hh] = ss[hh]
        sout_ref[0, hh] = ss[hh]


def gdn_core_prompt(qkv, brep, gcrep, proj, n_prompt, seq, norm_g, hb=GDN_HEADS, nck=1):
    c = GDN_CHUNK
    nh = GDN_HEADS
    bsz = n_prompt // seq
    tm = c * nck
    nt = seq // tm
    bw = hb * LANE
    nhb = nh // hb

    def rows(b, h, n):
        return b * nt + n

    og, s_out = pl.pallas_call(
        functools.partial(_gdn_core_kernel, hb=hb, nck=nck, c=c),
        grid=(bsz, nhb, nt),
        in_specs=[
            pl.BlockSpec((tm, bw), lambda b, h, n: (rows(b, h, n), h)),
            pl.BlockSpec((tm, bw), lambda b, h, n: (rows(b, h, n), nhb + h)),
            pl.BlockSpec((tm, bw), lambda b, h, n: (rows(b, h, n), 2 * nhb + h)),
            pl.BlockSpec((tm, bw), lambda b, h, n: (rows(b, h, n), h)),
            pl.BlockSpec((tm, bw), lambda b, h, n: (rows(b, h, n), h)),
            pl.BlockSpec((tm, bw), lambda b, h, n: (rows(b, h, n), 3 * nhb + h)),
            pl.BlockSpec((1, LANE), lambda b, h, n: (0, 0)),
        ],
        out_specs=[
            pl.BlockSpec((tm, bw), lambda b, h, n: (rows(b, h, n), h)),
            pl.BlockSpec((1, hb, GDN_DK, LANE), lambda b, h, n: (b, h, 0, 0)),
        ],
        out_shape=[jax.ShapeDtypeStruct((n_prompt, nh * LANE), BF16),
                   jax.ShapeDtypeStruct((bsz, nh, GDN_DK, LANE), F32)],
        scratch_shapes=[pltpu.VMEM((hb, GDN_DK, LANE), F32)],
        compiler_params=_params(("parallel", "parallel", "arbitrary")),
        name="gdn_core_prompt",
    )(qkv, qkv, qkv, brep, gcrep, proj, norm_g.reshape(1, LANE))
    return og, s_out


def _gdn_sample_kernel(q_ref, k_ref, v_ref, b_ref, g_ref, z_ref, ng_ref, s_ref, og_ref, sout_ref, o_scr):
    ns = q_ref.shape[0]
    qt = q_ref[...].T
    kt = k_ref[...].T
    lane = lax.broadcasted_iota(jnp.int32, qt.shape, 1)

    def body(i, carry):
        pick = lane == i
        qcol = jnp.sum(jnp.where(pick, qt, 0.0), axis=1, keepdims=True)
        kcol = jnp.sum(jnp.where(pick, kt, 0.0), axis=1, keepdims=True)
        v = v_ref[pl.ds(i, 1), :]
        beta = b_ref[pl.ds(i, 1), :]
        eg = jnp.exp(g_ref[pl.ds(i, 1), :])
        sd = s_ref[i, 0] * eg
        ks = jnp.sum(sd * kcol, axis=0, keepdims=True)
        v_new = beta * (v - ks)
        s_new = sd + kcol * v_new
        sout_ref[i, 0] = s_new
        o_scr[pl.ds(i, 1), :] = jnp.sum(s_new * qcol, axis=0, keepdims=True)
        return carry

    lax.fori_loop(0, ns, body, 0, unroll=8)
    og_ref[...] = (_rms(o_scr[...], ng_ref[...]) * _silu(z_ref[...].astype(F32))).astype(og_ref.dtype)


def gdn_core_sample(qkv_s, brep_s, grep_s, proj, n_prompt, state, norm_g):
    ns, nh = state.shape[0], state.shape[1]
    row_blk = n_prompt // ns
    blk = pl.BlockSpec((ns, LANE), lambda h: (0, h))
    st = pl.BlockSpec((ns, 1, GDN_DK, LANE), lambda h: (0, h, 0, 0))
    return pl.pallas_call(
        _gdn_sample_kernel,
        grid=(nh,),
        in_specs=[
            blk,
            pl.BlockSpec((ns, LANE), lambda h: (0, nh + h)),
            pl.BlockSpec((ns, LANE), lambda h: (0, 2 * nh + h)),
            blk,
            blk,
            pl.BlockSpec((ns, LANE), lambda h: (row_blk, 3 * nh + h)),
            pl.BlockSpec((1, LANE), lambda h: (0, 0)),
            st,
        ],
        out_specs=[blk, st],
        out_shape=[jax.ShapeDtypeStruct((ns, nh * LANE), BF16), jax.ShapeDtypeStruct(state.shape, F32)],
        scratch_shapes=[pltpu.VMEM((ns, LANE), F32)],
        compiler_params=_params(("parallel",)),
        name="gdn_core_sample",
    )(qkv_s, qkv_s, qkv_s, brep_s, grep_s, proj, norm_g.reshape(1, LANE), state)


def _peer_ple(x, p16, layer, norm_ffn, norm_ple, w_q16, keys, emb_u16, emb_v16, ple_proj16, ple_gate16):
    h_t = rmsnorm(x, norm_ffn[layer], BF16, transpose=True)
    r2, e2, n, e1 = peer_topk(h_t, w_q16, keys[layer], layer)
    o = peer_dense(h_t, emb_u16, emb_v16, r2, e2, n, e1, layer)
    x2, hp = add_norm(x, o, norm_ple[layer])
    return matmul(hp, ple_gate16, layer=layer, mode="ple", extras=(x2, p16[layer], ple_proj16))


def _chunk_tril(tm, c):
    r = jnp.arange(tm)
    return ((r[:, None] >= r[None, :]) & (r[:, None] // c == r[None, :] // c)).astype(F32)


def kernel(x_prompt, x_sample, p_prompt, p_sample, state_s5_re, state_s5_im, state_gdn, state_gdn_conv, norm_mix, norm_ffn, norm_ple, norm_final, ev_w_in, s5_a_re, s5_a_im, s5_log_dt, s5_b_re, s5_b_im, s5_c_re, s5_c_im, s5_d, s5_w_glu, sgu_ln_g, sgu_ln_b, sgu_w, sgu_b, ev_w_out, gdn_w_in, gdn_conv_w, gdn_a_log, gdn_dt_bias, gdn_norm_g, gdn_w_out, peer_w_q, peer_keys, peer_u, peer_v, ple_proj, ple_gate):
    bsz, seq, d = x_prompt.shape
    ns = x_sample.shape[0]
    n_prompt = bsz * seq
    x = jnp.concatenate([x_prompt.reshape(n_prompt, d), x_sample.reshape(ns, d)], axis=0)
    p16 = jnp.concatenate([p_prompt.reshape(2, n_prompt, -1), p_sample.reshape(2, ns, -1)], axis=1).astype(BF16)

    peer_w = (peer_w_q.astype(BF16), peer_keys, peer_u.astype(BF16), peer_v.astype(BF16),
              ple_proj.astype(BF16), ple_gate.astype(BF16))

    h = rmsnorm(x, norm_mix[0], BF16)
    proj = matmul(h, ev_w_in.astype(BF16))
    s5p = _s5_params(s5_a_re[0], s5_a_im[0], s5_log_dt[0], s5_b_re[0], s5_b_im[0], s5_c_re[0], s5_c_im[0])
    z_p, s5re_p, s5im_p = s5_prompt(proj, n_prompt, seq, s5p, s5_d[0])
    z_s, s5re_s, s5im_s = s5_sample(proj, n_prompt, state_s5_re[0], state_s5_im[0], s5p, s5_d[0])
    z = jnp.concatenate([z_p, z_s], axis=0)
    ya = matmul(z.astype(BF16), s5_w_glu.astype(BF16), mode="glu", extras=(z,), out_dtype=BF16)
    yb, v_last = sgu(proj, n_prompt, seq, sgu_ln_g[0], sgu_ln_b[0], sgu_w[0], sgu_b[0])
    x = matmul(ya, ev_w_out.astype(BF16), mode="resid2", extras=(x, yb))
    x = _peer_ple(x, p16, 0, norm_ffn, norm_ple, *peer_w)

    nh = GDN_HEADS
    n_ch = gdn_conv_w.shape[-1]
    n_qkvz = n_ch + nh * LANE
    h = rmsnorm(x, norm_mix[1], BF16)
    proj = matmul(h, gdn_w_in.astype(BF16), n_out=n_qkvz, out_dtype=BF16, tn=2048)
    w_ab = jnp.pad(gdn_w_in[:, :, n_qkvz:], ((0, 0), (0, 0), (0, LANE - 2 * nh))).astype(BF16)
    ab = matmul(h, w_ab)
    tm_g = 512
    brep_p, gcrep_p = gdn_gates(ab[:n_prompt], gdn_a_log[0], gdn_dt_bias[0], _chunk_tril(tm_g, GDN_CHUNK))
    brep_s, grep_s = gdn_gates(ab[n_prompt:], gdn_a_log[0], gdn_dt_bias[0], jnp.eye(ns, dtype=F32))
    qkv_p = gdn_conv_prompt(proj, n_prompt, seq, gdn_conv_w[0], n_ch)
    buf = state_gdn_conv[0]
    qkv_s = gdn_conv_sample(proj, n_prompt, jnp.swapaxes(buf, 0, 1), gdn_conv_w[0], n_ch)
    og_p, gdn_p = gdn_core_prompt(qkv_p, brep_p, gcrep_p, proj, n_prompt, seq, gdn_norm_g[0])
    og_s, gdn_s = gdn_core_sample(qkv_s, brep_s, grep_s, proj, n_prompt, state_gdn[0], gdn_norm_g[0])
    x = matmul(jnp.concatenate([og_p, og_s], axis=0), gdn_w_out.astype(BF16), mode="resid", extras=(x,))
    x = _peer_ple(x, p16, 1, norm_ffn, norm_ple, *peer_w)

    y_p, y_s = rmsnorm_split(x, norm_final, n_prompt)

    conv_p = jnp.stack([proj[(b + 1) * seq - (GDN_CONV - 1):(b + 1) * seq, :n_ch] for b in range(bsz)]).astype(F32)
    conv_s = jnp.concatenate([buf[:, 1:], proj[n_prompt:, None, :n_ch].astype(F32)], axis=1)
    n_v = bsz * SGU_CHUNK
    return (
        y_p.reshape(bsz, seq, d),
        y_s.reshape(ns, 1, d),
        s5re_p[None], s5im_p[None], s5re_s[None], s5im_s[None],
        v_last[:n_v].reshape(1, bsz, SGU_CHUNK, -1),
        v_last[n_v:].reshape(1, ns, 1, -1),
        gdn_p[None], gdn_s[None],
        conv_p[None], conv_s[None],
    )
```

```python
import functools

import jax
import jax.numpy as jnp
from jax import lax
from jax.experimental import pallas as pl
from jax.experimental.pallas import tpu as pltpu

F32 = jnp.float32
BF16 = jnp.bfloat16
EPS = 1e-6
LANE = 128
V7X_VMEM_BYTES = 64 * 1024 * 1024
VMEM_LIMIT = V7X_VMEM_BYTES - 8 * 1024 * 1024

S5_GROUP = 16
S5_STATE = 64
SGU_GROUPS = 8
SGU_CHUNK = 128
GDN_HEADS = 32
GDN_DK = 128
GDN_CONV = 4
GDN_CHUNK = 64
PEER_HEADS = 8
PEER_NKEYS = 128
PEER_TOPK = 16
PEER_BLOCK_KEYS = 4
GATE_DTYPE = jnp.float32

TOK_TILE = 640
ROW_TILE = 128


def _params(sem, vmem=VMEM_LIMIT):
    return pltpu.CompilerParams(dimension_semantics=sem, vmem_limit_bytes=vmem)


def _dot(a, b):
    return jnp.dot(a, b, preferred_element_type=F32)


def _dot_nt(a, b):
    return lax.dot_general(a, b, (((1,), (1,)), ((), ())), preferred_element_type=F32)


def _dot_tn(a, b):
    return lax.dot_general(a, b, (((0,), (0,)), ((), ())), preferred_element_type=F32)


def _rms(x, g):
    return x * lax.rsqrt(jnp.mean(x * x, axis=-1, keepdims=True) + EPS) * g


def _silu(x):
    return x * jax.nn.sigmoid(x)


def _rmsnorm_kernel(x_ref, g_ref, o_ref, *, transpose):
    y = _rms(x_ref[...], g_ref[...])
    if transpose:
        y = y.T
    o_ref[...] = y.astype(o_ref.dtype)


def rmsnorm(x, g, out_dtype, transpose=False):
    t, d = x.shape
    tm = ROW_TILE
    if transpose:
        out_shape, out_spec = (d, t), pl.BlockSpec((d, tm), lambda i: (0, i))
    else:
        out_shape, out_spec = (t, d), pl.BlockSpec((tm, d), lambda i: (i, 0))
    return pl.pallas_call(
        functools.partial(_rmsnorm_kernel, transpose=transpose),
        grid=(t // tm,),
        in_specs=[pl.BlockSpec((tm, d), lambda i: (i, 0)), pl.BlockSpec((1, d), lambda i: (0, 0))],
        out_specs=out_spec,
        out_shape=jax.ShapeDtypeStruct(out_shape, out_dtype),
        compiler_params=_params(("parallel",)),
        name="rmsnorm_t" if transpose else "rmsnorm",
    )(x, g.reshape(1, d))


def _rmsnorm_split_kernel(x_ref, g_ref, head_ref, tail_ref, *, n_head_blocks):
    i = pl.program_id(0)
    y = _rms(x_ref[...], g_ref[...])

    @pl.when(i < n_head_blocks)
    def _():
        head_ref[...] = y

    @pl.when(i >= n_head_blocks)
    def _():
        tail_ref[...] = y


def rmsnorm_split(x, g, n_head):
    t, d = x.shape
    tm = ROW_TILE
    nhb = n_head // tm
    return pl.pallas_call(
        functools.partial(_rmsnorm_split_kernel, n_head_blocks=nhb),
        grid=(t // tm,),
        in_specs=[pl.BlockSpec((tm, d), lambda i: (i, 0)), pl.BlockSpec((1, d), lambda i: (0, 0))],
        out_specs=[pl.BlockSpec((tm, d), lambda i: (jnp.minimum(i, nhb - 1), 0)),
                   pl.BlockSpec((tm, d), lambda i: (jnp.maximum(i - nhb, 0), 0))],
        out_shape=[jax.ShapeDtypeStruct((n_head, d), F32), jax.ShapeDtypeStruct((t - n_head, d), F32)],
        compiler_params=_params(("arbitrary",)),
        name="rmsnorm_split",
    )(x, g.reshape(1, d))


def _add_norm_kernel(x_ref, o_ref, g_ref, x2_ref, h_ref):
    x2 = x_ref[...] + o_ref[...]
    x2_ref[...] = x2
    h_ref[...] = _rms(x2, g_ref[...]).astype(h_ref.dtype)


def add_norm(x, o, g):
    t, d = x.shape
    tm = ROW_TILE
    row = pl.BlockSpec((tm, d), lambda i: (i, 0))
    return pl.pallas_call(
        _add_norm_kernel,
        grid=(t // tm,),
        in_specs=[row, row, pl.BlockSpec((1, d), lambda i: (0, 0))],
        out_specs=[row, row],
        out_shape=[jax.ShapeDtypeStruct((t, d), F32), jax.ShapeDtypeStruct((t, d), BF16)],
        compiler_params=_params(("parallel",)),
        name="add_norm",
    )(x, o, g.reshape(1, d))


def _mm_kernel(*refs, mode):
    x_ref, w_ref = refs[0], refs[1]
    o_ref = refs[-1]
    if mode == "resid2":
        x2_ref = refs[3]
        k1 = x_ref.shape[1]
        acc = refs[2][...] + (_dot(x_ref[...], w_ref[:k1, :]) + _dot(x2_ref[...], w_ref[k1:, :]))
        o_ref[...] = acc.astype(o_ref.dtype)
        return
    acc = _dot(x_ref[...], w_ref[...])
    if mode == "resid":
        acc = refs[2][...] + acc
    elif mode == "glu":
        z = refs[2][...]
        acc = z * jax.nn.sigmoid(acc)
    elif mode == "ple":
        r_ref, p_ref, pw_ref = refs[2], refs[3], refs[4]
        acc = r_ref[...] + jax.nn.sigmoid(acc) * _dot(p_ref[...], pw_ref[...])
    o_ref[...] = acc.astype(o_ref.dtype)


def matmul(x, w, *, layer=0, n_out=None, mode="plain", extras=(), out_dtype=F32, tn=1024):
    m, k = x.shape
    n = w.shape[2] if n_out is None else n_out
    tm = TOK_TILE
    tn = min(tn, n)
    in_specs = [pl.BlockSpec((tm, k), lambda j, i: (i, 0)),
                pl.BlockSpec((None, w.shape[1], tn), lambda j, i: (layer, 0, j))]
    if mode in ("resid", "resid2", "glu", "ple"):
        in_specs.append(pl.BlockSpec((tm, tn), lambda j, i: (i, j)))
    if mode == "resid2":
        in_specs.append(pl.BlockSpec((tm, extras[1].shape[1]), lambda j, i: (i, 0)))
    if mode == "ple":
        kp = extras[1].shape[1]
        in_specs += [pl.BlockSpec((tm, kp), lambda j, i: (i, 0)),
                     pl.BlockSpec((None, kp, tn), lambda j, i: (layer, 0, j))]
    return pl.pallas_call(
        functools.partial(_mm_kernel, mode=mode),
        grid=(n // tn, m // tm),
        in_specs=in_specs,
        out_specs=pl.BlockSpec((tm, tn), lambda j, i: (i, j)),
        out_shape=jax.ShapeDtypeStruct((m, n), out_dtype),
        compiler_params=_params(("parallel", "parallel")),
        name="mm_" + mode,
    )(x, w, *extras)


def _s5_prompt_kernel(x0, x1, x2, x3, bcat_ref, ccat_ref, lre_ref, lim_ref, d_ref,
                      z_ref, hre_out, him_out, u_scr, bu_scr, h_scr, y_scr, st_re, st_im, *, c):
    tc = pl.program_id(1)

    @pl.when(tc == 0)
    def _():
        st_re[...] = jnp.zeros_like(st_re)
        st_im[...] = jnp.zeros_like(st_im)

    xs = (x0, x1, x2, x3)
    half_w = st_re.shape[1]
    zeros = jnp.zeros((c, LANE), F32)
    for b in range(4):
        for half in range(2):
            s = 2 * b + half
            u_scr[half, pl.ds(s, c, stride=8), :] = xs[b][:, half * LANE:(half + 1) * LANE]
            u_scr[1 - half, pl.ds(s, c, stride=8), :] = zeros
    bu_scr[...] = _dot(jnp.concatenate([u_scr[0], u_scr[1]], axis=1), bcat_ref[0])
    lr = lre_ref[0]
    li = lim_ref[0]
    hr = st_re[...]
    hi = st_im[...]
    for t in range(c):
        rows = slice(8 * t, 8 * t + 8)
        nr = lr * hr - li * hi + bu_scr[rows, :half_w]
        ni = lr * hi + li * hr + bu_scr[rows, half_w:]
        h_scr[rows, :half_w] = nr
        h_scr[rows, half_w:] = ni
        hr, hi = nr, ni
    st_re[...] = hr
    st_im[...] = hi
    hre_out[0] = hr
    him_out[0] = hi
    y = _dot(h_scr[...], ccat_ref[0])
    odd = lax.broadcasted_iota(jnp.int32, (8 * c, LANE), 0) % 2 == 1
    y_scr[...] = jnp.where(odd, y[:, LANE:], y[:, :LANE])
    for b in range(4):
        for half in range(2):
            s = 2 * b + half
            ls = slice(half * LANE, (half + 1) * LANE)
            ys = y_scr[pl.ds(s, c, stride=8), :]
            z_ref[b, :, ls] = jax.nn.gelu(ys + d_ref[:, ls] * xs[b][:, ls])


def _s5_sample_kernel(x_ref, h0re_ref, h0im_ref, bblk_ref, cblk_ref, lre_ref, lim_ref, d_ref,
                      z_ref, hre_out, him_out):
    x = x_ref[...]
    half_w = h0re_ref.shape[1]
    bu = _dot(x, bblk_ref[0])
    lr = lre_ref[0]
    li = lim_ref[0]
    h0r = h0re_ref[...]
    h0i = h0im_ref[...]
    nr = lr * h0r - li * h0i + bu[:, :half_w]
    ni = lr * h0i + li * h0r + bu[:, half_w:]
    hre_out[...] = nr
    him_out[...] = ni
    y = _dot(nr, cblk_ref[0, :half_w, :]) + _dot(ni, cblk_ref[0, half_w:, :]) + d_ref[...] * x
    z_ref[...] = jax.nn.gelu(y)


def _s5_params(a_re, a_im, log_dt, b_re, b_im, c_re, c_im):
    g, p = a_re.shape
    nb = g // 8
    lam = lax.complex(a_re, a_im)
    dt = jnp.exp(log_dt)[:, None]
    lam_bar = jnp.exp(lam * dt)
    b_bar = ((lam_bar - 1.0) / lam)[..., None] * lax.complex(b_re, b_im)
    eye = jnp.eye(8, dtype=F32)

    def bmat(v):
        v = jnp.swapaxes(v, 1, 2).reshape(nb, 8, S5_GROUP, p)
        return jnp.einsum("kjcp,jl->kjclp", v, eye).reshape(nb, 8 * S5_GROUP, 8 * p)

    def cmat(v):
        v = jnp.swapaxes(v, 1, 2).reshape(nb, 8, p, S5_GROUP)
        return jnp.einsum("kjpc,jl->kjplc", v, eye).reshape(nb, 8 * p, 8 * S5_GROUP)

    bblk = jnp.concatenate([bmat(jnp.real(b_bar)), bmat(jnp.imag(b_bar))], axis=2)
    cblk = jnp.concatenate([cmat(c_re), cmat(-c_im)], axis=1)
    lre = jnp.real(lam_bar).reshape(nb, 8 * p)
    lim = jnp.imag(lam_bar).reshape(nb, 8 * p)
    return bblk, cblk, lre, lim


def s5_prompt(proj, n_prompt, seq, params, d_skip, c=128):
    bblk, cblk, lre, lim = params
    nb, _, two_w = bblk.shape
    half_w = two_w // 2
    bsz = n_prompt // seq
    assert bsz == 4
    w = nb * 8 * S5_GROUP
    ngb = nb // 2
    nt = seq // c
    lre_t = jnp.tile(lre.reshape(ngb, 2, half_w), (1, bsz, 1))
    lim_t = jnp.tile(lim.reshape(ngb, 2, half_w), (1, bsz, 1))
    bcat = bblk.reshape(ngb, 2 * LANE, two_w)
    ccat = jnp.swapaxes(cblk.reshape(ngb, 2, two_w, LANE), 1, 2).reshape(ngb, two_w, 2 * LANE)
    x_specs = [pl.BlockSpec((c, 2 * LANE), functools.partial(lambda gb, tc, b: (b * nt + tc, gb), b=b))
               for b in range(bsz)]
    z, hre, him = pl.pallas_call(
        functools.partial(_s5_prompt_kernel, c=c),
        grid=(ngb, nt),
        in_specs=x_specs + [
            pl.BlockSpec((1, 2 * LANE, two_w), lambda gb, tc: (gb, 0, 0)),
            pl.BlockSpec((1, two_w, 2 * LANE), lambda gb, tc: (gb, 0, 0)),
            pl.BlockSpec((1, 8, half_w), lambda gb, tc: (gb, 0, 0)),
            pl.BlockSpec((1, 8, half_w), lambda gb, tc: (gb, 0, 0)),
            pl.BlockSpec((1, 2 * LANE), lambda gb, tc: (0, gb)),
        ],
        out_specs=[
            pl.BlockSpec((bsz, c, 2 * LANE), lambda gb, tc: (0, tc, gb)),
            pl.BlockSpec((1, 8, half_w), lambda gb, tc: (gb, 0, 0)),
            pl.BlockSpec((1, 8, half_w), lambda gb, tc: (gb, 0, 0)),
        ],
        out_shape=[
            jax.ShapeDtypeStruct((bsz, seq, w), F32),
            jax.ShapeDtypeStruct((ngb, 8, half_w), F32),
            jax.ShapeDtypeStruct((ngb, 8, half_w), F32),
        ],
        scratch_shapes=[pltpu.VMEM((2, 8 * c, LANE), F32), pltpu.VMEM((8 * c, two_w), F32),
                        pltpu.VMEM((8 * c, two_w), F32), pltpu.VMEM((8 * c, LANE), F32),
                        pltpu.VMEM((8, half_w), F32), pltpu.VMEM((8, half_w), F32)],
        compiler_params=_params(("parallel", "arbitrary")),
        name="s5_prompt",
    )(proj, proj, proj, proj, bcat, ccat, lre_t, lim_t, d_skip.reshape(1, w))

    def states(h):
        h = h.reshape(ngb, bsz, 2, 8, S5_STATE)
        return jnp.transpose(h, (1, 0, 2, 3, 4)).reshape(bsz, nb * 8, S5_STATE)

    return z.reshape(n_prompt, w), states(hre), states(him)


def s5_sample(proj, n_prompt, h0_re, h0_im, params, d_skip):
    bblk, cblk, lre, lim = params
    nb, _, two_w = bblk.shape
    half_w = two_w // 2
    ns = h0_re.shape[0]
    w = nb * 8 * S5_GROUP
    row_blk = n_prompt // ns
    z, hre, him = pl.pallas_call(
        _s5_sample_kernel,
        grid=(nb,),
        in_specs=[
            pl.BlockSpec((ns, LANE), lambda k: (row_blk, k)),
            pl.BlockSpec((ns, half_w), lambda k: (0, k)),
            pl.BlockSpec((ns, half_w), lambda k: (0, k)),
            pl.BlockSpec((1, LANE, two_w), lambda k: (k, 0, 0)),
            pl.BlockSpec((1, two_w, LANE), lambda k: (k, 0, 0)),
            pl.BlockSpec((1, 1, half_w), lambda k: (k, 0, 0)),
            pl.BlockSpec((1, 1, half_w), lambda k: (k, 0, 0)),
            pl.BlockSpec((1, LANE), lambda k: (0, k)),
        ],
        out_specs=[
            pl.BlockSpec((ns, LANE), lambda k: (0, k)),
            pl.BlockSpec((ns, half_w), lambda k: (0, k)),
            pl.BlockSpec((ns, half_w), lambda k: (0, k)),
        ],
        out_shape=[
            jax.ShapeDtypeStruct((ns, w), F32),
            jax.ShapeDtypeStruct((ns, nb * half_w), F32),
            jax.ShapeDtypeStruct((ns, nb * half_w), F32),
        ],
        compiler_params=_params(("parallel",)),
        name="s5_sample",
    )(proj, h0_re.reshape(ns, -1), h0_im.reshape(ns, -1), bblk, cblk,
      lre.reshape(nb, 1, half_w), lim.reshape(nb, 1, half_w), d_skip.reshape(1, w))
    return z, hre.reshape(ns, nb * 8, S5_STATE), him.reshape(ns, nb * 8, S5_STATE)


def _sgu_kernel(u_ref, v_ref, g_ref, b_ref, w_ref, bias_ref, y_ref, vl_ref):
    u = jax.nn.gelu(u_ref[...])
    v = jax.nn.gelu(v_ref[...])
    mu = jnp.mean(v, axis=-1, keepdims=True)
    vc = v - mu
    vn = vc * lax.rsqrt(jnp.mean(vc * vc, axis=-1, keepdims=True) + EPS) * g_ref[...] + b_ref[...]
    vl_ref[...] = vn
    c = u.shape[0]
    dg = u.shape[1] // SGU_GROUPS
    causal = lax.broadcasted_iota(jnp.int32, (c, c), 0) >= lax.broadcasted_iota(jnp.int32, (c, c), 1)
    for g in range(SGU_GROUPS):
        ls = slice(g * dg, (g + 1) * dg)
        w = jnp.where(causal, w_ref[0, g], 0.0).astype(BF16)
        mixed = _dot(w, vn[:, ls].astype(BF16)) + bias_ref[0, g]
        y_ref[:, ls] = (u[:, ls] * mixed).astype(y_ref.dtype)


def sgu(proj, n_prompt, seq, ln_g, ln_b, w_s, b_s):
    t = proj.shape[0]
    wdt = proj.shape[1] // 3
    c = SGU_CHUNK
    dg = wdt // SGU_GROUPS
    n_chunks = t // c
    per_seq = seq // c
    n_prompt_chunks = n_prompt // c
    eye = jnp.eye(c, dtype=F32)
    w_sets = jnp.stack([w_s, w_s[:, :1, :1] * eye])
    bias_sets = jnp.stack([jnp.broadcast_to(b_s[:, :, None], (SGU_GROUPS, c, dg)),
                           jnp.broadcast_to(b_s[:, :1, None], (SGU_GROUPS, c, dg))])
    n_last = n_prompt // seq + (n_chunks - n_prompt_chunks)
    y, vl = pl.pallas_call(
        _sgu_kernel,
        grid=(n_chunks,),
        in_specs=[
            pl.BlockSpec((c, wdt), lambda i: (i, 1)),
            pl.BlockSpec((c, wdt), lambda i: (i, 2)),
            pl.BlockSpec((1, wdt), lambda i: (0, 0)),
            pl.BlockSpec((1, wdt), lambda i: (0, 0)),
            pl.BlockSpec((1, SGU_GROUPS, c, c), lambda i: (i // n_prompt_chunks, 0, 0, 0)),
            pl.BlockSpec((1, SGU_GROUPS, c, dg), lambda i: (i // n_prompt_chunks, 0, 0, 0)),
        ],
        out_specs=[
            pl.BlockSpec((c, wdt), lambda i: (i, 0)),
            pl.BlockSpec((c, wdt), lambda i: (i // per_seq, 0)),
        ],
        out_shape=[jax.ShapeDtypeStruct((t, wdt), BF16), jax.ShapeDtypeStruct((n_last * c, wdt), F32)],
        compiler_params=_params(("arbitrary",)),
        name="sgu",
    )(proj, proj, ln_g.reshape(1, wdt), ln_b.reshape(1, wdt), w_sets, bias_sets)
    return y, vl


def _extract16(arrays):
    l = arrays[0].shape[1]
    viota = lax.broadcasted_iota(jnp.int32, (PEER_TOPK, l), 0)

    def body(r, carry):
        out = []
        for s, vals, idxs in carry:
            iota = lax.broadcasted_iota(jnp.int32, s.shape, 0)
            m = jnp.max(s, axis=0, keepdims=True)
            idx = jnp.min(jnp.where(s == m, iota, s.shape[0]), axis=0, keepdims=True)
            out.append((jnp.where(iota == idx, -jnp.inf, s), jnp.where(viota == r, m, vals),
                        jnp.where(viota == r, idx, idxs)))
        return tuple(out)

    init = tuple((s, jnp.zeros((PEER_TOPK, l), F32), jnp.zeros((PEER_TOPK, l), jnp.int32)) for s in arrays)
    return lax.fori_loop(0, PEER_TOPK, body, init)


def _ranks(idxs, n):
    iota = lax.broadcasted_iota(jnp.int32, (n, idxs.shape[1]), 0)
    rank = jnp.full(iota.shape, PEER_TOPK, jnp.int32)
    for r in range(PEER_TOPK):
        rank = jnp.where(iota == idxs[r:r + 1], r, rank)
    return rank


def _peer_candidates(t1, t2):
    k = PEER_TOPK
    sub = lax.broadcasted_iota(jnp.int32, (8, t1.shape[1]), 0)
    groups = [t1[0:1] + t2[0:8], t1[0:1] + t2[8:16], t1[1:2] + t2[0:8]]
    slices = [slice(0, 16), slice(16, 24)]
    for i in range(2, 8):
        groups.append(jnp.where(sub < k // (i + 1), t1[i:i + 1] + t2[0:8], -jnp.inf))
        slices.append(slice(8 * (i + 1), 8 * (i + 2)))
    groups.append(t1[8:16] + t2[0:1])
    slices += [slice(64 + i, 65 + i) for i in range(8, k)]
    return jnp.concatenate(groups, axis=0), slices


def _peer_topk_kernel(ht_ref, wq_ref, keys_ref, r2_ref, e2_ref, n_ref, e1_ref):
    qt = _dot_tn(wq_ref[...], ht_ref[...]).astype(BF16)
    nk = keys_ref.shape[2]
    dq = keys_ref.shape[3]
    s1_all = _dot(keys_ref[0, 0].astype(BF16), qt[:dq])
    s2_all = _dot(keys_ref[0, 1].astype(BF16), qt[dq:])
    for c in range(ht_ref.shape[1] // LANE):
        cs = slice(c * LANE, (c + 1) * LANE)
        s1 = s1_all[:, cs]
        s2 = s2_all[:, cs]
        (_, t1, idx1), (_, t2, idx2) = _extract16([s1, s2])
        rank2 = _ranks(idx2, nk)
        cand, cand_rows = _peer_candidates(t1, t2)
        (cand_left, _, _), = _extract16([cand])
        selected = (cand_left == -jnp.inf) & (cand > -jnp.inf)
        cmax = t1[0:1] + t2[0:1]
        zsum = jnp.sum(jnp.where(selected, jnp.exp(cand - cmax), 0.0), axis=0, keepdims=True)
        n = jnp.zeros((nk, LANE), F32)
        row = lax.broadcasted_iota(jnp.int32, (nk, LANE), 0)
        for i in range(PEER_TOPK):
            cnt = jnp.sum(jnp.where(selected[cand_rows[i]], 1.0, 0.0), axis=0, keepdims=True)
            n = jnp.where(row == idx1[i:i + 1], cnt, n)
        r2_ref[0, :, cs] = rank2.astype(F32).astype(r2_ref.dtype)
        e1 = jnp.exp(s1 - t1[0:1])
        kb = n_ref.shape[2]
        for a in range(nk):
            n_ref[0, a // kb, a % kb:a % kb + 1, cs] = n[a:a + 1]
            e1_ref[0, a // kb, a % kb:a % kb + 1, cs] = e1[a:a + 1]
        e2_ref[0, :, cs] = (jnp.exp(s2 - t2[0:1]) / zsum).astype(e2_ref.dtype)


def peer_topk(h_t, wq, keys, layer):
    d, t = h_t.shape
    nh, _, nk, dq = keys.shape
    tt = TOK_TILE
    kb = PEER_BLOCK_KEYS
    spec = pl.BlockSpec((1, nk, tt), lambda j, h: (h, 0, j))
    shp = jax.ShapeDtypeStruct((nh, nk, t), GATE_DTYPE)
    spec_a = pl.BlockSpec((1, nk // kb, kb, tt), lambda j, h: (h, 0, 0, j))
    shp_a = jax.ShapeDtypeStruct((nh, nk // kb, kb, t), GATE_DTYPE)
    return pl.pallas_call(
        _peer_topk_kernel,
        grid=(t // tt, nh),
        in_specs=[
            pl.BlockSpec((d, tt), lambda j, h: (0, j)),
            pl.BlockSpec((None, d, 2 * dq), lambda j, h: (layer, 0, h)),
            pl.BlockSpec((1, 2, nk, dq), lambda j, h: (h, 0, 0, 0)),
        ],
        out_specs=[spec, spec, spec_a, spec_a],
        out_shape=[shp, shp, shp_a, shp_a],
        compiler_params=_params(("parallel", "arbitrary")),
        name="peer_topk",
    )(h_t, wq, keys)


def _peer_dense_kernel(ht_ref, u_ref, v_ref, r2_ref, e2_ref, n_ref, e1_ref, o_ref, s_a, s_b, act_scr,
                       *, na, nh, nb):
    i = pl.program_id(1)
    tt = ht_ref.shape[1]
    nk = r2_ref.shape[1]
    d = o_ref.shape[1]
    cb = 512
    tiles = [(a, c) for a in range(na) for c in range(tt // LANE)]

    @pl.when(i == 0)
    def _():
        o_ref[...] = jnp.zeros_like(o_ref)

    def step(s_prev, s_cur):
        @pl.when(i < nb)
        def _():
            s_cur[...] = _dot(u_ref[...], ht_ref[...])

        @pl.when(i > 0)
        def _():
            for a, c in tiles:
                rs = slice(a * nk, (a + 1) * nk)
                cs = slice(c * LANE, (c + 1) * LANE)
                w = jnp.zeros((nk, LANE), F32)
                for h in range(nh):
                    nrow = n_ref[h, 0, a:a + 1, cs]
                    e1row = e1_ref[h, 0, a:a + 1, cs]
                    w = w + jnp.where(r2_ref[h, :, cs] < nrow, e2_ref[h, :, cs] * e1row, 0.0)
                act_scr[cs, rs] = (jax.nn.gelu(s_prev[rs, cs]) * w).T.astype(BF16)
            for r in range(d // cb):
                o_ref[:, r * cb:(r + 1) * cb] += _dot(act_scr[...], v_ref[:, r * cb:(r + 1) * cb])

    @pl.when(i % 2 == 0)
    def _():
        step(s_b, s_a)

    @pl.when(i % 2 == 1)
    def _():
        step(s_a, s_b)


def peer_dense(h_t, u, v, r2, e2, n, e1, layer):
    d, t = h_t.shape
    e = u.shape[1]
    nh, nk, _ = r2.shape
    tt = TOK_TILE
    na = PEER_BLOCK_KEYS
    ne = na * nk
    nb = e // ne
    assert n.shape == (nh, nk // na, na, t) and e1.shape == n.shape
    once = pl.Buffered(1)

    def score_blk(i):
        return jnp.minimum(i, nb - 1)

    def value_blk(i):
        return jnp.maximum(i - 1, 0)

    key_spec = pl.BlockSpec((nh, 1, na, tt), lambda j, i: (0, value_blk(i), 0, j))
    return pl.pallas_call(
        functools.partial(_peer_dense_kernel, na=na, nh=nh, nb=nb),
        grid=(t // tt, nb + 1),
        in_specs=[
            pl.BlockSpec((d, tt), lambda j, i: (0, j), pipeline_mode=once),
            pl.BlockSpec((None, ne, d), lambda j, i: (layer, score_blk(i), 0)),
            pl.BlockSpec((None, ne, d), lambda j, i: (layer, value_blk(i), 0)),
            pl.BlockSpec((nh, nk, tt), lambda j, i: (0, 0, j), pipeline_mode=once),
            pl.BlockSpec((nh, nk, tt), lambda j, i: (0, 0, j), pipeline_mode=once),
            key_spec,
            key_spec,
        ],
        out_specs=pl.BlockSpec((tt, d), lambda j, i: (j, 0)),
        out_shape=jax.ShapeDtypeStruct((t, d), F32),
        scratch_shapes=[pltpu.VMEM((ne, tt), F32), pltpu.VMEM((ne, tt), F32), pltpu.VMEM((tt, ne), BF16)],
        compiler_params=_params(("parallel", "arbitrary")),
        name="peer_dense",
    )(h_t, u, v, r2, e2, n, e1)


def _softplus(x):
    return jnp.maximum(x, 0.0) + jnp.log1p(jnp.exp(-jnp.abs(x)))


def _replicate(x, onehot):
    hi = x.astype(BF16)
    rest = x - hi.astype(F32)
    mid = rest.astype(BF16)
    lo = (rest - mid.astype(F32)).astype(BF16)
    sel = onehot.astype(BF16)
    return (_dot(hi, sel) + _dot(mid, sel)) + _dot(lo, sel)


def _gdn_gates_kernel(ab_ref, alog_ref, dtb_ref, tril_ref, eg_ref, eb_ref, brep_ref, gcrep_ref):
    ab = ab_ref[...]
    g = -jnp.exp(alog_ref[...]) * _softplus(ab + dtb_ref[...])
    beta = jax.nn.sigmoid(ab)
    gc = jnp.dot(tril_ref[...], g, preferred_element_type=F32, precision=lax.Precision.HIGHEST)
    gcrep_ref[...] = _replicate(gc, eg_ref[...])
    brep_ref[...] = _replicate(beta, eb_ref[...])


def gdn_gates(ab, a_log, dt_bias, tril):
    rows = ab.shape[0]
    tm = tril.shape[0]
    nh = a_log.shape[0]
    wide = nh * LANE
    pad = LANE - nh
    alog_p = jnp.pad(a_log, (0, pad)).reshape(1, LANE)
    dtb_p = jnp.pad(dt_bias, (0, pad)).reshape(1, LANE)
    head_of_col = jnp.arange(wide) // LANE
    lane = jnp.arange(LANE)[:, None]
    e_g = (lane == head_of_col[None, :]).astype(F32)
    e_b = (lane == head_of_col[None, :] + nh).astype(F32)
    tn = 1024
    return pl.pallas_call(
        _gdn_gates_kernel,
        grid=(rows // tm, wide // tn),
        in_specs=[
            pl.BlockSpec((tm, LANE), lambda i, j: (i, 0)),
            pl.BlockSpec((1, LANE), lambda i, j: (0, 0)),
            pl.BlockSpec((1, LANE), lambda i, j: (0, 0)),
            pl.BlockSpec((tm, tm), lambda i, j: (0, 0)),
            pl.BlockSpec((LANE, tn), lambda i, j: (0, j)),
            pl.BlockSpec((LANE, tn), lambda i, j: (0, j)),
        ],
        out_specs=[pl.BlockSpec((tm, tn), lambda i, j: (i, j)), pl.BlockSpec((tm, tn), lambda i, j: (i, j))],
        out_shape=[jax.ShapeDtypeStruct((rows, wide), F32), jax.ShapeDtypeStruct((rows, wide), F32)],
        compiler_params=_params(("parallel", "parallel")),
        name="gdn_gates",
    )(ab, alog_p, dtb_p, tril, e_g, e_b)


def _gdn_post_conv(y, o_ref, cb, n_qk_blocks, n_q_blocks):
    y = _silu(y)
    is_qk = cb < n_qk_blocks
    qscale = jnp.where(cb < n_q_blocks, GDN_DK ** -0.5, 1.0)
    for hh in range(y.shape[1] // GDN_DK):
        ls = slice(hh * GDN_DK, (hh + 1) * GDN_DK)
        seg = y[:, ls]
        rs = lax.rsqrt(jnp.sum(seg * seg, axis=-1, keepdims=True) + EPS)
        o_ref[:, ls] = (seg * jnp.where(is_qk, rs * qscale, 1.0)).astype(o_ref.dtype)


def _gdn_conv_kernel(x_ref, w_ref, o_ref, ext, *, tm, n_qk_blocks, n_q_blocks):
    cb = pl.program_id(0)
    tt = pl.program_id(2)

    @pl.when(tt == 0)
    def _():
        ext[0:8, :] = jnp.zeros((8, ext.shape[1]), F32)

    ext[8:8 + tm, :] = x_ref[...].astype(F32)
    w = w_ref[...]
    y = w[0:1] * ext[5:5 + tm, :]
    for tap in range(1, GDN_CONV):
        y = y + w[tap:tap + 1] * ext[5 + tap:5 + tap + tm, :]
    ext[0:8, :] = ext[tm:tm + 8, :]
    _gdn_post_conv(y, o_ref, cb, n_qk_blocks, n_q_blocks)


def gdn_conv_prompt(proj, n_prompt, seq, conv_w, n_ch, tm=512, tc=1024):
    bsz = n_prompt // seq
    nt = seq // tm
    qk = (2 * n_ch) // 3
    return pl.pallas_call(
        functools.partial(_gdn_conv_kernel, tm=tm, n_qk_blocks=qk // tc, n_q_blocks=qk // 2 // tc),
        grid=(n_ch // tc, bsz, nt),
        in_specs=[pl.BlockSpec((tm, tc), lambda cb, b, tt: (b * nt + tt, cb)),
                  pl.BlockSpec((GDN_CONV, tc), lambda cb, b, tt: (0, cb))],
        out_specs=pl.BlockSpec((tm, tc), lambda cb, b, tt: (b * nt + tt, cb)),
        out_shape=jax.ShapeDtypeStruct((n_prompt, n_ch), BF16),
        scratch_shapes=[pltpu.VMEM((tm + 8, tc), F32)],
        compiler_params=_params(("parallel", "parallel", "arbitrary")),
        name="gdn_conv_prompt",
    )(proj, conv_w)


def _gdn_conv_sample_kernel(x_ref, buf_ref, w_ref, o_ref, *, n_qk_blocks, n_q_blocks):
    cb = pl.program_id(0)
    w = w_ref[...]
    y = w[0:1] * buf_ref[0]
    for tap in range(1, GDN_CONV - 1):
        y = y + w[tap:tap + 1] * buf_ref[tap]
    y = y + w[GDN_CONV - 1:GDN_CONV] * x_ref[...].astype(F32)
    _gdn_post_conv(y, o_ref, cb, n_qk_blocks, n_q_blocks)


def gdn_conv_sample(proj, n_prompt, buf_t, conv_w, n_ch, tc=1024):
    ns = buf_t.shape[1]
    row_blk = n_prompt // ns
    qk = (2 * n_ch) // 3
    return pl.pallas_call(
        functools.partial(_gdn_conv_sample_kernel, n_qk_blocks=qk // tc, n_q_blocks=qk // 2 // tc),
        grid=(n_ch // tc,),
        in_specs=[pl.BlockSpec((ns, tc), lambda cb: (row_blk, cb)),
                  pl.BlockSpec((GDN_CONV - 1, ns, tc), lambda cb: (0, 0, cb)),
                  pl.BlockSpec((GDN_CONV, tc), lambda cb: (0, cb))],
        out_specs=pl.BlockSpec((ns, tc), lambda cb: (0, cb)),
        out_shape=jax.ShapeDtypeStruct((ns, n_ch), F32),
        compiler_params=_params(("parallel",)),
        name="gdn_conv_sample",
    )(proj, buf_t, conv_w)


def _unit_lower_inverses(mats):
    c = mats[0].shape[0]
    row = lax.broadcasted_iota(jnp.int32, (c, c), 0)
    col = lax.broadcasted_iota(jnp.int32, (c, c), 1)
    eye = jnp.where(row == col, 1.0, 0.0)
    blk = 16
    ds = [jnp.where(row // blk == col // blk, a, 0.0) for a in mats]
    d2 = [_dot(d, d) for d in ds]
    d4 = [_dot(d, d) for d in d2]
    d8 = [_dot(d, d) for d in d4]
    ts = [_dot(eye - d, eye + x) for d, x in zip(ds, d2)]
    ts = [_dot(t, eye + x) for t, x in zip(ts, d4)]
    ts = [_dot(t, eye + x) for t, x in zip(ts, d8)]
    while blk < c:
        off_mask = (row // (2 * blk) == col // (2 * blk)) & (row // blk != col // blk)
        tmp = [_dot(t, jnp.where(off_mask, a, 0.0)) for t, a in zip(ts, mats)]
        tmp = [_dot(x, t) for x, t in zip(tmp, ts)]
        ts = [t - x for t, x in zip(ts, tmp)]
        blk *= 2
    return ts


def _gdn_chunks(qs, ks, vs, brs, gcs, zs, ng, ss):
    c = qs[0].shape[0]
    dv = vs[0].shape[1]
    row = lax.broadcasted_iota(jnp.int32, (c, c), 0)
    col = lax.broadcasted_iota(jnp.int32, (c, c), 1)
    kb = [k * b for k, b in zip(ks, brs)]
    vb = [v * b for v, b in zip(vs, brs)]
    eg = [jnp.exp(g) for g in gcs]
    glast = [g[c - 1:c, :] for g in gcs]
    kbg = [x * e for x, e in zip(kb, eg)]
    qg = [q * e for q, e in zip(qs, eg)]
    kdec = [k * jnp.exp(gl - g) for k, gl, g in zip(ks, glast, gcs)]
    decay = [jnp.exp(jnp.where(row >= col, g[:, :c] - g.T[:c, :], -jnp.inf)) for g in gcs]
    k16 = [k.astype(BF16) for k in ks]
    a = [_dot_nt(x.astype(BF16), k) * jnp.where(row > col, d, 0.0) for x, k, d in zip(kb, k16, decay)]
    attn = [_dot_nt(q.astype(BF16), k) * d for q, k, d in zip(qs, k16, decay)]
    ts = _unit_lower_inverses(a)
    uw = [_dot(t, jnp.concatenate([x, y], axis=1)) for t, x, y in zip(ts, vb, kbg)]
    s16 = [s.astype(BF16) for s in ss]
    v_new = [x[:, :dv] - _dot(x[:, dv:].astype(BF16), s) for x, s in zip(uw, s16)]
    vn16 = [x.astype(BF16) for x in v_new]
    o_state = [_dot(x.astype(BF16), s) for x, s in zip(qg, s16)]
    o_local = [_dot(x.astype(BF16), v) for x, v in zip(attn, vn16)]
    s_new = [s * jnp.exp(gl) + _dot_tn(x.astype(BF16), v) for s, gl, x, v in zip(ss, glast, kdec, vn16)]
    og = [_rms(x + y, ng) * _silu(z) for x, y, z in zip(o_state, o_local, zs)]
    return og, s_new


def _gdn_core_kernel(q_ref, k_ref, v_ref, b_ref, gc_ref, z_ref, ng_ref, og_ref, sout_ref, s_scr, *, hb, nck, c):
    n = pl.program_id(2)

    @pl.when(n == 0)
    def _():
        s_scr[...] = jnp.zeros_like(s_scr)

    ng = ng_ref[...]
    ss = [s_scr[hh] for hh in range(hb)]
    for ck in range(nck):
        rs = slice(ck * c, (ck + 1) * c)

        def heads(ref):
            return [ref[rs, hh * LANE:(hh + 1) * LANE].astype(F32) for hh in range(hb)]

        og, ss = _gdn_chunks(heads(q_ref), heads(k_ref), heads(v_ref), heads(b_ref), heads(gc_ref), heads(z_ref),
                             ng, ss)
        for hh in range(hb):
            og_ref[rs, hh * LANE:(hh + 1) * LANE] = og[hh].astype(og_ref.dtype)
    for hh in range(hb):
        s_scr[hh] = ss[hh]
        sout_ref[0, hh] = ss[hh]


def gdn_core_prompt(qkv, brep, gcrep, proj, n_prompt, seq, norm_g, hb=GDN_HEADS, nck=1):
    c = GDN_CHUNK
    nh = GDN_HEADS
    bsz = n_prompt // seq
    tm = c * nck
    nt = seq // tm
    bw = hb * LANE
    nhb = nh // hb

    def rows(b, h, n):
        return b * nt + n

    og, s_out = pl.pallas_call(
        functools.partial(_gdn_core_kernel, hb=hb, nck=nck, c=c),
        grid=(bsz, nhb, nt),
        in_specs=[
            pl.BlockSpec((tm, bw), lambda b, h, n: (rows(b, h, n), h)),
            pl.BlockSpec((tm, bw), lambda b, h, n: (rows(b, h, n), nhb + h)),
            pl.BlockSpec((tm, bw), lambda b, h, n: (rows(b, h, n), 2 * nhb + h)),
            pl.BlockSpec((tm, bw), lambda b, h, n: (rows(b, h, n), h)),
            pl.BlockSpec((tm, bw), lambda b, h, n: (rows(b, h, n), h)),
            pl.BlockSpec((tm, bw), lambda b, h, n: (rows(b, h, n), 3 * nhb + h)),
            pl.BlockSpec((1, LANE), lambda b, h, n: (0, 0)),
        ],
        out_specs=[
            pl.BlockSpec((tm, bw), lambda b, h, n: (rows(b, h, n), h)),
            pl.BlockSpec((1, hb, GDN_DK, LANE), lambda b, h, n: (b, h, 0, 0)),
        ],
        out_shape=[jax.ShapeDtypeStruct((n_prompt, nh * LANE), BF16),
                   jax.ShapeDtypeStruct((bsz, nh, GDN_DK, LANE), F32)],
        scratch_shapes=[pltpu.VMEM((hb, GDN_DK, LANE), F32)],
        compiler_params=_params(("parallel", "parallel", "arbitrary")),
        name="gdn_core_prompt",
    )(qkv, qkv, qkv, brep, gcrep, proj, norm_g.reshape(1, LANE))
    return og, s_out


def _gdn_sample_kernel(q_ref, k_ref, v_ref, b_ref, g_ref, z_ref, ng_ref, s_ref, og_ref, sout_ref, o_scr):
    ns = q_ref.shape[0]
    qt = q_ref[...].T
    kt = k_ref[...].T
    lane = lax.broadcasted_iota(jnp.int32, qt.shape, 1)

    def body(i, carry):
        pick = lane == i
        qcol = jnp.sum(jnp.where(pick, qt, 0.0), axis=1, keepdims=True)
        kcol = jnp.sum(jnp.where(pick, kt, 0.0), axis=1, keepdims=True)
        v = v_ref[pl.ds(i, 1), :]
        beta = b_ref[pl.ds(i, 1), :]
        eg = jnp.exp(g_ref[pl.ds(i, 1), :])
        sd = s_ref[i, 0] * eg
        ks = jnp.sum(sd * kcol, axis=0, keepdims=True)
        v_new = beta * (v - ks)
        s_new = sd + kcol * v_new
        sout_ref[i, 0] = s_new
        o_scr[pl.ds(i, 1), :] = jnp.sum(s_new * qcol, axis=0, keepdims=True)
        return carry

    lax.fori_loop(0, ns, body, 0, unroll=8)
    og_ref[...] = (_rms(o_scr[...], ng_ref[...]) * _silu(z_ref[...].astype(F32))).astype(og_ref.dtype)


def gdn_core_sample(qkv_s, brep_s, grep_s, proj, n_prompt, state, norm_g):
    ns, nh = state.shape[0], state.shape[1]
    row_blk = n_prompt // ns
    blk = pl.BlockSpec((ns, LANE), lambda h: (0, h))
    st = pl.BlockSpec((ns, 1, GDN_DK, LANE), lambda h: (0, h, 0, 0))
    return pl.pallas_call(
        _gdn_sample_kernel,
        grid=(nh,),
        in_specs=[
            blk,
            pl.BlockSpec((ns, LANE), lambda h: (0, nh + h)),
            pl.BlockSpec((ns, LANE), lambda h: (0, 2 * nh + h)),
            blk,
            blk,
            pl.BlockSpec((ns, LANE), lambda h: (row_blk, 3 * nh + h)),
            pl.BlockSpec((1, LANE), lambda h: (0, 0)),
            st,
        ],
        out_specs=[blk, st],
        out_shape=[jax.ShapeDtypeStruct((ns, nh * LANE), BF16), jax.ShapeDtypeStruct(state.shape, F32)],
        scratch_shapes=[pltpu.VMEM((ns, LANE), F32)],
        compiler_params=_params(("parallel",)),
        name="gdn_core_sample",
    )(qkv_s, qkv_s, qkv_s, brep_s, grep_s, proj, norm_g.reshape(1, LANE), state)


def _peer_ple(x, p16, layer, norm_ffn, norm_ple, w_q16, keys, emb_u16, emb_v16, ple_proj16, ple_gate16):
    h_t = rmsnorm(x, norm_ffn[layer], BF16, transpose=True)
    r2, e2, n, e1 = peer_topk(h_t, w_q16, keys[layer], layer)
    o = peer_dense(h_t, emb_u16, emb_v16, r2, e2, n, e1, layer)
    x2, hp = add_norm(x, o, norm_ple[layer])
    return matmul(hp, ple_gate16, layer=layer, mode="ple", extras=(x2, p16[layer], ple_proj16))


def _chunk_tril(tm, c):
    r = jnp.arange(tm)
    return ((r[:, None] >= r[None, :]) & (r[:, None] // c == r[None, :] // c)).astype(F32)


def kernel(x_prompt, x_sample, p_prompt, p_sample, state_s5_re, state_s5_im, state_gdn, state_gdn_conv, norm_mix, norm_ffn, norm_ple, norm_final, ev_w_in, s5_a_re, s5_a_im, s5_log_dt, s5_b_re, s5_b_im, s5_c_re, s5_c_im, s5_d, s5_w_glu, sgu_ln_g, sgu_ln_b, sgu_w, sgu_b, ev_w_out, gdn_w_in, gdn_conv_w, gdn_a_log, gdn_dt_bias, gdn_norm_g, gdn_w_out, peer_w_q, peer_keys, peer_u, peer_v, ple_proj, ple_gate):
    bsz, seq, d = x_prompt.shape
    ns = x_sample.shape[0]
    n_prompt = bsz * seq
    x = jnp.concatenate([x_prompt.reshape(n_prompt, d), x_sample.reshape(ns, d)], axis=0)
    p16 = jnp.concatenate([p_prompt.reshape(2, n_prompt, -1), p_sample.reshape(2, ns, -1)], axis=1).astype(BF16)

    peer_w = (peer_w_q.astype(BF16), peer_keys, peer_u.astype(BF16), peer_v.astype(BF16),
              ple_proj.astype(BF16), ple_gate.astype(BF16))

    h = rmsnorm(x, norm_mix[0], BF16)
    proj = matmul(h, ev_w_in.astype(BF16))
    s5p = _s5_params(s5_a_re[0], s5_a_im[0], s5_log_dt[0], s5_b_re[0], s5_b_im[0], s5_c_re[0], s5_c_im[0])
    z_p, s5re_p, s5im_p = s5_prompt(proj, n_prompt, seq, s5p, s5_d[0])
    z_s, s5re_s, s5im_s = s5_sample(proj, n_prompt, state_s5_re[0], state_s5_im[0], s5p, s5_d[0])
    z = jnp.concatenate([z_p, z_s], axis=0)
    ya = matmul(z.astype(BF16), s5_w_glu.astype(BF16), mode="glu", extras=(z,), out_dtype=BF16)
    yb, v_last = sgu(proj, n_prompt, seq, sgu_ln_g[0], sgu_ln_b[0], sgu_w[0], sgu_b[0])
    x = matmul(ya, ev_w_out.astype(BF16), mode="resid2", extras=(x, yb))
    x = _peer_ple(x, p16, 0, norm_ffn, norm_ple, *peer_w)

    nh = GDN_HEADS
    n_ch = gdn_conv_w.shape[-1]
    n_qkvz = n_ch + nh * LANE
    h = rmsnorm(x, norm_mix[1], BF16)
    proj = matmul(h, gdn_w_in.astype(BF16), n_out=n_qkvz, out_dtype=BF16, tn=2048)
    w_ab = jnp.pad(gdn_w_in[:, :, n_qkvz:], ((0, 0), (0, 0), (0, LANE - 2 * nh))).astype(BF16)
    ab = matmul(h, w_ab)
    tm_g = 512
    brep_p, gcrep_p = gdn_gates(ab[:n_prompt], gdn_a_log[0], gdn_dt_bias[0], _chunk_tril(tm_g, GDN_CHUNK))
    brep_s, grep_s = gdn_gates(ab[n_prompt:], gdn_a_log[0], gdn_dt_bias[0], jnp.eye(ns, dtype=F32))
    qkv_p = gdn_conv_prompt(proj, n_prompt, seq, gdn_conv_w[0], n_ch)
    buf = state_gdn_conv[0]
    qkv_s = gdn_conv_sample(proj, n_prompt, jnp.swapaxes(buf, 0, 1), gdn_conv_w[0], n_ch)
    og_p, gdn_p = gdn_core_prompt(qkv_p, brep_p, gcrep_p, proj, n_prompt, seq, gdn_norm_g[0])
    og_s, gdn_s = gdn_core_sample(qkv_s, brep_s, grep_s, proj, n_prompt, state_gdn[0], gdn_norm_g[0])
    x = matmul(jnp.concatenate([og_p, og_s], axis=0), gdn_w_out.astype(BF16), mode="resid", extras=(x,))
    x = _peer_ple(x, p16, 1, norm_ffn, norm_ple, *peer_w)

    y_p, y_s = rmsnorm_split(x, norm_final, n_prompt)

    conv_p = jnp.stack([proj[(b + 1) * seq - (GDN_CONV - 1):(b + 1) * seq, :n_ch] for b in range(bsz)]).astype(F32)
    conv_s = jnp.concatenate([buf[:, 1:], proj[n_prompt:, None, :n_ch].astype(F32)], axis=1)
    n_v = bsz * SGU_CHUNK
    return (
        y_p.reshape(bsz, seq, d),
        y_s.reshape(ns, 1, d),
        s5re_p[None], s5im_p[None], s5re_s[None], s5im_s[None],
        v_last[:n_v].reshape(1, bsz, SGU_CHUNK, -1),
        v_last[n_v:].reshape(1, ns, 1, -1),
        gdn_p[None], gdn_s[None],
        conv_p[None], conv_s[None],
    )
```
